```python
import jax, jax.numpy as jnp
from jax import lax
import numpy as np

D_MODEL = 2048
BATCH = 4
SEQ = 2048
DEPTH = 2

CHUNK = 64
N_MIXERS = 2
EPS = 1e-6

GLA_HEADS = 4
GLA_DK = D_MODEL // 2
GLA_DV = D_MODEL
GLA_DK_HEAD = GLA_DK // GLA_HEADS
GLA_DV_HEAD = GLA_DV // GLA_HEADS
GLA_GATE_RANK = 16
GLA_TAU = 16.0
GLA_IN = 2 * GLA_DK + 2 * GLA_DV + GLA_GATE_RANK

SGU_WIDTH = D_MODEL
SGU_BLOCK = 128
SGU_GROUPS = 8
SGU_GROUP_DIM = SGU_WIDTH // SGU_GROUPS
SGU_IN = 3 * SGU_WIDTH

N_GLA_LAYERS = (DEPTH + 1) // 2
N_SGU_LAYERS = DEPTH // 2

kernel_name = "hybrid_gla_sgu_sandwich_trunk"


def rmsnorm(x, gain):
    xf = x.astype(jnp.float32)
    y = xf * lax.rsqrt(jnp.mean(xf * xf, axis=-1, keepdims=True) + EPS)
    return (y * gain.astype(jnp.float32)).astype(x.dtype)


def gla_mixer(h, w_in, w_gate2, b_gate, o_gain, w_out):
    B, S, _ = h.shape
    nc = S // CHUNK
    proj = h @ w_in
    q, k, v, g, glr = jnp.split(
        proj, [GLA_DK, 2 * GLA_DK, 2 * GLA_DK + GLA_DV, 2 * GLA_DK + 2 * GLA_DV], axis=-1)
    log_a = jax.nn.log_sigmoid((glr @ w_gate2 + b_gate).astype(jnp.float32)) / GLA_TAU

    def to_chunks(t, dh):
        return t.astype(jnp.float32).reshape(B, nc, CHUNK, GLA_HEADS, dh).transpose(1, 0, 3, 2, 4)

    qc = to_chunks(q, GLA_DK_HEAD) * (GLA_DK_HEAD ** -0.5)
    kc = to_chunks(k, GLA_DK_HEAD)
    vc = to_chunks(v, GLA_DV_HEAD)
    la = to_chunks(log_a, GLA_DK_HEAD)
    bcum = jnp.cumsum(la, axis=3)
    b_end = bcum[:, :, :, -1:, :]
    k_dec = kc * jnp.exp(b_end - bcum)
    decay = jnp.exp(b_end[:, :, :, 0, :])

    def step(state, xs):
        q_i, k_i, v_i, d_i = xs
        state = state * d_i[..., None] + jnp.einsum('bhck,bhcv->bhkv', k_i, v_i)
        return state, jnp.einsum('bhck,bhkv->bhcv', q_i, state)

    s0 = jnp.zeros((B, GLA_HEADS, GLA_DK_HEAD, GLA_DV_HEAD), jnp.float32)
    _, o = lax.scan(step, s0, (qc, k_dec, vc, decay))
    o = o.transpose(1, 0, 3, 2, 4).reshape(B, S, GLA_HEADS, GLA_DV_HEAD)
    o = o * lax.rsqrt(jnp.mean(o * o, axis=-1, keepdims=True) + EPS)
    o = o.reshape(B, S, GLA_DV) * o_gain.astype(jnp.float32)
    o = o.astype(h.dtype) * jax.nn.silu(g)
    return o @ w_out


def sgu_mixer(h, w_in, ln_gain, ln_bias, w_spatial, b_spatial, w_out):
    B, S, _ = h.shape
    nb = S // SGU_BLOCK
    proj = h @ w_in
    u, v, g = jnp.split(proj, 3, axis=-1)
    u = jax.nn.gelu(u)
    vf = jax.nn.gelu(v).astype(jnp.float32)
    mu = jnp.mean(vf, axis=-1, keepdims=True)
    var = jnp.mean(jnp.square(vf - mu), axis=-1, keepdims=True)
    vn = (vf - mu) * lax.rsqrt(var + EPS) * ln_gain.astype(jnp.float32) + ln_bias.astype(jnp.float32)
    vn = vn.reshape(B, nb, SGU_BLOCK, SGU_GROUPS, SGU_GROUP_DIM)
    pos_chunk = jnp.arange(SGU_BLOCK) // CHUNK
    mask = pos_chunk[:, None] >= pos_chunk[None, :]
    ws = jnp.where(mask[None], w_spatial, 0).astype(jnp.float32)
    vs = jnp.einsum('gij,bnjgd->bnigd', ws, vn) \
        + b_spatial.astype(jnp.float32).T[None, None, :, :, None]
    vs = vs.reshape(B, S, SGU_WIDTH).astype(h.dtype)
    return (u * vs * jax.nn.silu(g)) @ w_out


def setup_inputs(seed: int = 0) -> dict:
    key = jax.random.key(seed)
    ks = jax.random.split(key, 15)
    f32 = jnp.float32
    nrm = lambda k, shape, scale: jax.random.normal(k, shape, f32) * scale
    return {
        "x": nrm(ks[0], (BATCH, SEQ, D_MODEL), 1.0),
        "norm_pre": 1.0 + nrm(ks[1], (DEPTH, D_MODEL), 0.02),
        "norm_post": 1.0 + nrm(ks[2], (DEPTH, D_MODEL), 0.02),
        "gla_w_in": nrm(ks[3], (N_GLA_LAYERS, D_MODEL, GLA_IN), D_MODEL ** -0.5),
        "gla_w_gate2": nrm(ks[4], (N_GLA_LAYERS, GLA_GATE_RANK, GLA_DK), GLA_GATE_RANK ** -0.5),
        "gla_b_gate": nrm(ks[5], (N_GLA_LAYERS, GLA_DK), 0.1),
        "gla_o_gain": 1.0 + nrm(ks[6], (N_GLA_LAYERS, GLA_DV), 0.02),
        "gla_w_out": nrm(ks[7], (N_GLA_LAYERS, GLA_DV, D_MODEL), GLA_DV ** -0.5),
        "sgu_w_in": nrm(ks[8], (N_SGU_LAYERS, D_MODEL, SGU_IN), D_MODEL ** -0.5),
        "sgu_ln_gain": 1.0 + nrm(ks[9], (N_SGU_LAYERS, SGU_WIDTH), 0.02),
        "sgu_ln_bias": nrm(ks[10], (N_SGU_LAYERS, SGU_WIDTH), 0.02),
        "sgu_w_spatial": nrm(ks[11], (N_SGU_LAYERS, SGU_GROUPS, SGU_BLOCK, SGU_BLOCK), SGU_BLOCK ** -0.5),
        "sgu_b_spatial": 1.0 + nrm(ks[12], (N_SGU_LAYERS, SGU_GROUPS, SGU_BLOCK), 0.1),
        "sgu_w_out": nrm(ks[13], (N_SGU_LAYERS, SGU_WIDTH, D_MODEL), SGU_WIDTH ** -0.5),
    }


def reference(x, norm_pre, norm_post, gla_w_in, gla_w_gate2, gla_b_gate, gla_o_gain,
              gla_w_out, sgu_w_in, sgu_ln_gain, sgu_ln_bias, sgu_w_spatial, sgu_b_spatial,
              sgu_w_out):
    for i in range(DEPTH):
        h = rmsnorm(x, norm_pre[i])
        j = i // N_MIXERS
        if i % N_MIXERS == 0:
            y = gla_mixer(h, gla_w_in[j], gla_w_gate2[j], gla_b_gate[j], gla_o_gain[j], gla_w_out[j])
        else:
            y = sgu_mixer(h, sgu_w_in[j], sgu_ln_gain[j], sgu_ln_bias[j], sgu_w_spatial[j],
                          sgu_b_spatial[j], sgu_w_out[j])
        x = x + rmsnorm(y, norm_post[i])
    return x
```

```python
import jax
import jax.numpy as jnp
from jax import lax
from jax.experimental import pallas as pl
from jax.experimental.pallas import tpu as pltpu

F32 = jnp.float32
BF16 = jnp.bfloat16

D_MODEL = 2048
EPS = 1e-6
CHUNK = 64

GLA_HEADS = 4
GLA_DK = D_MODEL // 2
GLA_DV = D_MODEL
GLA_DKH = GLA_DK // GLA_HEADS
GLA_DVH = GLA_DV // GLA_HEADS
GLA_RANK = 16
GLA_INV_TAU = 1.0 / 16.0
GLA_MAIN = 2 * GLA_DK + 2 * GLA_DV

SGU_WIDTH = D_MODEL
SGU_BLOCK = 128
SGU_GROUPS = 8
SGU_GD = SGU_WIDTH // SGU_GROUPS

LANES = 128
VMEM_LIMIT = 56 * 1024 * 1024

INPROJ_TM = 1024
INPROJ_TN = 1024
OUT_TM = 512
SGU_TM = 512


def _rms(x, gain):
    return x * lax.rsqrt(jnp.mean(x * x, axis=-1, keepdims=True) + EPS) * gain


def _inproj_kernel(x_ref, gain_ref, w_ref, o_ref, h_ref):
    @pl.when(pl.program_id(1) == 0)
    def _():
        h_ref[...] = _rms(x_ref[...], gain_ref[...]).astype(BF16)

    o_ref[...] = jnp.dot(h_ref[...], w_ref[...], preferred_element_type=F32).astype(o_ref.dtype)


def _inproj_lr_kernel(x_ref, gain_ref, w_ref, wlr_ref, o_ref, lr_ref, h_ref):
    @pl.when(pl.program_id(1) == 0)
    def _():
        h = _rms(x_ref[...], gain_ref[...]).astype(BF16)
        h_ref[...] = h
        lr_ref[...] = jnp.dot(h, wlr_ref[...], preferred_element_type=F32)

    o_ref[...] = jnp.dot(h_ref[...], w_ref[...], preferred_element_type=F32).astype(o_ref.dtype)


def _inproj(x2, gain, w, wlr=None):
    t, d = x2.shape
    n = w.shape[1]
    tm, tn = INPROJ_TM, INPROJ_TN
    grid = (t // tm, n // tn)
    x_spec = pl.BlockSpec((tm, d), lambda i, j: (i, 0))
    g_spec = pl.BlockSpec((1, d), lambda i, j: (0, 0))
    w_spec = pl.BlockSpec((d, tn), lambda i, j: (0, j))
    o_spec = pl.BlockSpec((tm, tn), lambda i, j: (i, j))
    params = pltpu.CompilerParams(
        dimension_semantics=("arbitrary", "arbitrary"), vmem_limit_bytes=VMEM_LIMIT)
    scratch = [pltpu.VMEM((tm, d), BF16)]
    if wlr is None:
        return pl.pallas_call(
            _inproj_kernel,
            grid=grid,
            in_specs=[x_spec, g_spec, w_spec],
            out_specs=o_spec,
            out_shape=jax.ShapeDtypeStruct((t, n), BF16),
            scratch_shapes=scratch,
            compiler_params=params,
            name="sgu_inproj",
        )(x2, gain, w)
    lr_w_spec = pl.BlockSpec((d, LANES), lambda i, j: (0, 0))
    lr_o_spec = pl.BlockSpec((tm, LANES), lambda i, j: (i, 0))
    return pl.pallas_call(
        _inproj_lr_kernel,
        grid=grid,
        in_specs=[x_spec, g_spec, w_spec, lr_w_spec],
        out_specs=[o_spec, lr_o_spec],
        out_shape=[jax.ShapeDtypeStruct((t, n), BF16), jax.ShapeDtypeStruct((t, LANES), F32)],
        scratch_shapes=scratch,
        compiler_params=params,
        name="gla_inproj",
    )(x2, gain, w, wlr)


def _gla_kernel(q_ref, k_ref, v_ref, g_ref, lr_ref, w2_ref, bg_ref, og_ref, tri_ref,
                a_ref, s_ref, la_ref):
    z = jnp.dot(lr_ref[0].astype(BF16), w2_ref[...], preferred_element_type=F32) + bg_ref[...]
    la_ref[...] = (jnp.minimum(z, 0.0) - jnp.log1p(jnp.exp(-jnp.abs(z)))) * GLA_INV_TAU
    s_ref[...] = jnp.zeros_like(s_ref)
    n_chunks = q_ref.shape[1] // CHUNK

    def chunk(c, carry):
        r0 = pl.multiple_of(c * CHUNK, CHUNK)
        rows = pl.ds(r0, CHUNK)
        la = la_ref[rows, :]
        hi = la.astype(BF16)
        lo = (la - hi.astype(F32)).astype(BF16)
        tri = tri_ref[...]
        bcum = (jnp.dot(tri, hi, preferred_element_type=F32)
                + jnp.dot(tri, lo, preferred_element_type=F32))
        b_end = bcum[CHUNK - 1:CHUNK, :]
        k_dec = (k_ref[0, rows, :].astype(F32) * jnp.exp(b_end - bcum)).astype(BF16)
        decay = jnp.exp(b_end)
        decay_col = jnp.transpose(jnp.broadcast_to(decay, (LANES, GLA_DKH)))
        upd = lax.dot_general(k_dec, v_ref[0, rows, :], (((0,), (0,)), ((), ())),
                              preferred_element_type=F32)
        for t in range(GLA_DVH // LANES):
            cols = slice(t * LANES, (t + 1) * LANES)
            s_ref[:, cols] = s_ref[:, cols] * decay_col + upd[:, cols]
        o = jnp.dot(q_ref[0, rows, :], s_ref[...].astype(BF16),
                    preferred_element_type=F32) * (GLA_DKH ** -0.5)
        o = _rms(o, og_ref[...])
        g = g_ref[0, rows, :].astype(F32)
        a_ref[0, rows, :] = (o * (g * jax.nn.sigmoid(g))).astype(a_ref.dtype)
        return carry

    lax.fori_loop(0, n_chunks, chunk, 0, unroll=2)


def _gla_scan(proj3, lr3, w2, b_gate, o_gain, tri):
    b, s, _ = proj3.shape
    k_off = GLA_DK // GLA_DKH
    v_off = (2 * GLA_DK) // GLA_DVH
    g_off = (2 * GLA_DK + GLA_DV) // GLA_DVH
    in_specs = [
        pl.BlockSpec((1, s, GLA_DKH), lambda i, h: (i, 0, h)),
        pl.BlockSpec((1, s, GLA_DKH), lambda i, h: (i, 0, k_off + h)),
        pl.BlockSpec((1, s, GLA_DVH), lambda i, h: (i, 0, v_off + h)),
        pl.BlockSpec((1, s, GLA_DVH), lambda i, h: (i, 0, g_off + h)),
        pl.BlockSpec((1, s, LANES), lambda i, h: (i, 0, 0)),
        pl.BlockSpec((LANES, GLA_DKH), lambda i, h: (0, h)),
        pl.BlockSpec((1, GLA_DKH), lambda i, h: (0, h)),
        pl.BlockSpec((1, GLA_DVH), lambda i, h: (0, h)),
        pl.BlockSpec((CHUNK, CHUNK), lambda i, h: (0, 0)),
    ]
    return pl.pallas_call(
        _gla_kernel,
        grid=(b, GLA_HEADS),
        in_specs=in_specs,
        out_specs=pl.BlockSpec((1, s, GLA_DVH), lambda i, h: (i, 0, h)),
        out_shape=jax.ShapeDtypeStruct((b, s, GLA_DV), BF16),
        scratch_shapes=[pltpu.VMEM((GLA_DKH, GLA_DVH), F32), pltpu.VMEM((s, GLA_DKH), F32)],
        compiler_params=pltpu.CompilerParams(
            dimension_semantics=("arbitrary", "arbitrary"), vmem_limit_bytes=VMEM_LIMIT),
        name="gla_scan",
    )(proj3, proj3, proj3, proj3, lr3, w2, b_gate, o_gain, tri)


def _outproj_kernel(a_ref, w_ref, gain_ref, x_ref, o_ref):
    y = jnp.dot(a_ref[...], w_ref[...], preferred_element_type=F32)
    o_ref[...] = x_ref[...] + _rms(y, gain_ref[...])


def _outproj(a2, w, gain, x2):
    t, d = x2.shape
    k = a2.shape[1]
    tm = OUT_TM
    return pl.pallas_call(
        _outproj_kernel,
        grid=(t // tm,),
        in_specs=[
            pl.BlockSpec((tm, k), lambda i: (i, 0)),
            pl.BlockSpec((k, d), lambda i: (0, 0)),
            pl.BlockSpec((1, d), lambda i: (0, 0)),
            pl.BlockSpec((tm, d), lambda i: (i, 0)),
        ],
        out_specs=pl.BlockSpec((tm, d), lambda i: (i, 0)),
        out_shape=jax.ShapeDtypeStruct((t, d), F32),
        compiler_params=pltpu.CompilerParams(
            dimension_semantics=("arbitrary",), vmem_limit_bytes=VMEM_LIMIT),
        name="gla_outproj",
    )(a2, w, gain, x2)


def _sgu_kernel(u_ref, v_ref, g_ref, lng_ref, lnb_ref, ws_ref, bs_ref, w_ref, gain_ref, x_ref,
                o_ref, wsm_ref, z_ref):
    @pl.when(pl.program_id(0) == 0)
    def _():
        ri = lax.broadcasted_iota(jnp.int32, (SGU_BLOCK, SGU_BLOCK), 0) // CHUNK
        ci = lax.broadcasted_iota(jnp.int32, (SGU_BLOCK, SGU_BLOCK), 1) // CHUNK
        for gi in range(SGU_GROUPS):
            wsm_ref[gi] = jnp.where(ri >= ci, ws_ref[gi], 0.0).astype(BF16)

    v = jax.nn.gelu(v_ref[...].astype(F32))
    mu = jnp.mean(v, axis=-1, keepdims=True)
    vc = v - mu
    var = jnp.mean(vc * vc, axis=-1, keepdims=True)
    vn = (vc * lax.rsqrt(var + EPS) * lng_ref[...] + lnb_ref[...]).astype(BF16)
    tm = v.shape[0]
    for nb in range(tm // SGU_BLOCK):
        rows = slice(nb * SGU_BLOCK, (nb + 1) * SGU_BLOCK)
        for gi in range(SGU_GROUPS):
            cols = slice(gi * SGU_GD, (gi + 1) * SGU_GD)
            vs = jnp.dot(wsm_ref[gi], vn[rows, cols], preferred_element_type=F32) + bs_ref[:, cols]
            u = jax.nn.gelu(u_ref[rows, cols].astype(F32))
            g = g_ref[rows, cols].astype(F32)
            z_ref[rows, cols] = (u * vs * (g * jax.nn.sigmoid(g))).astype(BF16)
    y = jnp.dot(z_ref[...], w_ref[...], preferred_element_type=F32)
    o_ref[...] = x_ref[...] + _rms(y, gain_ref[...])


def _sgu(proj, ln_gain, ln_bias, w_spatial, bias_full, w_out, gain, x2):
    t, d = x2.shape
    tm = SGU_TM
    row = lambda i: (i, 0)
    const2 = lambda i: (0, 0)
    return pl.pallas_call(
        _sgu_kernel,
        grid=(t // tm,),
        in_specs=[
            pl.BlockSpec((tm, SGU_WIDTH), lambda i: (i, 0)),
            pl.BlockSpec((tm, SGU_WIDTH), lambda i: (i, 1)),
            pl.BlockSpec((tm, SGU_WIDTH), lambda i: (i, 2)),
            pl.BlockSpec((1, SGU_WIDTH), const2),
            pl.BlockSpec((1, SGU_WIDTH), const2),
            pl.BlockSpec((SGU_GROUPS, SGU_BLOCK, SGU_BLOCK), lambda i: (0, 0, 0)),
            pl.BlockSpec((SGU_BLOCK, SGU_WIDTH), const2),
            pl.BlockSpec((SGU_WIDTH, d), const2),
            pl.BlockSpec((1, d), const2),
            pl.BlockSpec((tm, d), row),
        ],
        out_specs=pl.BlockSpec((tm, d), row),
        out_shape=jax.ShapeDtypeStruct((t, d), F32),
        scratch_shapes=[
            pltpu.VMEM((SGU_GROUPS, SGU_BLOCK, SGU_BLOCK), BF16),
            pltpu.VMEM((tm, SGU_WIDTH), BF16),
        ],
        compiler_params=pltpu.CompilerParams(
            dimension_semantics=("arbitrary",), vmem_limit_bytes=VMEM_LIMIT),
        name="sgu_mix_outproj",
    )(proj, proj, proj, ln_gain, ln_bias, w_spatial, bias_full, w_out, gain, x2)


def kernel(x, norm_pre, norm_post, gla_w_in, gla_w_gate2, gla_b_gate, gla_o_gain, gla_w_out,
           sgu_w_in, sgu_ln_gain, sgu_ln_bias, sgu_w_spatial, sgu_b_spatial, sgu_w_out):
    b, s, d = x.shape
    t = b * s
    x2 = x.reshape(t, d)

    w_in = gla_w_in[0]
    w_main = w_in[:, :GLA_MAIN].astype(BF16)
    w_lr = jnp.pad(w_in[:, GLA_MAIN:], ((0, 0), (0, LANES - GLA_RANK))).astype(BF16)
    w2 = jnp.pad(gla_w_gate2[0], ((0, LANES - GLA_RANK), (0, 0))).astype(BF16)
    tri = jnp.tril(jnp.ones((CHUNK, CHUNK), F32)).astype(BF16)
    proj, lr = _inproj(x2, norm_pre[0:1], w_main, w_lr)
    a = _gla_scan(proj.reshape(b, s, GLA_MAIN), lr.reshape(b, s, LANES), w2,
                  gla_b_gate[0:1], gla_o_gain[0:1], tri)
    x2 = _outproj(a.reshape(t, GLA_DV), gla_w_out[0].astype(BF16), norm_post[0:1], x2)

    proj = _inproj(x2, norm_pre[1:2], sgu_w_in[0].astype(BF16))
    bias_full = jnp.repeat(sgu_b_spatial[0].T, SGU_GD, axis=1)
    x2 = _sgu(proj, sgu_ln_gain[0:1], sgu_ln_bias[0:1], sgu_w_spatial[0], bias_full,
              sgu_w_out[0].astype(BF16), norm_post[1:2], x2)
    return x2.reshape(b, s, d)
```

```python
import jax
import jax.numpy as jnp
from jax import lax
from jax.experimental import pallas as pl
from jax.experimental.pallas import tpu as pltpu

F32 = jnp.float32
BF16 = jnp.bfloat16

D_MODEL = 2048
EPS = 1e-6
CHUNK = 64

GLA_HEADS = 4
GLA_DK = D_MODEL // 2
GLA_DV = D_MODEL
GLA_DKH = GLA_DK // GLA_HEADS
GLA_DVH = GLA_DV // GLA_HEADS
GLA_RANK = 16
GLA_INV_TAU = 1.0 / 16.0
GLA_MAIN = 2 * GLA_DK + 2 * GLA_DV

SGU_WIDTH = D_MODEL
SGU_BLOCK = 128
SGU_GROUPS = 8
SGU_GD = SGU_WIDTH // SGU_GROUPS

LANES = 128
VMEM_LIMIT = 56 * 1024 * 1024

INPROJ_TM = 1024
INPROJ_TN = 1024
OUT_TM = 512
SGU_TM = 512
GLA_TILE = 512
GLA_TRI = 256


def _rms(x, gain):
    return x * lax.rsqrt(jnp.mean(x * x, axis=-1, keepdims=True) + EPS) * gain


def _inproj_kernel(x_ref, gain_ref, w_ref, o_ref, h_ref):
    @pl.when(pl.program_id(1) == 0)
    def _():
        h_ref[...] = _rms(x_ref[...], gain_ref[...]).astype(BF16)

    o_ref[...] = jnp.dot(h_ref[...], w_ref[...], preferred_element_type=F32).astype(o_ref.dtype)


def _inproj_lr_kernel(x_ref, gain_ref, w_ref, wlr_ref, o_ref, lr_ref, h_ref):
    @pl.when(pl.program_id(1) == 0)
    def _():
        h = _rms(x_ref[...], gain_ref[...]).astype(BF16)
        h_ref[...] = h
        lr_ref[...] = jnp.dot(h, wlr_ref[...], preferred_element_type=F32)

    o_ref[...] = jnp.dot(h_ref[...], w_ref[...], preferred_element_type=F32).astype(o_ref.dtype)


def _inproj(x2, gain, w, wlr=None):
    t, d = x2.shape
    n = w.shape[1]
    tm, tn = INPROJ_TM, INPROJ_TN
    grid = (t // tm, n // tn)
    x_spec = pl.BlockSpec((tm, d), lambda i, j: (i, 0))
    g_spec = pl.BlockSpec((1, d), lambda i, j: (0, 0))
    w_spec = pl.BlockSpec((d, tn), lambda i, j: (0, j))
    o_spec = pl.BlockSpec((tm, tn), lambda i, j: (i, j))
    params = pltpu.CompilerParams(
        dimension_semantics=("arbitrary", "arbitrary"), vmem_limit_bytes=VMEM_LIMIT)
    scratch = [pltpu.VMEM((tm, d), BF16)]
    if wlr is None:
        return pl.pallas_call(
            _inproj_kernel,
            grid=grid,
            in_specs=[x_spec, g_spec, w_spec],
            out_specs=o_spec,
            out_shape=jax.ShapeDtypeStruct((t, n), BF16),
            scratch_shapes=scratch,
            compiler_params=params,
            name="sgu_inproj",
        )(x2, gain, w)
    lr_w_spec = pl.BlockSpec((d, LANES), lambda i, j: (0, 0))
    lr_o_spec = pl.BlockSpec((tm, LANES), lambda i, j: (i, 0))
    return pl.pallas_call(
        _inproj_lr_kernel,
        grid=grid,
        in_specs=[x_spec, g_spec, w_spec, lr_w_spec],
        out_specs=[o_spec, lr_o_spec],
        out_shape=[jax.ShapeDtypeStruct((t, n), BF16), jax.ShapeDtypeStruct((t, LANES), F32)],
        scratch_shapes=scratch,
        compiler_params=params,
        name="gla_inproj",
    )(x2, gain, w, wlr)


def _gla_kernel(q_ref, k_ref, v_ref, g_ref, lr_ref, w2_ref, bg_ref, og_ref, tri_ref,
                a_ref, s_ref, sb_ref, kd_ref, dec_ref):
    @pl.when(pl.program_id(1) == 0)
    def _():
        s_ref[...] = jnp.zeros_like(s_ref)

    tile = q_ref.shape[1]
    n_chunks = tile // CHUNK
    z = jnp.dot(lr_ref[0].astype(BF16), w2_ref[...], preferred_element_type=F32) + bg_ref[...]
    la = (jnp.minimum(z, 0.0) - jnp.log(1.0 + jnp.exp(-jnp.abs(z)))) * GLA_INV_TAU
    hi = la.astype(BF16)
    lo = (la - hi.astype(F32)).astype(BF16)
    tri = tri_ref[...]
    for r in range(tile // GLA_TRI):
        rs = slice(r * GLA_TRI, (r + 1) * GLA_TRI)
        bcum = (jnp.dot(tri, hi[rs], preferred_element_type=F32)
                + jnp.dot(tri, lo[rs], preferred_element_type=F32))
        for cc in range(GLA_TRI // CHUNK):
            c = r * (GLA_TRI // CHUNK) + cc
            bc = bcum[cc * CHUNK:(cc + 1) * CHUNK]
            b_end = bc[CHUNK - 1:CHUNK, :]
            rows = slice(c * CHUNK, (c + 1) * CHUNK)
            kd_ref[rows, :] = (k_ref[0, rows, :].astype(F32) * jnp.exp(b_end - bc)).astype(BF16)
            dec_ref[c:c + 1, :] = jnp.exp(b_end)

    def chunk(c, carry):
        r0 = pl.multiple_of(c * CHUNK, CHUNK)
        rows = pl.ds(r0, CHUNK)
        dec = dec_ref[pl.ds(c, 1), :]
        for h in range(GLA_HEADS):
            kc = slice(h * GLA_DKH, (h + 1) * GLA_DKH)
            vc = slice(h * GLA_DVH, (h + 1) * GLA_DVH)
            upd = lax.dot_general(kd_ref[rows, kc], v_ref[0, rows, vc], (((0,), (0,)), ((), ())),
                                  preferred_element_type=F32)
            decay_col = jnp.transpose(jnp.broadcast_to(dec[:, kc], (LANES, GLA_DKH)))
            for t in range(GLA_DVH // LANES):
                cols = slice(t * LANES, (t + 1) * LANES)
                s_new = s_ref[h, :, cols] * decay_col + upd[:, cols]
                s_ref[h, :, cols] = s_new
                sb_ref[h, :, cols] = s_new.astype(BF16)
        for h in range(GLA_HEADS):
            kc = slice(h * GLA_DKH, (h + 1) * GLA_DKH)
            vc = slice(h * GLA_DVH, (h + 1) * GLA_DVH)
            o = jnp.dot(q_ref[0, rows, kc], sb_ref[h],
                        preferred_element_type=F32) * (GLA_DKH ** -0.5)
            o = _rms(o, og_ref[:, vc])
            g = g_ref[0, rows, vc].astype(F32)
            a_ref[0, rows, vc] = (o * (g * jax.nn.sigmoid(g))).astype(a_ref.dtype)
        return carry

    lax.fori_loop(0, n_chunks, chunk, 0, unroll=2)


def _gla_scan(proj3, lr3, w2, b_gate, o_gain, tri):
    b, s, _ = proj3.shape
    tile = GLA_TILE
    in_specs = [
        pl.BlockSpec((1, tile, GLA_DK), lambda i, t: (i, t, 0)),
        pl.BlockSpec((1, tile, GLA_DK), lambda i, t: (i, t, 1)),
        pl.BlockSpec((1, tile, GLA_DV), lambda i, t: (i, t, 1)),
        pl.BlockSpec((1, tile, GLA_DV), lambda i, t: (i, t, 2)),
        pl.BlockSpec((1, tile, LANES), lambda i, t: (i, t, 0)),
        pl.BlockSpec((LANES, GLA_DK), lambda i, t: (0, 0)),
        pl.BlockSpec((1, GLA_DK), lambda i, t: (0, 0)),
        pl.BlockSpec((1, GLA_DV), lambda i, t: (0, 0)),
        pl.BlockSpec((GLA_TRI, GLA_TRI), lambda i, t: (0, 0)),
    ]
    return pl.pallas_call(
        _gla_kernel,
        grid=(b, s // tile),
        in_specs=in_specs,
        out_specs=pl.BlockSpec((1, tile, GLA_DV), lambda i, t: (i, t, 0)),
        out_shape=jax.ShapeDtypeStruct((b, s, GLA_DV), BF16),
        scratch_shapes=[
            pltpu.VMEM((GLA_HEADS, GLA_DKH, GLA_DVH), F32),
            pltpu.VMEM((GLA_HEADS, GLA_DKH, GLA_DVH), BF16),
            pltpu.VMEM((tile, GLA_DK), BF16),
            pltpu.VMEM((tile // CHUNK, GLA_DK), F32),
        ],
        compiler_params=pltpu.CompilerParams(
            dimension_semantics=("arbitrary", "arbitrary"), vmem_limit_bytes=VMEM_LIMIT),
        name="gla_scan",
    )(proj3, proj3, proj3, proj3, lr3, w2, b_gate, o_gain, tri)


def _outproj_kernel(a_ref, w_ref, gain_ref, x_ref, o_ref):
    y = jnp.dot(a_ref[...], w_ref[...], preferred_element_type=F32)
    o_ref[...] = x_ref[...] + _rms(y, gain_ref[...])


def _outproj(a2, w, gain, x2):
    t, d = x2.shape
    k = a2.shape[1]
    tm = OUT_TM
    return pl.pallas_call(
        _outproj_kernel,
        grid=(t // tm,),
        in_specs=[
            pl.BlockSpec((tm, k), lambda i: (i, 0)),
            pl.BlockSpec((k, d), lambda i: (0, 0)),
            pl.BlockSpec((1, d), lambda i: (0, 0)),
            pl.BlockSpec((tm, d), lambda i: (i, 0)),
        ],
        out_specs=pl.BlockSpec((tm, d), lambda i: (i, 0)),
        out_shape=jax.ShapeDtypeStruct((t, d), F32),
        compiler_params=pltpu.CompilerParams(
            dimension_semantics=("arbitrary",), vmem_limit_bytes=VMEM_LIMIT),
        name="gla_outproj",
    )(a2, w, gain, x2)


def _sgu_kernel(u_ref, v_ref, g_ref, lng_ref, lnb_ref, ws_ref, bs_ref, w_ref, gain_ref, x_ref,
                o_ref, wsm_ref, z_ref):
    @pl.when(pl.program_id(0) == 0)
    def _():
        ri = lax.broadcasted_iota(jnp.int32, (SGU_BLOCK, SGU_BLOCK), 0) // CHUNK
        ci = lax.broadcasted_iota(jnp.int32, (SGU_BLOCK, SGU_BLOCK), 1) // CHUNK
        for gi in range(SGU_GROUPS):
            wsm_ref[gi] = jnp.where(ri >= ci, ws_ref[gi], 0.0).astype(BF16)

    v = jax.nn.gelu(v_ref[...].astype(F32))
    mu = jnp.mean(v, axis=-1, keepdims=True)
    vc = v - mu
    var = jnp.mean(vc * vc, axis=-1, keepdims=True)
    vn = (vc * lax.rsqrt(var + EPS) * lng_ref[...] + lnb_ref[...]).astype(BF16)
    tm = v.shape[0]
    for nb in range(tm // SGU_BLOCK):
        rows = slice(nb * SGU_BLOCK, (nb + 1) * SGU_BLOCK)
        for gi in range(SGU_GROUPS):
            cols = slice(gi * SGU_GD, (gi + 1) * SGU_GD)
            vs = jnp.dot(wsm_ref[gi], vn[rows, cols], preferred_element_type=F32) + bs_ref[:, cols]
            u = jax.nn.gelu(u_ref[rows, cols].astype(F32))
            g = g_ref[rows, cols].astype(F32)
            z_ref[rows, cols] = (u * vs * (g * jax.nn.sigmoid(g))).astype(BF16)
    y = jnp.dot(z_ref[...], w_ref[...], preferred_element_type=F32)
    o_ref[...] = x_ref[...] + _rms(y, gain_ref[...])


def _sgu(proj, ln_gain, ln_bias, w_spatial, bias_full, w_out, gain, x2):
    t, d = x2.shape
    tm = SGU_TM
    row = lambda i: (i, 0)
    const2 = lambda i: (0, 0)
    return pl.pallas_call(
        _sgu_kernel,
        grid=(t // tm,),
        in_specs=[
            pl.BlockSpec((tm, SGU_WIDTH), lambda i: (i, 0)),
            pl.BlockSpec((tm, SGU_WIDTH), lambda i: (i, 1)),
            pl.BlockSpec((tm, SGU_WIDTH), lambda i: (i, 2)),
            pl.BlockSpec((1, SGU_WIDTH), const2),
            pl.BlockSpec((1, SGU_WIDTH), const2),
            pl.BlockSpec((SGU_GROUPS, SGU_BLOCK, SGU_BLOCK), lambda i: (0, 0, 0)),
            pl.BlockSpec((SGU_BLOCK, SGU_WIDTH), const2),
            pl.BlockSpec((SGU_WIDTH, d), const2),
            pl.BlockSpec((1, d), const2),
            pl.BlockSpec((tm, d), row),
        ],
        out_specs=pl.BlockSpec((tm, d), row),
        out_shape=jax.ShapeDtypeStruct((t, d), F32),
        scratch_shapes=[
            pltpu.VMEM((SGU_GROUPS, SGU_BLOCK, SGU_BLOCK), BF16),
            pltpu.VMEM((tm, SGU_WIDTH), BF16),
        ],
        compiler_params=pltpu.CompilerParams(
            dimension_semantics=("arbitrary",), vmem_limit_bytes=VMEM_LIMIT),
        name="sgu_mix_outproj",
    )(proj, proj, proj, ln_gain, ln_bias, w_spatial, bias_full, w_out, gain, x2)


def kernel(x, norm_pre, norm_post, gla_w_in, gla_w_gate2, gla_b_gate, gla_o_gain, gla_w_out,
           sgu_w_in, sgu_ln_gain, sgu_ln_bias, sgu_w_spatial, sgu_b_spatial, sgu_w_out):
    b, s, d = x.shape
    t = b * s
    x2 = x.reshape(t, d)

    w_in = gla_w_in[0]
    w_main = w_in[:, :GLA_MAIN].astype(BF16)
    w_lr = jnp.pad(w_in[:, GLA_MAIN:], ((0, 0), (0, LANES - GLA_RANK))).astype(BF16)
    w2 = jnp.pad(gla_w_gate2[0], ((0, LANES - GLA_RANK), (0, 0))).astype(BF16)
    ri = jnp.arange(GLA_TRI)
    tri = ((ri[:, None] >= ri[None, :]) & (ri[:, None] // CHUNK == ri[None, :] // CHUNK)).astype(BF16)
    proj, lr = _inproj(x2, norm_pre[0:1], w_main, w_lr)
    a = _gla_scan(proj.reshape(b, s, GLA_MAIN), lr.reshape(b, s, LANES), w2,
                  gla_b_gate[0:1], gla_o_gain[0:1], tri)
    x2 = _outproj(a.reshape(t, GLA_DV), gla_w_out[0].astype(BF16), norm_post[0:1], x2)

    proj = _inproj(x2, norm_pre[1:2], sgu_w_in[0].astype(BF16))
    bias_full = jnp.repeat(sgu_b_spatial[0].T, SGU_GD, axis=1)
    x2 = _sgu(proj, sgu_ln_gain[0:1], sgu_ln_bias[0:1], sgu_w_spatial[0], bias_full,
              sgu_w_out[0].astype(BF16), norm_post[1:2], x2)
    return x2.reshape(b, s, d)
```

```python
import jax
import jax.numpy as jnp
from jax import lax
from jax.experimental import pallas as pl
from jax.experimental.pallas import tpu as pltpu

F32 = jnp.float32
BF16 = jnp.bfloat16

D_MODEL = 2048
EPS = 1e-6
CHUNK = 64

GLA_HEADS = 4
GLA_DK = D_MODEL // 2
GLA_DV = D_MODEL
GLA_DKH = GLA_DK // GLA_HEADS
GLA_DVH = GLA_DV // GLA_HEADS
GLA_RANK = 16
GLA_INV_TAU = 1.0 / 16.0
GLA_MAIN = 2 * GLA_DK + 2 * GLA_DV

SGU_WIDTH = D_MODEL
SGU_BLOCK = 128
SGU_GROUPS = 8
SGU_GD = SGU_WIDTH // SGU_GROUPS

LANES = 128
VMEM_LIMIT = 56 * 1024 * 1024

INPROJ_TM = 512
INPROJ_SLAB = 512
OUT_TM = 512
SGU_TM = 512
GLA_TILE = 512
GLA_TRI = 256

GELU_A = 2.0 * (2.0 / 3.141592653589793) ** 0.5
GELU_B = GELU_A * 0.044715


def _rms(x, gain):
    return x * lax.rsqrt(jnp.mean(x * x, axis=-1, keepdims=True) + EPS) * gain


def _sigmoid_gate(r, p):
    return r / (1.0 + jnp.exp(-p))


def _gelu(r):
    return _sigmoid_gate(r, r * (GELU_A + GELU_B * (r * r)))


def _silu(r):
    return _sigmoid_gate(r, r)


def _slab_matmul_act(h, w_ref, o_ref, act_of_col):
    for n in range(w_ref.shape[1] // INPROJ_SLAB):
        cols = slice(n * INPROJ_SLAB, (n + 1) * INPROJ_SLAB)
        r = jnp.dot(h, w_ref[:, cols], preferred_element_type=F32)
        act = act_of_col(n * INPROJ_SLAB)
        o_ref[:, cols] = (r if act is None else act(r)).astype(o_ref.dtype)


def _sgu_inproj_kernel(x_ref, gain_ref, w_ref, o_ref):
    h = _rms(x_ref[...], gain_ref[...]).astype(BF16)
    _slab_matmul_act(h, w_ref, o_ref, lambda c: _gelu if c < 2 * SGU_WIDTH else _silu)


def _gla_inproj_kernel(x_ref, gain_ref, w_ref, wlr_ref, o_ref, lr_ref):
    h = _rms(x_ref[...], gain_ref[...]).astype(BF16)
    lr_ref[...] = jnp.dot(h, wlr_ref[...], preferred_element_type=F32)
    _slab_matmul_act(h, w_ref, o_ref, lambda c: _silu if c >= 2 * GLA_DK + GLA_DV else None)


def _inproj(x2, gain, w, wlr=None):
    t, d = x2.shape
    n = w.shape[1]
    tm = INPROJ_TM
    resident = pl.Buffered(1)
    x_spec = pl.BlockSpec((tm, d), lambda i: (i, 0))
    g_spec = pl.BlockSpec((1, d), lambda i: (0, 0))
    w_spec = pl.BlockSpec((d, n), lambda i: (0, 0), pipeline_mode=resident)
    o_spec = pl.BlockSpec((tm, n), lambda i: (i, 0))
    params = pltpu.CompilerParams(dimension_semantics=("arbitrary",), vmem_limit_bytes=VMEM_LIMIT)
    if wlr is None:
        return pl.pallas_call(
            _sgu_inproj_kernel,
            grid=(t // tm,),
            in_specs=[x_spec, g_spec, w_spec],
            out_specs=o_spec,
            out_shape=jax.ShapeDtypeStruct((t, n), BF16),
            compiler_params=params,
            name="sgu_inproj",
        )(x2, gain, w)
    lr_w_spec = pl.BlockSpec((d, LANES), lambda i: (0, 0), pipeline_mode=resident)
    lr_o_spec = pl.BlockSpec((tm, LANES), lambda i: (i, 0))
    return pl.pallas_call(
        _gla_inproj_kernel,
        grid=(t // tm,),
        in_specs=[x_spec, g_spec, w_spec, lr_w_spec],
        out_specs=[o_spec, lr_o_spec],
        out_shape=[jax.ShapeDtypeStruct((t, n), BF16), jax.ShapeDtypeStruct((t, LANES), F32)],
        compiler_params=params,
        name="gla_inproj",
    )(x2, gain, w, wlr)


def _gla_kernel(q_ref, k_ref, v_ref, g_ref, lr_ref, w2_ref, bg_ref, og_ref, tri_ref,
                a_ref, s_ref, sb_ref, kd_ref, dec_ref):
    @pl.when(pl.program_id(1) == 0)
    def _():
        s_ref[...] = jnp.zeros_like(s_ref)

    tile = q_ref.shape[1]
    n_chunks = tile // CHUNK
    z = jnp.dot(lr_ref[0].astype(BF16), w2_ref[...], preferred_element_type=F32) + bg_ref[...]
    la = (jnp.minimum(z, 0.0) - jnp.log(1.0 + jnp.exp(-jnp.abs(z)))) * GLA_INV_TAU
    hi = la.astype(BF16)
    lo = (la - hi.astype(F32)).astype(BF16)
    tri = tri_ref[...]
    for r in range(tile // GLA_TRI):
        rs = slice(r * GLA_TRI, (r + 1) * GLA_TRI)
        bcum = (jnp.dot(tri, hi[rs], preferred_element_type=F32)
                + jnp.dot(tri, lo[rs], preferred_element_type=F32))
        for cc in range(GLA_TRI // CHUNK):
            c = r * (GLA_TRI // CHUNK) + cc
            bc = bcum[cc * CHUNK:(cc + 1) * CHUNK]
            b_end = bc[CHUNK - 1:CHUNK, :]
            rows = slice(c * CHUNK, (c + 1) * CHUNK)
            kd_ref[rows, :] = (k_ref[0, rows, :].astype(F32) * jnp.exp(b_end - bc)).astype(BF16)
            dec_ref[c:c + 1, :] = jnp.exp(b_end)

    def chunk(c, carry):
        r0 = pl.multiple_of(c * CHUNK, CHUNK)
        rows = pl.ds(r0, CHUNK)
        dec = dec_ref[pl.ds(c, 1), :]
        for h in range(GLA_HEADS):
            kc = slice(h * GLA_DKH, (h + 1) * GLA_DKH)
            vc = slice(h * GLA_DVH, (h + 1) * GLA_DVH)
            upd = lax.dot_general(kd_ref[rows, kc], v_ref[0, rows, vc], (((0,), (0,)), ((), ())),
                                  preferred_element_type=F32)
            decay_col = jnp.transpose(jnp.broadcast_to(dec[:, kc], (LANES, GLA_DKH)))
            for t in range(GLA_DVH // LANES):
                cols = slice(t * LANES, (t + 1) * LANES)
                s_new = s_ref[h, :, cols] * decay_col + upd[:, cols]
                s_ref[h, :, cols] = s_new
                sb_ref[h, :, cols] = s_new.astype(BF16)
        for h in range(GLA_HEADS):
            kc = slice(h * GLA_DKH, (h + 1) * GLA_DKH)
            vc = slice(h * GLA_DVH, (h + 1) * GLA_DVH)
            o = jnp.dot(q_ref[0, rows, kc], sb_ref[h],
                        preferred_element_type=F32) * (GLA_DKH ** -0.5)
            o = _rms(o, og_ref[:, vc])
            a_ref[0, rows, vc] = (o * g_ref[0, rows, vc].astype(F32)).astype(a_ref.dtype)
        return carry

    lax.fori_loop(0, n_chunks, chunk, 0, unroll=2)


def _gla_scan(proj3, lr3, w2, b_gate, o_gain, tri):
    b, s, _ = proj3.shape
    tile = GLA_TILE
    in_specs = [
        pl.BlockSpec((1, tile, GLA_DK), lambda i, t: (i, t, 0)),
        pl.BlockSpec((1, tile, GLA_DK), lambda i, t: (i, t, 1)),
        pl.BlockSpec((1, tile, GLA_DV), lambda i, t: (i, t, 1)),
        pl.BlockSpec((1, tile, GLA_DV), lambda i, t: (i, t, 2)),
        pl.BlockSpec((1, tile, LANES), lambda i, t: (i, t, 0)),
        pl.BlockSpec((LANES, GLA_DK), lambda i, t: (0, 0)),
        pl.BlockSpec((1, GLA_DK), lambda i, t: (0, 0)),
        pl.BlockSpec((1, GLA_DV), lambda i, t: (0, 0)),
        pl.BlockSpec((GLA_TRI, GLA_TRI), lambda i, t: (0, 0)),
    ]
    return pl.pallas_call(
        _gla_kernel,
        grid=(b, s // tile),
        in_specs=in_specs,
        out_specs=pl.BlockSpec((1, tile, GLA_DV), lambda i, t: (i, t, 0)),
        out_shape=jax.ShapeDtypeStruct((b, s, GLA_DV), BF16),
        scratch_shapes=[
            pltpu.VMEM((GLA_HEADS, GLA_DKH, GLA_DVH), F32),
            pltpu.VMEM((GLA_HEADS, GLA_DKH, GLA_DVH), BF16),
            pltpu.VMEM((tile, GLA_DK), BF16),
            pltpu.VMEM((tile // CHUNK, GLA_DK), F32),
        ],
        compiler_params=pltpu.CompilerParams(
            dimension_semantics=("arbitrary", "arbitrary"), vmem_limit_bytes=VMEM_LIMIT),
        name="gla_scan",
    )(proj3, proj3, proj3, proj3, lr3, w2, b_gate, o_gain, tri)


def _outproj_kernel(a_ref, w_ref, gain_ref, x_ref, o_ref):
    y = jnp.dot(a_ref[...], w_ref[...], preferred_element_type=F32)
    o_ref[...] = x_ref[...] + _rms(y, gain_ref[...])


def _outproj(a2, w, gain, x2):
    t, d = x2.shape
    k = a2.shape[1]
    tm = OUT_TM
    return pl.pallas_call(
        _outproj_kernel,
        grid=(t // tm,),
        in_specs=[
            pl.BlockSpec((tm, k), lambda i: (i, 0)),
            pl.BlockSpec((k, d), lambda i: (0, 0)),
            pl.BlockSpec((1, d), lambda i: (0, 0)),
            pl.BlockSpec((tm, d), lambda i: (i, 0)),
        ],
        out_specs=pl.BlockSpec((tm, d), lambda i: (i, 0)),
        out_shape=jax.ShapeDtypeStruct((t, d), F32),
        compiler_params=pltpu.CompilerParams(
            dimension_semantics=("arbitrary",), vmem_limit_bytes=VMEM_LIMIT),
        name="gla_outproj",
    )(a2, w, gain, x2)


def _sgu_kernel(u_ref, v_ref, g_ref, lng_ref, lnb_ref, ws_ref, bs_ref, w_ref, gain_ref, x_ref,
                o_ref, wsm_ref, z_ref):
    @pl.when(pl.program_id(0) == 0)
    def _():
        ri = lax.broadcasted_iota(jnp.int32, (SGU_BLOCK, SGU_BLOCK), 0) // CHUNK
        ci = lax.broadcasted_iota(jnp.int32, (SGU_BLOCK, SGU_BLOCK), 1) // CHUNK
        for gi in range(SGU_GROUPS):
            wsm_ref[gi] = jnp.where(ri >= ci, ws_ref[gi], 0.0).astype(BF16)

    v = v_ref[...].astype(F32)
    mu = jnp.mean(v, axis=-1, keepdims=True)
    vc = v - mu
    var = jnp.mean(vc * vc, axis=-1, keepdims=True)
    vn = (vc * lax.rsqrt(var + EPS) * lng_ref[...] + lnb_ref[...]).astype(BF16)
    tm = v.shape[0]
    for nb in range(tm // SGU_BLOCK):
        rows = slice(nb * SGU_BLOCK, (nb + 1) * SGU_BLOCK)
        for gi in range(SGU_GROUPS):
            cols = slice(gi * SGU_GD, (gi + 1) * SGU_GD)
            vs = jnp.dot(wsm_ref[gi], vn[rows, cols], preferred_element_type=F32) + bs_ref[:, cols]
            ug = u_ref[rows, cols].astype(F32) * g_ref[rows, cols].astype(F32)
            z_ref[rows, cols] = (ug * vs).astype(BF16)
    y = jnp.dot(z_ref[...], w_ref[...], preferred_element_type=F32)
    o_ref[...] = x_ref[...] + _rms(y, gain_ref[...])


def _sgu(proj, ln_gain, ln_bias, w_spatial, bias_full, w_out, gain, x2):
    t, d = x2.shape
    tm = SGU_TM
    row = lambda i: (i, 0)
    const2 = lambda i: (0, 0)
    return pl.pallas_call(
        _sgu_kernel,
        grid=(t // tm,),
        in_specs=[
            pl.BlockSpec((tm, SGU_WIDTH), lambda i: (i, 0)),
            pl.BlockSpec((tm, SGU_WIDTH), lambda i: (i, 1)),
            pl.BlockSpec((tm, SGU_WIDTH), lambda i: (i, 2)),
            pl.BlockSpec((1, SGU_WIDTH), const2),
            pl.BlockSpec((1, SGU_WIDTH), const2),
            pl.BlockSpec((SGU_GROUPS, SGU_BLOCK, SGU_BLOCK), lambda i: (0, 0, 0)),
            pl.BlockSpec((SGU_BLOCK, SGU_WIDTH), const2),
            pl.BlockSpec((SGU_WIDTH, d), const2),
            pl.BlockSpec((1, d), const2),
            pl.BlockSpec((tm, d), row),
        ],
        out_specs=pl.BlockSpec((tm, d), row),
        out_shape=jax.ShapeDtypeStruct((t, d), F32),
        scratch_shapes=[
            pltpu.VMEM((SGU_GROUPS, SGU_BLOCK, SGU_BLOCK), BF16),
            pltpu.VMEM((tm, SGU_WIDTH), BF16),
        ],
        compiler_params=pltpu.CompilerParams(
            dimension_semantics=("arbitrary",), vmem_limit_bytes=VMEM_LIMIT),
        name="sgu_mix_outproj",
    )(proj, proj, proj, ln_gain, ln_bias, w_spatial, bias_full, w_out, gain, x2)


def kernel(x, norm_pre, norm_post, gla_w_in, gla_w_gate2, gla_b_gate, gla_o_gain, gla_w_out,
           sgu_w_in, sgu_ln_gain, sgu_ln_bias, sgu_w_spatial, sgu_b_spatial, sgu_w_out):
    b, s, d = x.shape
    t = b * s
    x2 = x.reshape(t, d)

    w_in = gla_w_in[0]
    w_main = w_in[:, :GLA_MAIN].astype(BF16)
    w_lr = jnp.pad(w_in[:, GLA_MAIN:], ((0, 0), (0, LANES - GLA_RANK))).astype(BF16)
    w2 = jnp.pad(gla_w_gate2[0], ((0, LANES - GLA_RANK), (0, 0))).astype(BF16)
    ri = jnp.arange(GLA_TRI)
    tri = ((ri[:, None] >= ri[None, :]) & (ri[:, None] // CHUNK == ri[None, :] // CHUNK)).astype(BF16)
    proj, lr = _inproj(x2, norm_pre[0:1], w_main, w_lr)
    a = _gla_scan(proj.reshape(b, s, GLA_MAIN), lr.reshape(b, s, LANES), w2,
                  gla_b_gate[0:1], gla_o_gain[0:1], tri)
    x2 = _outproj(a.reshape(t, GLA_DV), gla_w_out[0].astype(BF16), norm_post[0:1], x2)

    proj = _inproj(x2, norm_pre[1:2], sgu_w_in[0].astype(BF16))
    bias_full = jnp.repeat(sgu_b_spatial[0].T, SGU_GD, axis=1)
    x2 = _sgu(proj, sgu_ln_gain[0:1], sgu_ln_bias[0:1], sgu_w_spatial[0], bias_full,
              sgu_w_out[0].astype(BF16), norm_post[1:2], x2)
    return x2.reshape(b, s, d)
```

```python
import jax
import jax.numpy as jnp
from jax import lax
from jax.experimental import pallas as pl
from jax.experimental.pallas import tpu as pltpu

F32 = jnp.float32
BF16 = jnp.bfloat16

D_MODEL = 2048
EPS = 1e-6
CHUNK = 64

GLA_HEADS = 4
GLA_DK = D_MODEL // 2
GLA_DV = D_MODEL
GLA_DKH = GLA_DK // GLA_HEADS
GLA_DVH = GLA_DV // GLA_HEADS
GLA_RANK = 16
GLA_INV_TAU = 1.0 / 16.0
GLA_MAIN = 2 * GLA_DK + 2 * GLA_DV

SGU_WIDTH = D_MODEL
SGU_BLOCK = 128
SGU_GROUPS = 8
SGU_GD = SGU_WIDTH // SGU_GROUPS

LANES = 128
VMEM_LIMIT = 56 * 1024 * 1024

INPROJ_TM = 512
INPROJ_SLAB = 512
OUT_TM = 512
SGU_TM = 512
GLA_TILE = 512
GLA_TRI = 256
CAST_TN = 1024

GELU_C1 = (2.0 / 3.141592653589793) ** 0.5
GELU_C3 = GELU_C1 * 0.044715


def _rms(x, gain):
    return x * lax.rsqrt(jnp.mean(x * x, axis=-1, keepdims=True) + EPS) * gain


def _gelu(r):
    return (0.5 * r) * (1.0 + jnp.tanh(r * (GELU_C1 + GELU_C3 * (r * r))))


def _silu(r):
    hr = 0.5 * r
    return hr * (1.0 + jnp.tanh(hr))


def _slab_matmul_act(h, w_ref, o_ref, act_of_col):
    for n in range(w_ref.shape[1] // INPROJ_SLAB):
        cols = slice(n * INPROJ_SLAB, (n + 1) * INPROJ_SLAB)
        r = jnp.dot(h, w_ref[:, cols], preferred_element_type=F32)
        act = act_of_col(n * INPROJ_SLAB)
        o_ref[:, cols] = (r if act is None else act(r)).astype(o_ref.dtype)


def _sgu_inproj_kernel(x_ref, gain_ref, w_ref, wn_ref, o_ref, wnb_ref):
    wnb_ref[...] = wn_ref[...].astype(BF16)
    h = _rms(x_ref[...], gain_ref[...]).astype(BF16)
    _slab_matmul_act(h, w_ref, o_ref, lambda c: _gelu if c < 2 * SGU_WIDTH else _silu)


def _gla_inproj_kernel(x_ref, gain_ref, w_ref, wlr_ref, wn_ref, o_ref, lr_ref, wnb_ref):
    wnb_ref[...] = wn_ref[...].astype(BF16)
    h = _rms(x_ref[...], gain_ref[...]).astype(BF16)
    lr_ref[...] = jnp.dot(h, wlr_ref[...], preferred_element_type=F32)
    _slab_matmul_act(h, w_ref, o_ref, lambda c: _silu if c >= 2 * GLA_DK + GLA_DV else None)


def _cast_specs(w_next, steps):
    rows, cols = w_next.shape
    spec = pl.BlockSpec((rows // steps, cols), lambda i: (i, 0))
    return spec, spec, jax.ShapeDtypeStruct((rows, cols), BF16)


def _inproj(x2, gain, w, w_next, wlr=None):
    t, d = x2.shape
    n = w.shape[1]
    tm = INPROJ_TM
    steps = t // tm
    resident = pl.Buffered(1)
    x_spec = pl.BlockSpec((tm, d), lambda i: (i, 0))
    g_spec = pl.BlockSpec((1, d), lambda i: (0, 0))
    w_spec = pl.BlockSpec((d, n), lambda i: (0, 0), pipeline_mode=resident)
    o_spec = pl.BlockSpec((tm, n), lambda i: (i, 0))
    wn_in, wn_out, wn_shape = _cast_specs(w_next, steps)
    params = pltpu.CompilerParams(dimension_semantics=("arbitrary",), vmem_limit_bytes=VMEM_LIMIT)
    if wlr is None:
        return pl.pallas_call(
            _sgu_inproj_kernel,
            grid=(steps,),
            in_specs=[x_spec, g_spec, w_spec, wn_in],
            out_specs=[o_spec, wn_out],
            out_shape=[jax.ShapeDtypeStruct((t, n), BF16), wn_shape],
            compiler_params=params,
            name="sgu_inproj",
        )(x2, gain, w, w_next)
    lr_w_spec = pl.BlockSpec((d, LANES), lambda i: (0, 0), pipeline_mode=resident)
    lr_o_spec = pl.BlockSpec((tm, LANES), lambda i: (i, 0))
    return pl.pallas_call(
        _gla_inproj_kernel,
        grid=(steps,),
        in_specs=[x_spec, g_spec, w_spec, lr_w_spec, wn_in],
        out_specs=[o_spec, lr_o_spec, wn_out],
        out_shape=[jax.ShapeDtypeStruct((t, n), BF16), jax.ShapeDtypeStruct((t, LANES), F32),
                   wn_shape],
        compiler_params=params,
        name="gla_inproj",
    )(x2, gain, w, wlr, w_next)


def _cast_kernel(w_ref, o_ref):
    o_ref[...] = w_ref[...].astype(BF16)


def _cast_cols(w, n_cols):
    rows = w.shape[0]
    spec = pl.BlockSpec((rows, CAST_TN), lambda j: (0, j))
    return pl.pallas_call(
        _cast_kernel,
        grid=(n_cols // CAST_TN,),
        in_specs=[spec],
        out_specs=spec,
        out_shape=jax.ShapeDtypeStruct((rows, n_cols), BF16),
        compiler_params=pltpu.CompilerParams(
            dimension_semantics=("arbitrary",), vmem_limit_bytes=VMEM_LIMIT),
        name="cast_w_in",
    )(w)


def _gla_kernel(q_ref, k_ref, v_ref, g_ref, lr_ref, w2_ref, bg_ref, og_ref, tri_ref,
                a_ref, s_ref, sb_ref, kd_ref, dec_ref):
    @pl.when(pl.program_id(1) == 0)
    def _():
        s_ref[...] = jnp.zeros_like(s_ref)

    tile = q_ref.shape[1]
    n_chunks = tile // CHUNK
    z = jnp.dot(lr_ref[0].astype(BF16), w2_ref[...], preferred_element_type=F32) + bg_ref[...]
    la = (jnp.minimum(z, 0.0) - jnp.log(1.0 + jnp.exp(-jnp.abs(z)))) * GLA_INV_TAU
    hi = la.astype(BF16)
    lo = (la - hi.astype(F32)).astype(BF16)
    tri = tri_ref[...]
    for r in range(tile // GLA_TRI):
        rs = slice(r * GLA_TRI, (r + 1) * GLA_TRI)
        bcum = (jnp.dot(tri, hi[rs], preferred_element_type=F32)
                + jnp.dot(tri, lo[rs], preferred_element_type=F32))
        for cc in range(GLA_TRI // CHUNK):
            c = r * (GLA_TRI // CHUNK) + cc
            bc = bcum[cc * CHUNK:(cc + 1) * CHUNK]
            b_end = bc[CHUNK - 1:CHUNK, :]
            rows = slice(c * CHUNK, (c + 1) * CHUNK)
            kd_ref[rows, :] = (k_ref[0, rows, :].astype(F32) * jnp.exp(b_end - bc)).astype(BF16)
            dec_ref[c:c + 1, :] = jnp.exp(b_end)

    def chunk(c, carry):
        r0 = pl.multiple_of(c * CHUNK, CHUNK)
        rows = pl.ds(r0, CHUNK)
        dec = dec_ref[pl.ds(c, 1), :]
        for h in range(GLA_HEADS):
            kc = slice(h * GLA_DKH, (h + 1) * GLA_DKH)
            vc = slice(h * GLA_DVH, (h + 1) * GLA_DVH)
            upd = lax.dot_general(kd_ref[rows, kc], v_ref[0, rows, vc], (((0,), (0,)), ((), ())),
                                  preferred_element_type=F32)
            decay_col = jnp.transpose(jnp.broadcast_to(dec[:, kc], (LANES, GLA_DKH)))
            for t in range(GLA_DVH // LANES):
                cols = slice(t * LANES, (t + 1) * LANES)
                s_new = s_ref[h, :, cols] * decay_col + upd[:, cols]
                s_ref[h, :, cols] = s_new
                sb_ref[h, :, cols] = s_new.astype(BF16)
        for h in range(GLA_HEADS):
            kc = slice(h * GLA_DKH, (h + 1) * GLA_DKH)
            vc = slice(h * GLA_DVH, (h + 1) * GLA_DVH)
            o = jnp.dot(q_ref[0, rows, kc], sb_ref[h],
                        preferred_element_type=F32) * (GLA_DKH ** -0.5)
            o = _rms(o, og_ref[:, vc])
            a_ref[0, rows, vc] = (o * g_ref[0, rows, vc].astype(F32)).astype(a_ref.dtype)
        return carry

    lax.fori_loop(0, n_chunks, chunk, 0, unroll=2)


def _gla_scan(proj3, lr3, w2, b_gate, o_gain, tri):
    b, s, _ = proj3.shape
    tile = GLA_TILE
    in_specs = [
        pl.BlockSpec((1, tile, GLA_DK), lambda i, t: (i, t, 0)),
        pl.BlockSpec((1, tile, GLA_DK), lambda i, t: (i, t, 1)),
        pl.BlockSpec((1, tile, GLA_DV), lambda i, t: (i, t, 1)),
        pl.BlockSpec((1, tile, GLA_DV), lambda i, t: (i, t, 2)),
        pl.BlockSpec((1, tile, LANES), lambda i, t: (i, t, 0)),
        pl.BlockSpec((LANES, GLA_DK), lambda i, t: (0, 0)),
        pl.BlockSpec((1, GLA_DK), lambda i, t: (0, 0)),
        pl.BlockSpec((1, GLA_DV), lambda i, t: (0, 0)),
        pl.BlockSpec((GLA_TRI, GLA_TRI), lambda i, t: (0, 0)),
    ]
    return pl.pallas_call(
        _gla_kernel,
        grid=(b, s // tile),
        in_specs=in_specs,
        out_specs=pl.BlockSpec((1, tile, GLA_DV), lambda i, t: (i, t, 0)),
        out_shape=jax.ShapeDtypeStruct((b, s, GLA_DV), BF16),
        scratch_shapes=[
            pltpu.VMEM((GLA_HEADS, GLA_DKH, GLA_DVH), F32),
            pltpu.VMEM((GLA_HEADS, GLA_DKH, GLA_DVH), BF16),
            pltpu.VMEM((tile, GLA_DK), BF16),
            pltpu.VMEM((tile // CHUNK, GLA_DK), F32),
        ],
        compiler_params=pltpu.CompilerParams(
            dimension_semantics=("arbitrary", "arbitrary"), vmem_limit_bytes=VMEM_LIMIT),
        name="gla_scan",
    )(proj3, proj3, proj3, proj3, lr3, w2, b_gate, o_gain, tri)


def _outproj_kernel(a_ref, w_ref, gain_ref, x_ref, wn_ref, o_ref, wnb_ref):
    wnb_ref[...] = wn_ref[...].astype(BF16)
    y = jnp.dot(a_ref[...], w_ref[...], preferred_element_type=F32)
    o_ref[...] = x_ref[...] + _rms(y, gain_ref[...])


def _outproj(a2, w, gain, x2, w_next):
    t, d = x2.shape
    k = a2.shape[1]
    tm = OUT_TM
    steps = t // tm
    wn_in, wn_out, wn_shape = _cast_specs(w_next, steps)
    return pl.pallas_call(
        _outproj_kernel,
        grid=(steps,),
        in_specs=[
            pl.BlockSpec((tm, k), lambda i: (i, 0)),
            pl.BlockSpec((k, d), lambda i: (0, 0), pipeline_mode=pl.Buffered(1)),
            pl.BlockSpec((1, d), lambda i: (0, 0)),
            pl.BlockSpec((tm, d), lambda i: (i, 0)),
            wn_in,
        ],
        out_specs=[pl.BlockSpec((tm, d), lambda i: (i, 0)), wn_out],
        out_shape=[jax.ShapeDtypeStruct((t, d), F32), wn_shape],
        compiler_params=pltpu.CompilerParams(
            dimension_semantics=("arbitrary",), vmem_limit_bytes=VMEM_LIMIT),
        name="gla_outproj",
    )(a2, w, gain, x2, w_next)


def _sgu_kernel(u_ref, v_ref, g_ref, lng_ref, lnb_ref, ws_ref, bs_ref, w_ref, gain_ref, x_ref,
                o_ref, wsm_ref, z_ref):
    @pl.when(pl.program_id(0) == 0)
    def _():
        ri = lax.broadcasted_iota(jnp.int32, (SGU_BLOCK, SGU_BLOCK), 0) // CHUNK
        ci = lax.broadcasted_iota(jnp.int32, (SGU_BLOCK, SGU_BLOCK), 1) // CHUNK
        for gi in range(SGU_GROUPS):
            wsm_ref[gi] = jnp.where(ri >= ci, ws_ref[gi], 0.0).astype(BF16)

    v = v_ref[...].astype(F32)
    mu = jnp.mean(v, axis=-1, keepdims=True)
    vc = v - mu
    var = jnp.mean(vc * vc, axis=-1, keepdims=True)
    vn = (vc * lax.rsqrt(var + EPS) * lng_ref[...] + lnb_ref[...]).astype(BF16)
    tm = v.shape[0]
    for nb in range(tm // SGU_BLOCK):
        rows = slice(nb * SGU_BLOCK, (nb + 1) * SGU_BLOCK)
        for gi in range(SGU_GROUPS):
            cols = slice(gi * SGU_GD, (gi + 1) * SGU_GD)
            vs = jnp.dot(wsm_ref[gi], vn[rows, cols], preferred_element_type=F32) + bs_ref[:, cols]
            ug = u_ref[rows, cols].astype(F32) * g_ref[rows, cols].astype(F32)
            z_ref[rows, cols] = (ug * vs).astype(BF16)
    y = jnp.dot(z_ref[...], w_ref[...], preferred_element_type=F32)
    o_ref[...] = x_ref[...] + _rms(y, gain_ref[...])


def _sgu(proj, ln_gain, ln_bias, w_spatial, bias_full, w_out, gain, x2):
    t, d = x2.shape
    tm = SGU_TM
    row = lambda i: (i, 0)
    const2 = lambda i: (0, 0)
    return pl.pallas_call(
        _sgu_kernel,
        grid=(t // tm,),
        in_specs=[
            pl.BlockSpec((tm, SGU_WIDTH), lambda i: (i, 0)),
            pl.BlockSpec((tm, SGU_WIDTH), lambda i: (i, 1)),
            pl.BlockSpec((tm, SGU_WIDTH), lambda i: (i, 2)),
            pl.BlockSpec((1, SGU_WIDTH), const2),
            pl.BlockSpec((1, SGU_WIDTH), const2),
            pl.BlockSpec((SGU_GROUPS, SGU_BLOCK, SGU_BLOCK), lambda i: (0, 0, 0)),
            pl.BlockSpec((SGU_BLOCK, SGU_WIDTH), const2),
            pl.BlockSpec((SGU_WIDTH, d), const2),
            pl.BlockSpec((1, d), const2),
            pl.BlockSpec((tm, d), row),
        ],
        out_specs=pl.BlockSpec((tm, d), row),
        out_shape=jax.ShapeDtypeStruct((t, d), F32),
        scratch_shapes=[
            pltpu.VMEM((SGU_GROUPS, SGU_BLOCK, SGU_BLOCK), BF16),
            pltpu.VMEM((tm, SGU_WIDTH), BF16),
        ],
        compiler_params=pltpu.CompilerParams(
            dimension_semantics=("arbitrary",), vmem_limit_bytes=VMEM_LIMIT),
        name="sgu_mix_outproj",
    )(proj, proj, proj, ln_gain, ln_bias, w_spatial, bias_full, w_out, gain, x2)


def kernel(x, norm_pre, norm_post, gla_w_in, gla_w_gate2, gla_b_gate, gla_o_gain, gla_w_out,
           sgu_w_in, sgu_ln_gain, sgu_ln_bias, sgu_w_spatial, sgu_b_spatial, sgu_w_out):
    b, s, d = x.shape
    t = b * s
    x2 = x.reshape(t, d)

    w_in = gla_w_in.reshape(d, -1)
    w_main = _cast_cols(w_in, GLA_MAIN)
    w_lr = jnp.pad(w_in[:, GLA_MAIN:], ((0, 0), (0, LANES - GLA_RANK))).astype(BF16)
    w2 = jnp.pad(gla_w_gate2[0], ((0, LANES - GLA_RANK), (0, 0))).astype(BF16)
    ri = jnp.arange(GLA_TRI)
    tri = ((ri[:, None] >= ri[None, :]) & (ri[:, None] // CHUNK == ri[None, :] // CHUNK)).astype(BF16)
    proj, lr, gla_w_out_b = _inproj(x2, norm_pre[0:1], w_main, gla_w_out.reshape(GLA_DV, d), w_lr)
    a = _gla_scan(proj.reshape(b, s, GLA_MAIN), lr.reshape(b, s, LANES), w2,
                  gla_b_gate[0:1], gla_o_gain[0:1], tri)
    x2, sgu_w_in_b = _outproj(a.reshape(t, GLA_DV), gla_w_out_b, norm_post[0:1], x2,
                              sgu_w_in.reshape(d, 3 * SGU_WIDTH))

    proj, sgu_w_out_b = _inproj(x2, norm_pre[1:2], sgu_w_in_b, sgu_w_out.reshape(SGU_WIDTH, d))
    bias_full = jnp.repeat(sgu_b_spatial[0].T, SGU_GD, axis=1)
    x2 = _sgu(proj, sgu_ln_gain[0:1], sgu_ln_bias[0:1], sgu_w_spatial[0], bias_full,
              sgu_w_out_b, norm_post[1:2], x2)
    return x2.reshape(b, s, d)
```

```python
import jax
import jax.numpy as jnp
from jax import lax
from jax.experimental import pallas as pl
from jax.experimental.pallas import tpu as pltpu

F32 = jnp.float32
BF16 = jnp.bfloat16

D_MODEL = 2048
EPS = 1e-6
CHUNK = 64

GLA_HEADS = 4
GLA_DK = D_MODEL // 2
GLA_DV = D_MODEL
GLA_DKH = GLA_DK // GLA_HEADS
GLA_DVH = GLA_DV // GLA_HEADS
GLA_RANK = 16
GLA_INV_TAU = 1.0 / 16.0
GLA_MAIN = 2 * GLA_DK + 2 * GLA_DV

SGU_WIDTH = D_MODEL
SGU_BLOCK = 128
SGU_GROUPS = 8
SGU_GD = SGU_WIDTH // SGU_GROUPS

LANES = 128
VMEM_LIMIT = 56 * 1024 * 1024

INPROJ_TM = 512
INPROJ_SLAB = 512
OUT_TM = 512
SGU_TM = 512
GLA_TILE = 512
GLA_TRI = 256
CAST_TM = 1024

GELU_C1 = (2.0 / 3.141592653589793) ** 0.5
GELU_C3 = GELU_C1 * 0.044715


def _rms(x, gain):
    return x * lax.rsqrt(jnp.mean(x * x, axis=-1, keepdims=True) + EPS) * gain


def _gelu(r):
    return (0.5 * r) * (1.0 + jnp.tanh(r * (GELU_C1 + GELU_C3 * (r * r))))


def _silu(r):
    hr = 0.5 * r
    return hr * (1.0 + jnp.tanh(hr))


def _dot_nt(a, b_t):
    return lax.dot_general(a, b_t, (((1,), (1,)), ((), ())), preferred_element_type=F32)


def _slab_matmul_act(h, w_ref, o_ref, act_of_col, w_transposed=False):
    n_out = w_ref.shape[0] if w_transposed else w_ref.shape[1]
    for n in range(n_out // INPROJ_SLAB):
        cols = slice(n * INPROJ_SLAB, (n + 1) * INPROJ_SLAB)
        if w_transposed:
            r = _dot_nt(h, w_ref[cols, :])
        else:
            r = jnp.dot(h, w_ref[:, cols], preferred_element_type=F32)
        act = act_of_col(n * INPROJ_SLAB)
        o_ref[:, cols] = (r if act is None else act(r)).astype(o_ref.dtype)


def _sgu_inproj_kernel(x_ref, gain_ref, w_ref, wn_ref, o_ref, wnb_ref):
    wnb_ref[...] = wn_ref[...].astype(BF16)
    h = _rms(x_ref[...], gain_ref[...]).astype(BF16)
    _slab_matmul_act(h, w_ref, o_ref, lambda c: _gelu if c < 2 * SGU_WIDTH else _silu)


def _gla_inproj_kernel(x_ref, gain_ref, w_ref, wlr_ref, wn_ref, o_ref, lr_ref, wnb_ref):
    wnb_ref[...] = wn_ref[...].astype(BF16)
    h = _rms(x_ref[...], gain_ref[...]).astype(BF16)
    lr_ref[...] = _dot_nt(h, wlr_ref[...])
    _slab_matmul_act(h, w_ref, o_ref, lambda c: _silu if c >= 2 * GLA_DK + GLA_DV else None,
                     w_transposed=True)


def _cast_specs(w_next, steps):
    rows, cols = w_next.shape
    spec = pl.BlockSpec((rows // steps, cols), lambda i: (i, 0))
    return spec, spec, jax.ShapeDtypeStruct((rows, cols), BF16)


def _inproj(x2, gain, w, w_next, wlr=None):
    t, d = x2.shape
    n = w.shape[1] if wlr is None else w.shape[0]
    tm = INPROJ_TM
    steps = t // tm
    resident = pl.Buffered(1)
    x_spec = pl.BlockSpec((tm, d), lambda i: (i, 0))
    g_spec = pl.BlockSpec((1, d), lambda i: (0, 0))
    w_spec = pl.BlockSpec(w.shape, lambda i: (0, 0), pipeline_mode=resident)
    o_spec = pl.BlockSpec((tm, n), lambda i: (i, 0))
    wn_in, wn_out, wn_shape = _cast_specs(w_next, steps)
    params = pltpu.CompilerParams(dimension_semantics=("arbitrary",), vmem_limit_bytes=VMEM_LIMIT)
    if wlr is None:
        return pl.pallas_call(
            _sgu_inproj_kernel,
            grid=(steps,),
            in_specs=[x_spec, g_spec, w_spec, wn_in],
            out_specs=[o_spec, wn_out],
            out_shape=[jax.ShapeDtypeStruct((t, n), BF16), wn_shape],
            compiler_params=params,
            name="sgu_inproj",
        )(x2, gain, w, w_next)
    lr_w_spec = pl.BlockSpec((LANES, d), lambda i: (0, 0), pipeline_mode=resident)
    lr_o_spec = pl.BlockSpec((tm, LANES), lambda i: (i, 0))
    return pl.pallas_call(
        _gla_inproj_kernel,
        grid=(steps,),
        in_specs=[x_spec, g_spec, w_spec, lr_w_spec, wn_in],
        out_specs=[o_spec, lr_o_spec, wn_out],
        out_shape=[jax.ShapeDtypeStruct((t, n), BF16), jax.ShapeDtypeStruct((t, LANES), F32),
                   wn_shape],
        compiler_params=params,
        name="gla_inproj",
    )(x2, gain, w, wlr, w_next)


def _cast_kernel(w_ref, o_ref):
    o_ref[...] = w_ref[...].astype(BF16)


def _cast_rows(w, n_rows):
    cols = w.shape[1]
    spec = pl.BlockSpec((CAST_TM, cols), lambda j: (j, 0))
    return pl.pallas_call(
        _cast_kernel,
        grid=(n_rows // CAST_TM,),
        in_specs=[spec],
        out_specs=spec,
        out_shape=jax.ShapeDtypeStruct((n_rows, cols), BF16),
        compiler_params=pltpu.CompilerParams(
            dimension_semantics=("arbitrary",), vmem_limit_bytes=VMEM_LIMIT),
        name="cast_w_in",
    )(w)


def _gla_kernel(q_ref, k_ref, v_ref, g_ref, lr_ref, w2_ref, bg_ref, og_ref, tri_ref,
                a_ref, s_ref, sb_ref, kd_ref, dec_ref):
    @pl.when(pl.program_id(1) == 0)
    def _():
        s_ref[...] = jnp.zeros_like(s_ref)

    tile = q_ref.shape[1]
    n_chunks = tile // CHUNK
    z = jnp.dot(lr_ref[0].astype(BF16), w2_ref[...], preferred_element_type=F32) + bg_ref[...]
    la = (jnp.minimum(z, 0.0) - jnp.log(1.0 + jnp.exp(-jnp.abs(z)))) * GLA_INV_TAU
    hi = la.astype(BF16)
    lo = (la - hi.astype(F32)).astype(BF16)
    tri = tri_ref[...]
    for r in range(tile // GLA_TRI):
        rs = slice(r * GLA_TRI, (r + 1) * GLA_TRI)
        bcum = (jnp.dot(tri, hi[rs], preferred_element_type=F32)
                + jnp.dot(tri, lo[rs], preferred_element_type=F32))
        for cc in range(GLA_TRI // CHUNK):
            c = r * (GLA_TRI // CHUNK) + cc
            bc = bcum[cc * CHUNK:(cc + 1) * CHUNK]
            b_end = bc[CHUNK - 1:CHUNK, :]
            rows = slice(c * CHUNK, (c + 1) * CHUNK)
            kd_ref[rows, :] = (k_ref[0, rows, :].astype(F32) * jnp.exp(b_end - bc)).astype(BF16)
            dec_ref[c:c + 1, :] = jnp.exp(b_end)

    def chunk(c, carry):
        r0 = pl.multiple_of(c * CHUNK, CHUNK)
        rows = pl.ds(r0, CHUNK)
        dec = dec_ref[pl.ds(c, 1), :]
        for h in range(GLA_HEADS):
            kc = slice(h * GLA_DKH, (h + 1) * GLA_DKH)
            vc = slice(h * GLA_DVH, (h + 1) * GLA_DVH)
            upd = lax.dot_general(kd_ref[rows, kc], v_ref[0, rows, vc], (((0,), (0,)), ((), ())),
                                  preferred_element_type=F32)
            decay_col = jnp.transpose(jnp.broadcast_to(dec[:, kc], (LANES, GLA_DKH)))
            for t in range(GLA_DVH // LANES):
                cols = slice(t * LANES, (t + 1) * LANES)
                s_new = s_ref[h, :, cols] * decay_col + upd[:, cols]
                s_ref[h, :, cols] = s_new
                sb_ref[h, :, cols] = s_new.astype(BF16)
        for h in range(GLA_HEADS):
            kc = slice(h * GLA_DKH, (h + 1) * GLA_DKH)
            vc = slice(h * GLA_DVH, (h + 1) * GLA_DVH)
            o = jnp.dot(q_ref[0, rows, kc], sb_ref[h],
                        preferred_element_type=F32) * (GLA_DKH ** -0.5)
            o = _rms(o, og_ref[:, vc])
            a_ref[0, rows, vc] = (o * g_ref[0, rows, vc].astype(F32)).astype(a_ref.dtype)
        return carry

    lax.fori_loop(0, n_chunks, chunk, 0, unroll=2)


def _gla_scan(proj3, lr3, w2, b_gate, o_gain, tri):
    b, s, _ = proj3.shape
    tile = GLA_TILE
    in_specs = [
        pl.BlockSpec((1, tile, GLA_DK), lambda i, t: (i, t, 0)),
        pl.BlockSpec((1, tile, GLA_DK), lambda i, t: (i, t, 1)),
        pl.BlockSpec((1, tile, GLA_DV), lambda i, t: (i, t, 1)),
        pl.BlockSpec((1, tile, GLA_DV), lambda i, t: (i, t, 2)),
        pl.BlockSpec((1, tile, LANES), lambda i, t: (i, t, 0)),
        pl.BlockSpec((LANES, GLA_DK), lambda i, t: (0, 0)),
        pl.BlockSpec((1, GLA_DK), lambda i, t: (0, 0)),
        pl.BlockSpec((1, GLA_DV), lambda i, t: (0, 0)),
        pl.BlockSpec((GLA_TRI, GLA_TRI), lambda i, t: (0, 0)),
    ]
    return pl.pallas_call(
        _gla_kernel,
        grid=(b, s // tile),
        in_specs=in_specs,
        out_specs=pl.BlockSpec((1, tile, GLA_DV), lambda i, t: (i, t, 0)),
        out_shape=jax.ShapeDtypeStruct((b, s, GLA_DV), BF16),
        scratch_shapes=[
            pltpu.VMEM((GLA_HEADS, GLA_DKH, GLA_DVH), F32),
            pltpu.VMEM((GLA_HEADS, GLA_DKH, GLA_DVH), BF16),
            pltpu.VMEM((tile, GLA_DK), BF16),
            pltpu.VMEM((tile // CHUNK, GLA_DK), F32),
        ],
        compiler_params=pltpu.CompilerParams(
            dimension_semantics=("arbitrary", "arbitrary"), vmem_limit_bytes=VMEM_LIMIT),
        name="gla_scan",
    )(proj3, proj3, proj3, proj3, lr3, w2, b_gate, o_gain, tri)


def _outproj_kernel(a_ref, w_ref, gain_ref, x_ref, wn_ref, o_ref, wnb_ref):
    wnb_ref[...] = wn_ref[...].astype(BF16)
    y = jnp.dot(a_ref[...], w_ref[...], preferred_element_type=F32)
    o_ref[...] = x_ref[...] + _rms(y, gain_ref[...])


def _outproj(a2, w, gain, x2, w_next):
    t, d = x2.shape
    k = a2.shape[1]
    tm = OUT_TM
    steps = t // tm
    wn_in, wn_out, wn_shape = _cast_specs(w_next, steps)
    return pl.pallas_call(
        _outproj_kernel,
        grid=(steps,),
        in_specs=[
            pl.BlockSpec((tm, k), lambda i: (i, 0)),
            pl.BlockSpec((k, d), lambda i: (0, 0), pipeline_mode=pl.Buffered(1)),
            pl.BlockSpec((1, d), lambda i: (0, 0)),
            pl.BlockSpec((tm, d), lambda i: (i, 0)),
            wn_in,
        ],
        out_specs=[pl.BlockSpec((tm, d), lambda i: (i, 0)), wn_out],
        out_shape=[jax.ShapeDtypeStruct((t, d), F32), wn_shape],
        compiler_params=pltpu.CompilerParams(
            dimension_semantics=("arbitrary",), vmem_limit_bytes=VMEM_LIMIT),
        name="gla_outproj",
    )(a2, w, gain, x2, w_next)


def _sgu_kernel(u_ref, v_ref, g_ref, lng_ref, lnb_ref, ws_ref, bs_ref, w_ref, gain_ref, x_ref,
                o_ref, wsm_ref, z_ref):
    @pl.when(pl.program_id(0) == 0)
    def _():
        ri = lax.broadcasted_iota(jnp.int32, (SGU_BLOCK, SGU_BLOCK), 0) // CHUNK
        ci = lax.broadcasted_iota(jnp.int32, (SGU_BLOCK, SGU_BLOCK), 1) // CHUNK
        for gi in range(SGU_GROUPS):
            wsm_ref[gi] = jnp.where(ri >= ci, ws_ref[gi], 0.0).astype(BF16)

    v = v_ref[...].astype(F32)
    mu = jnp.mean(v, axis=-1, keepdims=True)
    vc = v - mu
    var = jnp.mean(vc * vc, axis=-1, keepdims=True)
    vn = (vc * lax.rsqrt(var + EPS) * lng_ref[...] + lnb_ref[...]).astype(BF16)
    tm = v.shape[0]
    for nb in range(tm // SGU_BLOCK):
        rows = slice(nb * SGU_BLOCK, (nb + 1) * SGU_BLOCK)
        for gi in range(SGU_GROUPS):
            cols = slice(gi * SGU_GD, (gi + 1) * SGU_GD)
            vs = jnp.dot(wsm_ref[gi], vn[rows, cols], preferred_element_type=F32) + bs_ref[:, cols]
            ug = u_ref[rows, cols].astype(F32) * g_ref[rows, cols].astype(F32)
            z_ref[rows, cols] = (ug * vs).astype(BF16)
    y = jnp.dot(z_ref[...], w_ref[...], preferred_element_type=F32)
    o_ref[...] = x_ref[...] + _rms(y, gain_ref[...])


def _sgu(proj, ln_gain, ln_bias, w_spatial, bias_full, w_out, gain, x2):
    t, d = x2.shape
    tm = SGU_TM
    row = lambda i: (i, 0)
    const2 = lambda i: (0, 0)
    return pl.pallas_call(
        _sgu_kernel,
        grid=(t // tm,),
        in_specs=[
            pl.BlockSpec((tm, SGU_WIDTH), lambda i: (i, 0)),
            pl.BlockSpec((tm, SGU_WIDTH), lambda i: (i, 1)),
            pl.BlockSpec((tm, SGU_WIDTH), lambda i: (i, 2)),
            pl.BlockSpec((1, SGU_WIDTH), const2),
            pl.BlockSpec((1, SGU_WIDTH), const2),
            pl.BlockSpec((SGU_GROUPS, SGU_BLOCK, SGU_BLOCK), lambda i: (0, 0, 0)),
            pl.BlockSpec((SGU_BLOCK, SGU_WIDTH), const2),
            pl.BlockSpec((SGU_WIDTH, d), const2),
            pl.BlockSpec((1, d), const2),
            pl.BlockSpec((tm, d), row),
        ],
        out_specs=pl.BlockSpec((tm, d), row),
        out_shape=jax.ShapeDtypeStruct((t, d), F32),
        scratch_shapes=[
            pltpu.VMEM((SGU_GROUPS, SGU_BLOCK, SGU_BLOCK), BF16),
            pltpu.VMEM((tm, SGU_WIDTH), BF16),
        ],
        compiler_params=pltpu.CompilerParams(
            dimension_semantics=("arbitrary",), vmem_limit_bytes=VMEM_LIMIT),
        name="sgu_mix_outproj",
    )(proj, proj, proj, ln_gain, ln_bias, w_spatial, bias_full, w_out, gain, x2)


def kernel(x, norm_pre, norm_post, gla_w_in, gla_w_gate2, gla_b_gate, gla_o_gain, gla_w_out,
           sgu_w_in, sgu_ln_gain, sgu_ln_bias, sgu_w_spatial, sgu_b_spatial, sgu_w_out):
    b, s, d = x.shape
    t = b * s
    x2 = x.reshape(t, d)

    w_in_t = gla_w_in.reshape(d, -1).T
    w_main = _cast_rows(w_in_t, GLA_MAIN)
    w_lr = jnp.pad(w_in_t[GLA_MAIN:], ((0, LANES - GLA_RANK), (0, 0))).astype(BF16)
    w2 = jnp.pad(gla_w_gate2[0], ((0, LANES - GLA_RANK), (0, 0))).astype(BF16)
    ri = jnp.arange(GLA_TRI)
    tri = ((ri[:, None] >= ri[None, :]) & (ri[:, None] // CHUNK == ri[None, :] // CHUNK)).astype(BF16)
    proj, lr, gla_w_out_b = _inproj(x2, norm_pre[0:1], w_main, gla_w_out.reshape(GLA_DV, d), w_lr)
    a = _gla_scan(proj.reshape(b, s, GLA_MAIN), lr.reshape(b, s, LANES), w2,
                  gla_b_gate[0:1], gla_o_gain[0:1], tri)
    x2, sgu_w_in_b = _outproj(a.reshape(t, GLA_DV), gla_w_out_b, norm_post[0:1], x2,
                              sgu_w_in.reshape(d, 3 * SGU_WIDTH))

    proj, sgu_w_out_b = _inproj(x2, norm_pre[1:2], sgu_w_in_b, sgu_w_out.reshape(SGU_WIDTH, d))
    bias_full = jnp.repeat(sgu_b_spatial[0].T, SGU_GD, axis=1)
    x2 = _sgu(proj, sgu_ln_gain[0:1], sgu_ln_bias[0:1], sgu_w_spatial[0], bias_full,
              sgu_w_out_b, norm_post[1:2], x2)
    return x2.reshape(b, s, d)
```

```python
import jax
import jax.numpy as jnp
from jax import lax
from jax.experimental import pallas as pl
from jax.experimental.pallas import tpu as pltpu

F32 = jnp.float32
BF16 = jnp.bfloat16

D_MODEL = 2048
EPS = 1e-6
CHUNK = 64

GLA_HEADS = 4
GLA_DK = D_MODEL // 2
GLA_DV = D_MODEL
GLA_DKH = GLA_DK // GLA_HEADS
GLA_DVH = GLA_DV // GLA_HEADS
GLA_RANK = 16
GLA_INV_TAU = 1.0 / 16.0
GLA_MAIN = 2 * GLA_DK + 2 * GLA_DV

SGU_WIDTH = D_MODEL
SGU_BLOCK = 128
SGU_GROUPS = 8
SGU_GD = SGU_WIDTH // SGU_GROUPS

LANES = 128
VMEM_LIMIT = 56 * 1024 * 1024

INPROJ_TM = 512
INPROJ_SLAB = 512
OUT_TM = 512
LN_ROWS = 16
GLA_TILE = 512
GLA_TRI = 256
CAST_TM = 1024

GELU_C1 = (2.0 / 3.141592653589793) ** 0.5
GELU_C3 = GELU_C1 * 0.044715


def _rms(x, gain):
    return x * lax.rsqrt(jnp.mean(x * x, axis=-1, keepdims=True) + EPS) * gain


def _gelu(r):
    return (0.5 * r) * (1.0 + jnp.tanh(r * (GELU_C1 + GELU_C3 * (r * r))))


def _silu(r):
    hr = 0.5 * r
    return hr * (1.0 + jnp.tanh(hr))


def _dot_nt(a, b_t):
    return lax.dot_general(a, b_t, (((1,), (1,)), ((), ())), preferred_element_type=F32)


def _cast_specs(w_next, steps):
    rows, cols = w_next.shape
    spec = pl.BlockSpec((rows // steps, cols), lambda i: (i, 0))
    return spec, spec, jax.ShapeDtypeStruct((rows, cols), BF16)


def _gla_inproj_kernel(x_ref, gain_ref, w_ref, wlr_ref, wn_ref, o_ref, lr_ref, wnb_ref):
    wnb_ref[...] = wn_ref[...].astype(BF16)
    h = _rms(x_ref[...], gain_ref[...]).astype(BF16)
    lr_ref[...] = _dot_nt(h, wlr_ref[...])
    for n in range(w_ref.shape[0] // INPROJ_SLAB):
        cols = slice(n * INPROJ_SLAB, (n + 1) * INPROJ_SLAB)
        r = _dot_nt(h, w_ref[cols, :])
        if n * INPROJ_SLAB >= 2 * GLA_DK + GLA_DV:
            r = _silu(r)
        o_ref[:, cols] = r.astype(o_ref.dtype)


def _gla_inproj(x2, gain, w_t, wlr_t, w_next):
    t, d = x2.shape
    n = w_t.shape[0]
    tm = INPROJ_TM
    steps = t // tm
    resident = pl.Buffered(1)
    wn_in, wn_out, wn_shape = _cast_specs(w_next, steps)
    return pl.pallas_call(
        _gla_inproj_kernel,
        grid=(steps,),
        in_specs=[
            pl.BlockSpec((tm, d), lambda i: (i, 0)),
            pl.BlockSpec((1, d), lambda i: (0, 0)),
            pl.BlockSpec((n, d), lambda i: (0, 0), pipeline_mode=resident),
            pl.BlockSpec((LANES, d), lambda i: (0, 0), pipeline_mode=resident),
            wn_in,
        ],
        out_specs=[pl.BlockSpec((tm, n), lambda i: (i, 0)),
                   pl.BlockSpec((tm, LANES), lambda i: (i, 0)), wn_out],
        out_shape=[jax.ShapeDtypeStruct((t, n), BF16), jax.ShapeDtypeStruct((t, LANES), F32),
                   wn_shape],
        compiler_params=pltpu.CompilerParams(
            dimension_semantics=("arbitrary",), vmem_limit_bytes=VMEM_LIMIT),
        name="gla_inproj",
    )(x2, gain, w_t, wlr_t, w_next)


def _sgu_inproj_kernel(x_ref, gain_ref, w_ref, wn_ref, lng_ref, lnb_ref, ws_ref, bs_ref,
                       z_ref, wnb_ref, wsm_ref, va_ref, vn_ref, tg_ref):
    @pl.when(pl.program_id(0) == 0)
    def _():
        ri = lax.broadcasted_iota(jnp.int32, (SGU_BLOCK, SGU_BLOCK), 0) // CHUNK
        ci = lax.broadcasted_iota(jnp.int32, (SGU_BLOCK, SGU_BLOCK), 1) // CHUNK
        for gi in range(SGU_GROUPS):
            wsm_ref[gi] = jnp.where(ri >= ci, ws_ref[gi], 0.0).astype(BF16)

    wnb_ref[...] = wn_ref[...].astype(BF16)
    h = _rms(x_ref[...], gain_ref[...]).astype(BF16)
    tm = h.shape[0]
    slabs = SGU_WIDTH // INPROJ_SLAB

    def proj(n):
        cols = slice(n * INPROJ_SLAB, (n + 1) * INPROJ_SLAB)
        return jnp.dot(h, w_ref[:, cols], preferred_element_type=F32)

    for n in range(slabs):
        va_ref[:, n * INPROJ_SLAB:(n + 1) * INPROJ_SLAB] = _gelu(proj(slabs + n)).astype(BF16)
    for r in range(tm // LN_ROWS):
        rows = slice(r * LN_ROWS, (r + 1) * LN_ROWS)
        v = va_ref[rows, :].astype(F32)
        vc = v - jnp.mean(v, axis=-1, keepdims=True)
        var = jnp.mean(vc * vc, axis=-1, keepdims=True)
        vn_ref[rows, :] = (vc * lax.rsqrt(var + EPS) * lng_ref[...] + lnb_ref[...]).astype(BF16)
    for n in range(slabs):
        tg_ref[:, n * INPROJ_SLAB:(n + 1) * INPROJ_SLAB] = _silu(proj(2 * slabs + n)).astype(BF16)
    for nb in range(tm // SGU_BLOCK):
        rows = slice(nb * SGU_BLOCK, (nb + 1) * SGU_BLOCK)
        for gi in range(SGU_GROUPS):
            cols = slice(gi * SGU_GD, (gi + 1) * SGU_GD)
            vs = (jnp.dot(wsm_ref[gi], vn_ref[rows, cols], preferred_element_type=F32)
                  + bs_ref[:, cols])
            tg_ref[rows, cols] = (vs * tg_ref[rows, cols].astype(F32)).astype(BF16)
    for n in range(slabs):
        cols = slice(n * INPROJ_SLAB, (n + 1) * INPROJ_SLAB)
        z_ref[:, cols] = (_gelu(proj(n)) * tg_ref[:, cols].astype(F32)).astype(BF16)


def _sgu_inproj(x2, gain, w, w_next, ln_gain, ln_bias, w_spatial, bias_full):
    t, d = x2.shape
    n = w.shape[1]
    tm = INPROJ_TM
    steps = t // tm
    resident = pl.Buffered(1)
    const2 = lambda i: (0, 0)
    wn_in, wn_out, wn_shape = _cast_specs(w_next, steps)
    return pl.pallas_call(
        _sgu_inproj_kernel,
        grid=(steps,),
        in_specs=[
            pl.BlockSpec((tm, d), lambda i: (i, 0)),
            pl.BlockSpec((1, d), const2),
            pl.BlockSpec((d, n), const2, pipeline_mode=resident),
            wn_in,
            pl.BlockSpec((1, SGU_WIDTH), const2),
            pl.BlockSpec((1, SGU_WIDTH), const2),
            pl.BlockSpec((SGU_GROUPS, SGU_BLOCK, SGU_BLOCK), lambda i: (0, 0, 0),
                         pipeline_mode=resident),
            pl.BlockSpec((SGU_BLOCK, SGU_WIDTH), const2, pipeline_mode=resident),
        ],
        out_specs=[pl.BlockSpec((tm, SGU_WIDTH), lambda i: (i, 0)), wn_out],
        out_shape=[jax.ShapeDtypeStruct((t, SGU_WIDTH), BF16), wn_shape],
        scratch_shapes=[
            pltpu.VMEM((SGU_GROUPS, SGU_BLOCK, SGU_BLOCK), BF16),
            pltpu.VMEM((tm, SGU_WIDTH), BF16),
            pltpu.VMEM((tm, SGU_WIDTH), BF16),
            pltpu.VMEM((tm, SGU_WIDTH), BF16),
        ],
        compiler_params=pltpu.CompilerParams(
            dimension_semantics=("arbitrary",), vmem_limit_bytes=VMEM_LIMIT),
        name="sgu_inproj_mix",
    )(x2, gain, w, w_next, ln_gain, ln_bias, w_spatial, bias_full)


def _cast_kernel(w_ref, o_ref):
    o_ref[...] = w_ref[...].astype(BF16)


def _cast_rows(w, n_rows):
    cols = w.shape[1]
    spec = pl.BlockSpec((CAST_TM, cols), lambda j: (j, 0))
    return pl.pallas_call(
        _cast_kernel,
        grid=(n_rows // CAST_TM,),
        in_specs=[spec],
        out_specs=spec,
        out_shape=jax.ShapeDtypeStruct((n_rows, cols), BF16),
        compiler_params=pltpu.CompilerParams(
            dimension_semantics=("arbitrary",), vmem_limit_bytes=VMEM_LIMIT),
        name="cast_w_in",
    )(w)


def _gla_kernel(q_ref, k_ref, v_ref, g_ref, lr_ref, w2_ref, bg_ref, og_ref, tri_ref,
                a_ref, s_ref, sb_ref, kd_ref, dec_ref):
    @pl.when(pl.program_id(1) == 0)
    def _():
        s_ref[...] = jnp.zeros_like(s_ref)

    tile = q_ref.shape[1]
    n_chunks = tile // CHUNK
    z = jnp.dot(lr_ref[0].astype(BF16), w2_ref[...], preferred_element_type=F32) + bg_ref[...]
    la = (jnp.minimum(z, 0.0) - jnp.log(1.0 + jnp.exp(-jnp.abs(z)))) * GLA_INV_TAU
    hi = la.astype(BF16)
    lo = (la - hi.astype(F32)).astype(BF16)
    tri = tri_ref[...]
    for r in range(tile // GLA_TRI):
        rs = slice(r * GLA_TRI, (r + 1) * GLA_TRI)
        bcum = (jnp.dot(tri, hi[rs], preferred_element_type=F32)
                + jnp.dot(tri, lo[rs], preferred_element_type=F32))
        for cc in range(GLA_TRI // CHUNK):
            c = r * (GLA_TRI // CHUNK) + cc
            bc = bcum[cc * CHUNK:(cc + 1) * CHUNK]
            b_end = bc[CHUNK - 1:CHUNK, :]
            rows = slice(c * CHUNK, (c + 1) * CHUNK)
            kd_ref[rows, :] = (k_ref[0, rows, :].astype(F32) * jnp.exp(b_end - bc)).astype(BF16)
            dec_ref[c:c + 1, :] = jnp.exp(b_end)

    def chunk(c, carry):
        r0 = pl.multiple_of(c * CHUNK, CHUNK)
        rows = pl.ds(r0, CHUNK)
        dec = dec_ref[pl.ds(c, 1), :]
        for h in range(GLA_HEADS):
            kc = slice(h * GLA_DKH, (h + 1) * GLA_DKH)
            vc = slice(h * GLA_DVH, (h + 1) * GLA_DVH)
            upd = lax.dot_general(kd_ref[rows, kc], v_ref[0, rows, vc], (((0,), (0,)), ((), ())),
                                  preferred_element_type=F32)
            decay_col = jnp.transpose(jnp.broadcast_to(dec[:, kc], (LANES, GLA_DKH)))
            for t in range(GLA_DVH // LANES):
                cols = slice(t * LANES, (t + 1) * LANES)
                s_new = s_ref[h, :, cols] * decay_col + upd[:, cols]
                s_ref[h, :, cols] = s_new
                sb_ref[h, :, cols] = s_new.astype(BF16)
        for h in range(GLA_HEADS):
            kc = slice(h * GLA_DKH, (h + 1) * GLA_DKH)
            vc = slice(h * GLA_DVH, (h + 1) * GLA_DVH)
            o = jnp.dot(q_ref[0, rows, kc], sb_ref[h],
                        preferred_element_type=F32) * (GLA_DKH ** -0.5)
            o = _rms(o, og_ref[:, vc])
            a_ref[0, rows, vc] = (o * g_ref[0, rows, vc].astype(F32)).astype(a_ref.dtype)
        return carry

    lax.fori_loop(0, n_chunks, chunk, 0, unroll=2)


def _gla_scan(proj3, lr3, w2, b_gate, o_gain, tri):
    b, s, _ = proj3.shape
    tile = GLA_TILE
    in_specs = [
        pl.BlockSpec((1, tile, GLA_DK), lambda i, t: (i, t, 0)),
        pl.BlockSpec((1, tile, GLA_DK), lambda i, t: (i, t, 1)),
        pl.BlockSpec((1, tile, GLA_DV), lambda i, t: (i, t, 1)),
        pl.BlockSpec((1, tile, GLA_DV), lambda i, t: (i, t, 2)),
        pl.BlockSpec((1, tile, LANES), lambda i, t: (i, t, 0)),
        pl.BlockSpec((LANES, GLA_DK), lambda i, t: (0, 0)),
        pl.BlockSpec((1, GLA_DK), lambda i, t: (0, 0)),
        pl.BlockSpec((1, GLA_DV), lambda i, t: (0, 0)),
        pl.BlockSpec((GLA_TRI, GLA_TRI), lambda i, t: (0, 0)),
    ]
    return pl.pallas_call(
        _gla_kernel,
        grid=(b, s // tile),
        in_specs=in_specs,
        out_specs=pl.BlockSpec((1, tile, GLA_DV), lambda i, t: (i, t, 0)),
        out_shape=jax.ShapeDtypeStruct((b, s, GLA_DV), BF16),
        scratch_shapes=[
            pltpu.VMEM((GLA_HEADS, GLA_DKH, GLA_DVH), F32),
            pltpu.VMEM((GLA_HEADS, GLA_DKH, GLA_DVH), BF16),
            pltpu.VMEM((tile, GLA_DK), BF16),
            pltpu.VMEM((tile // CHUNK, GLA_DK), F32),
        ],
        compiler_params=pltpu.CompilerParams(
            dimension_semantics=("arbitrary", "arbitrary"), vmem_limit_bytes=VMEM_LIMIT),
        name="gla_scan",
    )(proj3, proj3, proj3, proj3, lr3, w2, b_gate, o_gain, tri)


def _outproj_kernel(a_ref, w_ref, gain_ref, x_ref, o_ref):
    y = jnp.dot(a_ref[...], w_ref[...], preferred_element_type=F32)
    o_ref[...] = x_ref[...] + _rms(y, gain_ref[...])


def _outproj_cast_kernel(a_ref, w_ref, gain_ref, x_ref, wn_ref, o_ref, wnb_ref):
    wnb_ref[...] = wn_ref[...].astype(BF16)
    _outproj_kernel(a_ref, w_ref, gain_ref, x_ref, o_ref)


def _outproj(a2, w, gain, x2, w_next=None, name="outproj"):
    t, d = x2.shape
    k = a2.shape[1]
    tm = OUT_TM
    steps = t // tm
    in_specs = [
        pl.BlockSpec((tm, k), lambda i: (i, 0)),
        pl.BlockSpec((k, d), lambda i: (0, 0), pipeline_mode=pl.Buffered(1)),
        pl.BlockSpec((1, d), lambda i: (0, 0)),
        pl.BlockSpec((tm, d), lambda i: (i, 0)),
    ]
    o_spec = pl.BlockSpec((tm, d), lambda i: (i, 0))
    o_shape = jax.ShapeDtypeStruct((t, d), F32)
    params = pltpu.CompilerParams(dimension_semantics=("arbitrary",), vmem_limit_bytes=VMEM_LIMIT)
    if w_next is None:
        return pl.pallas_call(
            _outproj_kernel, grid=(steps,), in_specs=in_specs, out_specs=o_spec,
            out_shape=o_shape, compiler_params=params, name=name,
        )(a2, w, gain, x2)
    wn_in, wn_out, wn_shape = _cast_specs(w_next, steps)
    return pl.pallas_call(
        _outproj_cast_kernel, grid=(steps,), in_specs=in_specs + [wn_in],
        out_specs=[o_spec, wn_out], out_shape=[o_shape, wn_shape],
        compiler_params=params, name=name,
    )(a2, w, gain, x2, w_next)


def kernel(x, norm_pre, norm_post, gla_w_in, gla_w_gate2, gla_b_gate, gla_o_gain, gla_w_out,
           sgu_w_in, sgu_ln_gain, sgu_ln_bias, sgu_w_spatial, sgu_b_spatial, sgu_w_out):
    b, s, d = x.shape
    t = b * s
    x2 = x.reshape(t, d)

    w_in_t = gla_w_in.reshape(d, -1).T
    w_main = _cast_rows(w_in_t, GLA_MAIN)
    w_lr = jnp.pad(w_in_t[GLA_MAIN:], ((0, LANES - GLA_RANK), (0, 0))).astype(BF16)
    w2 = jnp.pad(gla_w_gate2[0], ((0, LANES - GLA_RANK), (0, 0))).astype(BF16)
    ri = jnp.arange(GLA_TRI)
    tri = ((ri[:, None] >= ri[None, :]) & (ri[:, None] // CHUNK == ri[None, :] // CHUNK)).astype(BF16)
    proj, lr, gla_w_out_b = _gla_inproj(x2, norm_pre[0:1], w_main, w_lr,
                                        gla_w_out.reshape(GLA_DV, d))
    a = _gla_scan(proj.reshape(b, s, GLA_MAIN), lr.reshape(b, s, LANES), w2,
                  gla_b_gate[0:1], gla_o_gain[0:1], tri)
    x2, sgu_w_in_b = _outproj(a.reshape(t, GLA_DV), gla_w_out_b, norm_post[0:1], x2,
                              sgu_w_in.reshape(d, 3 * SGU_WIDTH), name="gla_outproj")

    bias_full = jnp.repeat(sgu_b_spatial[0].T, SGU_GD, axis=1)
    z, sgu_w_out_b = _sgu_inproj(x2, norm_pre[1:2], sgu_w_in_b, sgu_w_out.reshape(SGU_WIDTH, d),
                                 sgu_ln_gain[0:1], sgu_ln_bias[0:1], sgu_w_spatial[0], bias_full)
    x2 = _outproj(z, sgu_w_out_b, norm_post[1:2], x2, name="sgu_outproj")
    return x2.reshape(b, s, d)
```

```python
import jax
import jax.numpy as jnp
from jax import lax
from jax.experimental import pallas as pl
from jax.experimental.pallas import tpu as pltpu

F32 = jnp.float32
BF16 = jnp.bfloat16

D_MODEL = 2048
EPS = 1e-6
CHUNK = 64

GLA_HEADS = 4
GLA_DK = D_MODEL // 2
GLA_DV = D_MODEL
GLA_DKH = GLA_DK // GLA_HEADS
GLA_DVH = GLA_DV // GLA_HEADS
GLA_RANK = 16
GLA_INV_TAU = 1.0 / 16.0
GLA_MAIN = 2 * GLA_DK + 2 * GLA_DV

SGU_WIDTH = D_MODEL
SGU_BLOCK = 128
SGU_GROUPS = 8
SGU_GD = SGU_WIDTH // SGU_GROUPS

LANES = 128
VMEM_LIMIT = 56 * 1024 * 1024

INPROJ_TM = 512
INPROJ_SLAB = 512
OUT_TM = 512
LN_ROWS = 16
GLA_TILE = 512
GLA_TRI = 256
CAST_TM = 1024

GELU_C1 = (2.0 / 3.141592653589793) ** 0.5
GELU_C3 = GELU_C1 * 0.044715


def _rms(x, gain):
    return x * lax.rsqrt(jnp.mean(x * x, axis=-1, keepdims=True) + EPS) * gain


def _gelu(r):
    return (0.5 * r) * (1.0 + jnp.tanh(r * (GELU_C1 + GELU_C3 * (r * r))))


def _silu(r):
    hr = 0.5 * r
    return hr * (1.0 + jnp.tanh(hr))


def _dot_nt(a, b_t):
    return lax.dot_general(a, b_t, (((1,), (1,)), ((), ())), preferred_element_type=F32)


def _cast_specs(w_next, index_map, steps):
    rows, cols = w_next.shape
    spec = pl.BlockSpec((rows // steps, cols), index_map)
    return spec, spec, jax.ShapeDtypeStruct((rows, cols), BF16)


def _gla_inproj_kernel(x_ref, gain_ref, w_ref, wlr_ref, wn_ref, o_ref, lr_ref, wnb_ref):
    wnb_ref[...] = wn_ref[...].astype(BF16)
    h = _rms(x_ref[...], gain_ref[...]).astype(BF16)
    lr_ref[...] = _dot_nt(h, wlr_ref[...])
    for n in range(w_ref.shape[0] // INPROJ_SLAB):
        cols = slice(n * INPROJ_SLAB, (n + 1) * INPROJ_SLAB)
        r = _dot_nt(h, w_ref[cols, :])
        if n * INPROJ_SLAB >= 2 * GLA_DK + GLA_DV:
            r = _silu(r)
        o_ref[:, cols] = r.astype(o_ref.dtype)


def _gla_inproj(x2, gain, w_t, wlr_t, w_next):
    t, d = x2.shape
    n = w_t.shape[0]
    tm = INPROJ_TM
    steps = t // tm
    resident = pl.Buffered(1)
    wn_in, wn_out, wn_shape = _cast_specs(w_next, lambda i: (i, 0), steps)
    return pl.pallas_call(
        _gla_inproj_kernel,
        grid=(steps,),
        in_specs=[
            pl.BlockSpec((tm, d), lambda i: (i, 0)),
            pl.BlockSpec((1, d), lambda i: (0, 0)),
            pl.BlockSpec((n, d), lambda i: (0, 0), pipeline_mode=resident),
            pl.BlockSpec((LANES, d), lambda i: (0, 0), pipeline_mode=resident),
            wn_in,
        ],
        out_specs=[pl.BlockSpec((tm, n), lambda i: (i, 0)),
                   pl.BlockSpec((tm, LANES), lambda i: (i, 0)), wn_out],
        out_shape=[jax.ShapeDtypeStruct((t, n), BF16), jax.ShapeDtypeStruct((t, LANES), F32),
                   wn_shape],
        compiler_params=pltpu.CompilerParams(
            dimension_semantics=("arbitrary",), vmem_limit_bytes=VMEM_LIMIT),
        name="gla_inproj",
    )(x2, gain, w_t, wlr_t, w_next)


def _sgu_inproj_kernel(x_ref, gain_ref, w_ref, wn_ref, lng_ref, lnb_ref, ws_ref, bs_ref,
                       z_ref, wnb_ref, wsm_ref, va_ref, vn_ref, tg_ref):
    @pl.when(pl.program_id(0) == 0)
    def _():
        ri = lax.broadcasted_iota(jnp.int32, (SGU_BLOCK, SGU_BLOCK), 0) // CHUNK
        ci = lax.broadcasted_iota(jnp.int32, (SGU_BLOCK, SGU_BLOCK), 1) // CHUNK
        for gi in range(SGU_GROUPS):
            wsm_ref[gi] = jnp.where(ri >= ci, ws_ref[gi], 0.0).astype(BF16)

    wnb_ref[...] = wn_ref[...].astype(BF16)
    h = _rms(x_ref[...], gain_ref[...]).astype(BF16)
    tm = h.shape[0]
    slabs = SGU_WIDTH // INPROJ_SLAB

    def proj(n):
        cols = slice(n * INPROJ_SLAB, (n + 1) * INPROJ_SLAB)
        return jnp.dot(h, w_ref[:, cols], preferred_element_type=F32)

    for n in range(slabs):
        va_ref[:, n * INPROJ_SLAB:(n + 1) * INPROJ_SLAB] = _gelu(proj(slabs + n)).astype(BF16)
    for r in range(tm // LN_ROWS):
        rows = slice(r * LN_ROWS, (r + 1) * LN_ROWS)
        v = va_ref[rows, :].astype(F32)
        vc = v - jnp.mean(v, axis=-1, keepdims=True)
        var = jnp.mean(vc * vc, axis=-1, keepdims=True)
        vn_ref[rows, :] = (vc * lax.rsqrt(var + EPS) * lng_ref[...] + lnb_ref[...]).astype(BF16)
    for n in range(slabs):
        tg_ref[:, n * INPROJ_SLAB:(n + 1) * INPROJ_SLAB] = _silu(proj(2 * slabs + n)).astype(BF16)
    for nb in range(tm // SGU_BLOCK):
        rows = slice(nb * SGU_BLOCK, (nb + 1) * SGU_BLOCK)
        for gi in range(SGU_GROUPS):
            cols = slice(gi * SGU_GD, (gi + 1) * SGU_GD)
            vs = (jnp.dot(wsm_ref[gi], vn_ref[rows, cols], preferred_element_type=F32)
                  + bs_ref[:, cols])
            tg_ref[rows, cols] = (vs * tg_ref[rows, cols].astype(F32)).astype(BF16)
    for n in range(slabs):
        cols = slice(n * INPROJ_SLAB, (n + 1) * INPROJ_SLAB)
        z_ref[:, cols] = (_gelu(proj(n)) * tg_ref[:, cols].astype(F32)).astype(BF16)


def _sgu_inproj(x2, gain, w, w_next, ln_gain, ln_bias, w_spatial, bias_full):
    t, d = x2.shape
    n = w.shape[1]
    tm = INPROJ_TM
    steps = t // tm
    resident = pl.Buffered(1)
    const2 = lambda i: (0, 0)
    wn_in, wn_out, wn_shape = _cast_specs(w_next, lambda i: (i, 0), steps)
    return pl.pallas_call(
        _sgu_inproj_kernel,
        grid=(steps,),
        in_specs=[
            pl.BlockSpec((tm, d), lambda i: (i, 0)),
            pl.BlockSpec((1, d), const2),
            pl.BlockSpec((d, n), const2, pipeline_mode=resident),
            wn_in,
            pl.BlockSpec((1, SGU_WIDTH), const2),
            pl.BlockSpec((1, SGU_WIDTH), const2),
            pl.BlockSpec((SGU_GROUPS, SGU_BLOCK, SGU_BLOCK), lambda i: (0, 0, 0),
                         pipeline_mode=resident),
            pl.BlockSpec((SGU_BLOCK, SGU_WIDTH), const2, pipeline_mode=resident),
        ],
        out_specs=[pl.BlockSpec((tm, SGU_WIDTH), lambda i: (i, 0)), wn_out],
        out_shape=[jax.ShapeDtypeStruct((t, SGU_WIDTH), BF16), wn_shape],
        scratch_shapes=[
            pltpu.VMEM((SGU_GROUPS, SGU_BLOCK, SGU_BLOCK), BF16),
            pltpu.VMEM((tm, SGU_WIDTH), BF16),
            pltpu.VMEM((tm, SGU_WIDTH), BF16),
            pltpu.VMEM((tm, SGU_WIDTH), BF16),
        ],
        compiler_params=pltpu.CompilerParams(
            dimension_semantics=("arbitrary",), vmem_limit_bytes=VMEM_LIMIT),
        name="sgu_inproj_mix",
    )(x2, gain, w, w_next, ln_gain, ln_bias, w_spatial, bias_full)


def _cast_kernel(w_ref, o_ref):
    o_ref[...] = w_ref[...].astype(BF16)


def _cast_rows(w, n_rows):
    cols = w.shape[1]
    spec = pl.BlockSpec((CAST_TM, cols), lambda j: (j, 0))
    return pl.pallas_call(
        _cast_kernel,
        grid=(n_rows // CAST_TM,),
        in_specs=[spec],
        out_specs=spec,
        out_shape=jax.ShapeDtypeStruct((n_rows, cols), BF16),
        compiler_params=pltpu.CompilerParams(
            dimension_semantics=("arbitrary",), vmem_limit_bytes=VMEM_LIMIT),
        name="cast_w_in",
    )(w)


def _gla_kernel(q_ref, k_ref, v_ref, g_ref, lr_ref, w2_ref, bg_ref, og_ref, tri_ref, wn_ref,
                a_ref, wnb_ref, s_ref, sb_ref, kd_ref, dec_ref):
    @pl.when(pl.program_id(1) == 0)
    def _():
        s_ref[...] = jnp.zeros_like(s_ref)

    wnb_ref[...] = wn_ref[...].astype(BF16)
    tile = q_ref.shape[1]
    n_chunks = tile // CHUNK
    z = jnp.dot(lr_ref[0].astype(BF16), w2_ref[...], preferred_element_type=F32) + bg_ref[...]
    la = (jnp.minimum(z, 0.0) - jnp.log(1.0 + jnp.exp(-jnp.abs(z)))) * GLA_INV_TAU
    hi = la.astype(BF16)
    lo = (la - hi.astype(F32)).astype(BF16)
    tri = tri_ref[...]
    for r in range(tile // GLA_TRI):
        rs = slice(r * GLA_TRI, (r + 1) * GLA_TRI)
        bcum = (jnp.dot(tri, hi[rs], preferred_element_type=F32)
                + jnp.dot(tri, lo[rs], preferred_element_type=F32))
        for cc in range(GLA_TRI // CHUNK):
            c = r * (GLA_TRI // CHUNK) + cc
            bc = bcum[cc * CHUNK:(cc + 1) * CHUNK]
            b_end = bc[CHUNK - 1:CHUNK, :]
            rows = slice(c * CHUNK, (c + 1) * CHUNK)
            kd_ref[rows, :] = (k_ref[0, rows, :].astype(F32) * jnp.exp(b_end - bc)).astype(BF16)
            dec_ref[c:c + 1, :] = jnp.exp(b_end)

    def chunk(c, carry):
        r0 = pl.multiple_of(c * CHUNK, CHUNK)
        rows = pl.ds(r0, CHUNK)
        dec = dec_ref[pl.ds(c, 1), :]
        for h in range(GLA_HEADS):
            kc = slice(h * GLA_DKH, (h + 1) * GLA_DKH)
            vc = slice(h * GLA_DVH, (h + 1) * GLA_DVH)
            upd = lax.dot_general(kd_ref[rows, kc], v_ref[0, rows, vc], (((0,), (0,)), ((), ())),
                                  preferred_element_type=F32)
            decay_col = jnp.transpose(jnp.broadcast_to(dec[:, kc], (LANES, GLA_DKH)))
            for t in range(GLA_DVH // LANES):
                cols = slice(t * LANES, (t + 1) * LANES)
                s_new = s_ref[h, :, cols] * decay_col + upd[:, cols]
                s_ref[h, :, cols] = s_new
                sb_ref[h, :, cols] = s_new.astype(BF16)
        for h in range(GLA_HEADS):
            kc = slice(h * GLA_DKH, (h + 1) * GLA_DKH)
            vc = slice(h * GLA_DVH, (h + 1) * GLA_DVH)
            o = jnp.dot(q_ref[0, rows, kc], sb_ref[h],
                        preferred_element_type=F32) * (GLA_DKH ** -0.5)
            o = _rms(o, og_ref[:, vc])
            a_ref[0, rows, vc] = (o * g_ref[0, rows, vc].astype(F32)).astype(a_ref.dtype)
        return carry

    lax.fori_loop(0, n_chunks, chunk, 0, unroll=2)


def _gla_scan(proj3, lr3, w2, b_gate, o_gain, tri, w_next):
    b, s, _ = proj3.shape
    tile = GLA_TILE
    tiles = s // tile
    const2 = lambda i, t: (0, 0)
    wn_in, wn_out, wn_shape = _cast_specs(w_next, lambda i, t: (i * tiles + t, 0), b * tiles)
    in_specs = [
        pl.BlockSpec((1, tile, GLA_DK), lambda i, t: (i, t, 0)),
        pl.BlockSpec((1, tile, GLA_DK), lambda i, t: (i, t, 1)),
        pl.BlockSpec((1, tile, GLA_DV), lambda i, t: (i, t, 1)),
        pl.BlockSpec((1, tile, GLA_DV), lambda i, t: (i, t, 2)),
        pl.BlockSpec((1, tile, LANES), lambda i, t: (i, t, 0)),
        pl.BlockSpec((LANES, GLA_DK), const2),
        pl.BlockSpec((1, GLA_DK), const2),
        pl.BlockSpec((1, GLA_DV), const2),
        pl.BlockSpec((GLA_TRI, GLA_TRI), const2),
        wn_in,
    ]
    return pl.pallas_call(
        _gla_kernel,
        grid=(b, tiles),
        in_specs=in_specs,
        out_specs=[pl.BlockSpec((1, tile, GLA_DV), lambda i, t: (i, t, 0)), wn_out],
        out_shape=[jax.ShapeDtypeStruct((b, s, GLA_DV), BF16), wn_shape],
        scratch_shapes=[
            pltpu.VMEM((GLA_HEADS, GLA_DKH, GLA_DVH), F32),
            pltpu.VMEM((GLA_HEADS, GLA_DKH, GLA_DVH), BF16),
            pltpu.VMEM((tile, GLA_DK), BF16),
            pltpu.VMEM((tile // CHUNK, GLA_DK), F32),
        ],
        compiler_params=pltpu.CompilerParams(
            dimension_semantics=("arbitrary", "arbitrary"), vmem_limit_bytes=VMEM_LIMIT),
        name="gla_scan",
    )(proj3, proj3, proj3, proj3, lr3, w2, b_gate, o_gain, tri, w_next)


def _outproj_kernel(a_ref, w_ref, gain_ref, x_ref, o_ref):
    y = jnp.dot(a_ref[...], w_ref[...], preferred_element_type=F32)
    o_ref[...] = x_ref[...] + _rms(y, gain_ref[...])


def _outproj(a2, w, gain, x2, name):
    t, d = x2.shape
    k = a2.shape[1]
    tm = OUT_TM
    return pl.pallas_call(
        _outproj_kernel,
        grid=(t // tm,),
        in_specs=[
            pl.BlockSpec((tm, k), lambda i: (i, 0)),
            pl.BlockSpec((k, d), lambda i: (0, 0), pipeline_mode=pl.Buffered(1)),
            pl.BlockSpec((1, d), lambda i: (0, 0)),
            pl.BlockSpec((tm, d), lambda i: (i, 0)),
        ],
        out_specs=pl.BlockSpec((tm, d), lambda i: (i, 0)),
        out_shape=jax.ShapeDtypeStruct((t, d), F32),
        compiler_params=pltpu.CompilerParams(
            dimension_semantics=("arbitrary",), vmem_limit_bytes=VMEM_LIMIT),
        name=name,
    )(a2, w, gain, x2)


def kernel(x, norm_pre, norm_post, gla_w_in, gla_w_gate2, gla_b_gate, gla_o_gain, gla_w_out,
           sgu_w_in, sgu_ln_gain, sgu_ln_bias, sgu_w_spatial, sgu_b_spatial, sgu_w_out):
    b, s, d = x.shape
    t = b * s
    x2 = x.reshape(t, d)

    w_in_t = gla_w_in.reshape(d, -1).T
    w_main = _cast_rows(w_in_t, GLA_MAIN)
    w_lr = jnp.pad(w_in_t[GLA_MAIN:], ((0, LANES - GLA_RANK), (0, 0))).astype(BF16)
    w2 = jnp.pad(gla_w_gate2[0], ((0, LANES - GLA_RANK), (0, 0))).astype(BF16)
    ri = jnp.arange(GLA_TRI)
    tri = ((ri[:, None] >= ri[None, :]) & (ri[:, None] // CHUNK == ri[None, :] // CHUNK)).astype(BF16)
    proj, lr, gla_w_out_b = _gla_inproj(x2, norm_pre[0:1], w_main, w_lr,
                                        gla_w_out.reshape(GLA_DV, d))
    a, sgu_w_in_b = _gla_scan(proj.reshape(b, s, GLA_MAIN), lr.reshape(b, s, LANES), w2,
                              gla_b_gate[0:1], gla_o_gain[0:1], tri,
                              sgu_w_in.reshape(d, 3 * SGU_WIDTH))
    x2 = _outproj(a.reshape(t, GLA_DV), gla_w_out_b, norm_post[0:1], x2, name="gla_outproj")

    bias_full = jnp.repeat(sgu_b_spatial[0].T, SGU_GD, axis=1)
    z, sgu_w_out_b = _sgu_inproj(x2, norm_pre[1:2], sgu_w_in_b, sgu_w_out.reshape(SGU_WIDTH, d),
                                 sgu_ln_gain[0:1], sgu_ln_bias[0:1], sgu_w_spatial[0], bias_full)
    x2 = _outproj(z, sgu_w_out_b, norm_post[1:2], x2, name="sgu_outproj")
    return x2.reshape(b, s, d)
```

```python
import jax
import jax.numpy as jnp
from jax import lax
from jax.experimental import pallas as pl
from jax.experimental.pallas import tpu as pltpu

F32 = jnp.float32
BF16 = jnp.bfloat16

D_MODEL = 2048
EPS = 1e-6
CHUNK = 64

GLA_HEADS = 4
GLA_DK = D_MODEL // 2
GLA_DV = D_MODEL
GLA_DKH = GLA_DK // GLA_HEADS
GLA_DVH = GLA_DV // GLA_HEADS
GLA_RANK = 16
GLA_INV_TAU = 1.0 / 16.0
GLA_MAIN = 2 * GLA_DK + 2 * GLA_DV

SGU_WIDTH = D_MODEL
SGU_BLOCK = 128
SGU_GROUPS = 8
SGU_GD = SGU_WIDTH // SGU_GROUPS

LANES = 128
VMEM_LIMIT = 56 * 1024 * 1024

INPROJ_TM = 512
INPROJ_SLAB = 512
OUT_TM = 512
LN_ROWS = 16
GLA_TILE = 512
GLA_TRI = 256
W_CHUNK = 256

GELU_C1 = (2.0 / 3.141592653589793) ** 0.5
GELU_C3 = GELU_C1 * 0.044715


def _rms(x, gain):
    return x * lax.rsqrt(jnp.mean(x * x, axis=-1, keepdims=True) + EPS) * gain


def _gelu(r):
    return (0.5 * r) * (1.0 + jnp.tanh(r * (GELU_C1 + GELU_C3 * (r * r))))


def _silu(r):
    hr = 0.5 * r
    return hr * (1.0 + jnp.tanh(hr))


def _dot_nt(a, b_t):
    return lax.dot_general(a, b_t, (((1,), (1,)), ((), ())), preferred_element_type=F32)


def _cast_specs(w_next, index_map, steps):
    rows, cols = w_next.shape
    spec = pl.BlockSpec((rows // steps, cols), index_map)
    return spec, spec, jax.ShapeDtypeStruct((rows, cols), BF16)


def _gla_inproj_kernel(x_ref, gain_ref, w_hbm, wlr_ref, wn_ref, o_ref, lr_ref, wnb_ref,
                       w_ref, stage_ref, sem):
    n_slabs = w_ref.shape[0] // INPROJ_SLAB
    per_slab = INPROJ_SLAB // W_CHUNK
    n_chunks = n_slabs * per_slab

    def chunk_copy(k):
        slot = k % per_slab
        return pltpu.make_async_copy(w_hbm.at[pl.ds(k * W_CHUNK, W_CHUNK), :],
                                     stage_ref.at[slot], sem.at[slot])

    def body(stream_weight):
        wnb_ref[...] = wn_ref[...].astype(BF16)
        h = _rms(x_ref[...], gain_ref[...]).astype(BF16)
        lr_ref[...] = _dot_nt(h, wlr_ref[...])
        for n in range(n_slabs):
            cols = slice(n * INPROJ_SLAB, (n + 1) * INPROJ_SLAB)
            if stream_weight:
                for k in range(n * per_slab, (n + 1) * per_slab):
                    chunk_copy(k).wait()
                    w_ref[k * W_CHUNK:(k + 1) * W_CHUNK, :] = stage_ref[k % per_slab].astype(BF16)
                    if k + per_slab < n_chunks:
                        chunk_copy(k + per_slab).start()
            r = _dot_nt(h, w_ref[cols, :])
            if n * INPROJ_SLAB >= 2 * GLA_DK + GLA_DV:
                r = _silu(r)
            o_ref[:, cols] = r.astype(o_ref.dtype)

    @pl.when(pl.program_id(0) == 0)
    def _():
        for k in range(per_slab):
            chunk_copy(k).start()
        body(True)

    @pl.when(pl.program_id(0) > 0)
    def _():
        body(False)


def _gla_inproj(x2, gain, w_t, wlr_t, w_next):
    t, d = x2.shape
    n = GLA_MAIN
    tm = INPROJ_TM
    steps = t // tm
    resident = pl.Buffered(1)
    wn_in, wn_out, wn_shape = _cast_specs(w_next, lambda i: (i, 0), steps)
    return pl.pallas_call(
        _gla_inproj_kernel,
        grid=(steps,),
        in_specs=[
            pl.BlockSpec((tm, d), lambda i: (i, 0)),
            pl.BlockSpec((1, d), lambda i: (0, 0)),
            pl.BlockSpec(memory_space=pl.ANY),
            pl.BlockSpec((LANES, d), lambda i: (0, 0), pipeline_mode=resident),
            wn_in,
        ],
        out_specs=[pl.BlockSpec((tm, n), lambda i: (i, 0)),
                   pl.BlockSpec((tm, LANES), lambda i: (i, 0)), wn_out],
        out_shape=[jax.ShapeDtypeStruct((t, n), BF16), jax.ShapeDtypeStruct((t, LANES), F32),
                   wn_shape],
        scratch_shapes=[
            pltpu.VMEM((n, d), BF16),
            pltpu.VMEM((INPROJ_SLAB // W_CHUNK, W_CHUNK, d), F32),
            pltpu.SemaphoreType.DMA((INPROJ_SLAB // W_CHUNK,)),
        ],
        compiler_params=pltpu.CompilerParams(
            dimension_semantics=("arbitrary",), vmem_limit_bytes=VMEM_LIMIT),
        name="gla_inproj",
    )(x2, gain, w_t, wlr_t, w_next)


def _sgu_inproj_kernel(x_ref, gain_ref, w_ref, wn_ref, lng_ref, lnb_ref, ws_ref, bs_ref,
                       z_ref, wnb_ref, wsm_ref, va_ref, vn_ref, tg_ref):
    @pl.when(pl.program_id(0) == 0)
    def _():
        ri = lax.broadcasted_iota(jnp.int32, (SGU_BLOCK, SGU_BLOCK), 0) // CHUNK
        ci = lax.broadcasted_iota(jnp.int32, (SGU_BLOCK, SGU_BLOCK), 1) // CHUNK
        for gi in range(SGU_GROUPS):
            wsm_ref[gi] = jnp.where(ri >= ci, ws_ref[gi], 0.0).astype(BF16)

    wnb_ref[...] = wn_ref[...].astype(BF16)
    h = _rms(x_ref[...], gain_ref[...]).astype(BF16)
    tm = h.shape[0]
    slabs = SGU_WIDTH // INPROJ_SLAB

    def proj(n):
        cols = slice(n * INPROJ_SLAB, (n + 1) * INPROJ_SLAB)
        return jnp.dot(h, w_ref[:, cols], preferred_element_type=F32)

    for n in range(slabs):
        va_ref[:, n * INPROJ_SLAB:(n + 1) * INPROJ_SLAB] = _gelu(proj(slabs + n)).astype(BF16)
    for r in range(tm // LN_ROWS):
        rows = slice(r * LN_ROWS, (r + 1) * LN_ROWS)
        v = va_ref[rows, :].astype(F32)
        vc = v - jnp.mean(v, axis=-1, keepdims=True)
        var = jnp.mean(vc * vc, axis=-1, keepdims=True)
        vn_ref[rows, :] = (vc * lax.rsqrt(var + EPS) * lng_ref[...] + lnb_ref[...]).astype(BF16)
    for n in range(slabs):
        tg_ref[:, n * INPROJ_SLAB:(n + 1) * INPROJ_SLAB] = _silu(proj(2 * slabs + n)).astype(BF16)
    for nb in range(tm // SGU_BLOCK):
        rows = slice(nb * SGU_BLOCK, (nb + 1) * SGU_BLOCK)
        for gi in range(SGU_GROUPS):
            cols = slice(gi * SGU_GD, (gi + 1) * SGU_GD)
            vs = (jnp.dot(wsm_ref[gi], vn_ref[rows, cols], preferred_element_type=F32)
                  + bs_ref[:, cols])
            tg_ref[rows, cols] = (vs * tg_ref[rows, cols].astype(F32)).astype(BF16)
    for n in range(slabs):
        cols = slice(n * INPROJ_SLAB, (n + 1) * INPROJ_SLAB)
        z_ref[:, cols] = (_gelu(proj(n)) * tg_ref[:, cols].astype(F32)).astype(BF16)


def _sgu_inproj(x2, gain, w, w_next, ln_gain, ln_bias, w_spatial, bias_full):
    t, d = x2.shape
    n = w.shape[1]
    tm = INPROJ_TM
    steps = t // tm
    resident = pl.Buffered(1)
    const2 = lambda i: (0, 0)
    wn_in, wn_out, wn_shape = _cast_specs(w_next, lambda i: (i, 0), steps)
    return pl.pallas_call(
        _sgu_inproj_kernel,
        grid=(steps,),
        in_specs=[
            pl.BlockSpec((tm, d), lambda i: (i, 0)),
            pl.BlockSpec((1, d), const2),
            pl.BlockSpec((d, n), const2, pipeline_mode=resident),
            wn_in,
            pl.BlockSpec((1, SGU_WIDTH), const2),
            pl.BlockSpec((1, SGU_WIDTH), const2),
            pl.BlockSpec((SGU_GROUPS, SGU_BLOCK, SGU_BLOCK), lambda i: (0, 0, 0),
                         pipeline_mode=resident),
            pl.BlockSpec((SGU_BLOCK, SGU_WIDTH), const2, pipeline_mode=resident),
        ],
        out_specs=[pl.BlockSpec((tm, SGU_WIDTH), lambda i: (i, 0)), wn_out],
        out_shape=[jax.ShapeDtypeStruct((t, SGU_WIDTH), BF16), wn_shape],
        scratch_shapes=[
            pltpu.VMEM((SGU_GROUPS, SGU_BLOCK, SGU_BLOCK), BF16),
            pltpu.VMEM((tm, SGU_WIDTH), BF16),
            pltpu.VMEM((tm, SGU_WIDTH), BF16),
            pltpu.VMEM((tm, SGU_WIDTH), BF16),
        ],
        compiler_params=pltpu.CompilerParams(
            dimension_semantics=("arbitrary",), vmem_limit_bytes=VMEM_LIMIT),
        name="sgu_inproj_mix",
    )(x2, gain, w, w_next, ln_gain, ln_bias, w_spatial, bias_full)


def _gla_kernel(q_ref, k_ref, v_ref, g_ref, lr_ref, w2_ref, bg_ref, og_ref, tri_ref, wn_ref,
                a_ref, wnb_ref, s_ref, sb_ref, kd_ref, dec_ref):
    @pl.when(pl.program_id(1) == 0)
    def _():
        s_ref[...] = jnp.zeros_like(s_ref)

    wnb_ref[...] = wn_ref[...].astype(BF16)
    tile = q_ref.shape[1]
    n_chunks = tile // CHUNK
    z = jnp.dot(lr_ref[0].astype(BF16), w2_ref[...], preferred_element_type=F32) + bg_ref[...]
    la = (jnp.minimum(z, 0.0) - jnp.log(1.0 + jnp.exp(-jnp.abs(z)))) * GLA_INV_TAU
    hi = la.astype(BF16)
    lo = (la - hi.astype(F32)).astype(BF16)
    tri = tri_ref[...]
    for r in range(tile // GLA_TRI):
        rs = slice(r * GLA_TRI, (r + 1) * GLA_TRI)
        bcum = (jnp.dot(tri, hi[rs], preferred_element_type=F32)
                + jnp.dot(tri, lo[rs], preferred_element_type=F32))
        for cc in range(GLA_TRI // CHUNK):
            c = r * (GLA_TRI // CHUNK) + cc
            bc = bcum[cc * CHUNK:(cc + 1) * CHUNK]
            b_end = bc[CHUNK - 1:CHUNK, :]
            rows = slice(c * CHUNK, (c + 1) * CHUNK)
            kd_ref[rows, :] = (k_ref[0, rows, :].astype(F32) * jnp.exp(b_end - bc)).astype(BF16)
            dec_ref[c:c + 1, :] = jnp.exp(b_end)

    def chunk(c, carry):
        r0 = pl.multiple_of(c * CHUNK, CHUNK)
        rows = pl.ds(r0, CHUNK)
        dec = dec_ref[pl.ds(c, 1), :]
        for h in range(GLA_HEADS):
            kc = slice(h * GLA_DKH, (h + 1) * GLA_DKH)
            vc = slice(h * GLA_DVH, (h + 1) * GLA_DVH)
            upd = lax.dot_general(kd_ref[rows, kc], v_ref[0, rows, vc], (((0,), (0,)), ((), ())),
                                  preferred_element_type=F32)
            decay_col = jnp.transpose(jnp.broadcast_to(dec[:, kc], (LANES, GLA_DKH)))
            for t in range(GLA_DVH // LANES):
                cols = slice(t * LANES, (t + 1) * LANES)
                s_new = s_ref[h, :, cols] * decay_col + upd[:, cols]
                s_ref[h, :, cols] = s_new
                sb_ref[h, :, cols] = s_new.astype(BF16)
        for h in range(GLA_HEADS):
            kc = slice(h * GLA_DKH, (h + 1) * GLA_DKH)
            vc = slice(h * GLA_DVH, (h + 1) * GLA_DVH)
            o = jnp.dot(q_ref[0, rows, kc], sb_ref[h],
                        preferred_element_type=F32) * (GLA_DKH ** -0.5)
            o = _rms(o, og_ref[:, vc])
            a_ref[0, rows, vc] = (o * g_ref[0, rows, vc].astype(F32)).astype(a_ref.dtype)
        return carry

    lax.fori_loop(0, n_chunks, chunk, 0, unroll=2)


def _gla_scan(proj3, lr3, w2, b_gate, o_gain, tri, w_next):
    b, s, _ = proj3.shape
    tile = GLA_TILE
    tiles = s // tile
    const2 = lambda i, t: (0, 0)
    wn_in, wn_out, wn_shape = _cast_specs(w_next, lambda i, t: (i * tiles + t, 0), b * tiles)
    in_specs = [
        pl.BlockSpec((1, tile, GLA_DK), lambda i, t: (i, t, 0)),
        pl.BlockSpec((1, tile, GLA_DK), lambda i, t: (i, t, 1)),
        pl.BlockSpec((1, tile, GLA_DV), lambda i, t: (i, t, 1)),
        pl.BlockSpec((1, tile, GLA_DV), lambda i, t: (i, t, 2)),
        pl.BlockSpec((1, tile, LANES), lambda i, t: (i, t, 0)),
        pl.BlockSpec((LANES, GLA_DK), const2),
        pl.BlockSpec((1, GLA_DK), const2),
        pl.BlockSpec((1, GLA_DV), const2),
        pl.BlockSpec((GLA_TRI, GLA_TRI), const2),
        wn_in,
    ]
    return pl.pallas_call(
        _gla_kernel,
        grid=(b, tiles),
        in_specs=in_specs,
        out_specs=[pl.BlockSpec((1, tile, GLA_DV), lambda i, t: (i, t, 0)), wn_out],
        out_shape=[jax.ShapeDtypeStruct((b, s, GLA_DV), BF16), wn_shape],
        scratch_shapes=[
            pltpu.VMEM((GLA_HEADS, GLA_DKH, GLA_DVH), F32),
            pltpu.VMEM((GLA_HEADS, GLA_DKH, GLA_DVH), BF16),
            pltpu.VMEM((tile, GLA_DK), BF16),
            pltpu.VMEM((tile // CHUNK, GLA_DK), F32),
        ],
        compiler_params=pltpu.CompilerParams(
            dimension_semantics=("arbitrary", "arbitrary"), vmem_limit_bytes=VMEM_LIMIT),
        name="gla_scan",
    )(proj3, proj3, proj3, proj3, lr3, w2, b_gate, o_gain, tri, w_next)


def _outproj_kernel(a_ref, w_ref, gain_ref, x_ref, o_ref):
    y = jnp.dot(a_ref[...], w_ref[...], preferred_element_type=F32)
    o_ref[...] = x_ref[...] + _rms(y, gain_ref[...])


def _outproj(a2, w, gain, x2, name):
    t, d = x2.shape
    k = a2.shape[1]
    tm = OUT_TM
    return pl.pallas_call(
        _outproj_kernel,
        grid=(t // tm,),
        in_specs=[
            pl.BlockSpec((tm, k), lambda i: (i, 0)),
            pl.BlockSpec((k, d), lambda i: (0, 0), pipeline_mode=pl.Buffered(1)),
            pl.BlockSpec((1, d), lambda i: (0, 0)),
            pl.BlockSpec((tm, d), lambda i: (i, 0)),
        ],
        out_specs=pl.BlockSpec((tm, d), lambda i: (i, 0)),
        out_shape=jax.ShapeDtypeStruct((t, d), F32),
        compiler_params=pltpu.CompilerParams(
            dimension_semantics=("arbitrary",), vmem_limit_bytes=VMEM_LIMIT),
        name=name,
    )(a2, w, gain, x2)


def kernel(x, norm_pre, norm_post, gla_w_in, gla_w_gate2, gla_b_gate, gla_o_gain, gla_w_out,
           sgu_w_in, sgu_ln_gain, sgu_ln_bias, sgu_w_spatial, sgu_b_spatial, sgu_w_out):
    b, s, d = x.shape
    t = b * s
    x2 = x.reshape(t, d)

    w_in_t = gla_w_in.reshape(d, -1).T
    w_lr = jnp.pad(w_in_t[GLA_MAIN:], ((0, LANES - GLA_RANK), (0, 0))).astype(BF16)
    w2 = jnp.pad(gla_w_gate2[0], ((0, LANES - GLA_RANK), (0, 0))).astype(BF16)
    ri = jnp.arange(GLA_TRI)
    tri = ((ri[:, None] >= ri[None, :]) & (ri[:, None] // CHUNK == ri[None, :] // CHUNK)).astype(BF16)
    proj, lr, gla_w_out_b = _gla_inproj(x2, norm_pre[0:1], w_in_t, w_lr,
                                        gla_w_out.reshape(GLA_DV, d))
    a, sgu_w_in_b = _gla_scan(proj.reshape(b, s, GLA_MAIN), lr.reshape(b, s, LANES), w2,
                              gla_b_gate[0:1], gla_o_gain[0:1], tri,
                              sgu_w_in.reshape(d, 3 * SGU_WIDTH))
    x2 = _outproj(a.reshape(t, GLA_DV), gla_w_out_b, norm_post[0:1], x2, name="gla_outproj")

    bias_full = jnp.repeat(sgu_b_spatial[0].T, SGU_GD, axis=1)
    z, sgu_w_out_b = _sgu_inproj(x2, norm_pre[1:2], sgu_w_in_b, sgu_w_out.reshape(SGU_WIDTH, d),
                                 sgu_ln_gain[0:1], sgu_ln_bias[0:1], sgu_w_spatial[0], bias_full)
    x2 = _outproj(z, sgu_w_out_b, norm_post[1:2], x2, name="sgu_outproj")
    return x2.reshape(b, s, d)
```

```python
import jax
import jax.numpy as jnp
from jax import lax
from jax.experimental import pallas as pl
from jax.experimental.pallas import tpu as pltpu

F32 = jnp.float32
BF16 = jnp.bfloat16

D_MODEL = 2048
EPS = 1e-6
CHUNK = 64

GLA_HEADS = 4
GLA_DK = D_MODEL // 2
GLA_DV = D_MODEL
GLA_DKH = GLA_DK // GLA_HEADS
GLA_DVH = GLA_DV // GLA_HEADS
GLA_RANK = 16
GLA_INV_TAU = 1.0 / 16.0
GLA_MAIN = 2 * GLA_DK + 2 * GLA_DV

SGU_WIDTH = D_MODEL
SGU_BLOCK = 128
SGU_GROUPS = 8
SGU_GD = SGU_WIDTH // SGU_GROUPS

LANES = 128
VMEM_LIMIT = 56 * 1024 * 1024

INPROJ_TM = 512
INPROJ_SLAB = 512
OUT_TM = 512
LN_ROWS = 16
GLA_TILE = 512
GLA_TRI = 256
W_CHUNK = 128

GELU_C1 = (2.0 / 3.141592653589793) ** 0.5
GELU_C3 = GELU_C1 * 0.044715


def _rms(x, gain):
    return x * lax.rsqrt(jnp.mean(x * x, axis=-1, keepdims=True) + EPS) * gain


def _gelu(r):
    return (0.5 * r) * (1.0 + jnp.tanh(r * (GELU_C1 + GELU_C3 * (r * r))))


def _silu(r):
    hr = 0.5 * r
    return hr * (1.0 + jnp.tanh(hr))


def _dot_nt(a, b_t):
    return lax.dot_general(a, b_t, (((1,), (1,)), ((), ())), preferred_element_type=F32)


def _cast_specs(w_next, index_map, steps):
    rows, cols = w_next.shape
    spec = pl.BlockSpec((rows // steps, cols), index_map)
    return spec, spec, jax.ShapeDtypeStruct((rows, cols), BF16)


def _gla_inproj_kernel(x_ref, gain_ref, w_hbm, wlr_ref, wn_ref, o_ref, lr_ref, wnb_ref,
                       w_ref, stage_ref, sem):
    n_slabs = w_ref.shape[0] // INPROJ_SLAB
    per_slab = INPROJ_SLAB // W_CHUNK
    n_chunks = n_slabs * per_slab

    def chunk_copy(k):
        slot = k % per_slab
        return pltpu.make_async_copy(w_hbm.at[pl.ds(k * W_CHUNK, W_CHUNK), :],
                                     stage_ref.at[slot], sem.at[slot])

    def body(stream_weight):
        wnb_ref[...] = wn_ref[...].astype(BF16)
        h = _rms(x_ref[...], gain_ref[...]).astype(BF16)
        lr_ref[...] = _dot_nt(h, wlr_ref[...])
        for n in range(n_slabs):
            cols = slice(n * INPROJ_SLAB, (n + 1) * INPROJ_SLAB)
            if stream_weight:
                for k in range(n * per_slab, (n + 1) * per_slab):
                    chunk_copy(k).wait()
                    w_ref[k * W_CHUNK:(k + 1) * W_CHUNK, :] = stage_ref[k % per_slab].astype(BF16)
                    if k + per_slab < n_chunks:
                        chunk_copy(k + per_slab).start()
            r = _dot_nt(h, w_ref[cols, :])
            if n * INPROJ_SLAB >= 2 * GLA_DK + GLA_DV:
                r = _silu(r)
            o_ref[:, cols] = r.astype(o_ref.dtype)

    @pl.when(pl.program_id(0) == 0)
    def _():
        for k in range(per_slab):
            chunk_copy(k).start()
        body(True)

    @pl.when(pl.program_id(0) > 0)
    def _():
        body(False)


def _gla_inproj(x2, gain, w_t, wlr_t, w_next):
    t, d = x2.shape
    n = GLA_MAIN
    tm = INPROJ_TM
    steps = t // tm
    resident = pl.Buffered(1)
    wn_in, wn_out, wn_shape = _cast_specs(w_next, lambda i: (i, 0), steps)
    return pl.pallas_call(
        _gla_inproj_kernel,
        grid=(steps,),
        in_specs=[
            pl.BlockSpec((tm, d), lambda i: (i, 0)),
            pl.BlockSpec((1, d), lambda i: (0, 0)),
            pl.BlockSpec(memory_space=pl.ANY),
            pl.BlockSpec((LANES, d), lambda i: (0, 0), pipeline_mode=resident),
            wn_in,
        ],
        out_specs=[pl.BlockSpec((tm, n), lambda i: (i, 0)),
                   pl.BlockSpec((tm, LANES), lambda i: (i, 0)), wn_out],
        out_shape=[jax.ShapeDtypeStruct((t, n), BF16), jax.ShapeDtypeStruct((t, LANES), F32),
                   wn_shape],
        scratch_shapes=[
            pltpu.VMEM((n, d), BF16),
            pltpu.VMEM((INPROJ_SLAB // W_CHUNK, W_CHUNK, d), F32),
            pltpu.SemaphoreType.DMA((INPROJ_SLAB // W_CHUNK,)),
        ],
        compiler_params=pltpu.CompilerParams(
            dimension_semantics=("arbitrary",), vmem_limit_bytes=VMEM_LIMIT),
        name="gla_inproj",
    )(x2, gain, w_t, wlr_t, w_next)


def _sgu_inproj_kernel(x_ref, gain_ref, w_hbm, wn_ref, lng_ref, lnb_ref, ws_ref, bs_ref,
                       z_ref, wnb_ref, w_ref, wsm_ref, va_ref, vn_ref, tg_ref, sem):
    slabs = SGU_WIDTH // INPROJ_SLAB
    order = (list(range(slabs, 2 * slabs)) + list(range(2 * slabs, 3 * slabs))
             + list(range(slabs)))

    def slab_copy(n):
        cols = pl.ds(n * INPROJ_SLAB, INPROJ_SLAB)
        return pltpu.make_async_copy(w_hbm.at[:, cols], w_ref.at[:, cols], sem.at[n])

    def body(stream_weight):
        wnb_ref[...] = wn_ref[...].astype(BF16)
        h = _rms(x_ref[...], gain_ref[...]).astype(BF16)
        tm = h.shape[0]

        def proj(n):
            if stream_weight:
                slab_copy(n).wait()
            cols = slice(n * INPROJ_SLAB, (n + 1) * INPROJ_SLAB)
            return jnp.dot(h, w_ref[:, cols], preferred_element_type=F32)

        for n in range(slabs):
            va_ref[:, n * INPROJ_SLAB:(n + 1) * INPROJ_SLAB] = _gelu(proj(slabs + n)).astype(BF16)
        for r in range(tm // LN_ROWS):
            rows = slice(r * LN_ROWS, (r + 1) * LN_ROWS)
            v = va_ref[rows, :].astype(F32)
            vc = v - jnp.mean(v, axis=-1, keepdims=True)
            var = jnp.mean(vc * vc, axis=-1, keepdims=True)
            vn_ref[rows, :] = (vc * lax.rsqrt(var + EPS) * lng_ref[...] + lnb_ref[...]).astype(BF16)
        for n in range(slabs):
            tg_ref[:, n * INPROJ_SLAB:(n + 1) * INPROJ_SLAB] = (
                _silu(proj(2 * slabs + n)).astype(BF16))
        for nb in range(tm // SGU_BLOCK):
            rows = slice(nb * SGU_BLOCK, (nb + 1) * SGU_BLOCK)
            for gi in range(SGU_GROUPS):
                cols = slice(gi * SGU_GD, (gi + 1) * SGU_GD)
                vs = (jnp.dot(wsm_ref[gi], vn_ref[rows, cols], preferred_element_type=F32)
                      + bs_ref[:, cols])
                tg_ref[rows, cols] = (vs * tg_ref[rows, cols].astype(F32)).astype(BF16)
        for n in range(slabs):
            cols = slice(n * INPROJ_SLAB, (n + 1) * INPROJ_SLAB)
            z_ref[:, cols] = (_gelu(proj(n)) * tg_ref[:, cols].astype(F32)).astype(BF16)

    @pl.when(pl.program_id(0) == 0)
    def _():
        for n in order:
            slab_copy(n).start()
        ri = lax.broadcasted_iota(jnp.int32, (SGU_BLOCK, SGU_BLOCK), 0) // CHUNK
        ci = lax.broadcasted_iota(jnp.int32, (SGU_BLOCK, SGU_BLOCK), 1) // CHUNK
        for gi in range(SGU_GROUPS):
            wsm_ref[gi] = jnp.where(ri >= ci, ws_ref[gi], 0.0).astype(BF16)
        body(True)

    @pl.when(pl.program_id(0) > 0)
    def _():
        body(False)


def _sgu_inproj(x2, gain, w, w_next, ln_gain, ln_bias, w_spatial, bias_full):
    t, d = x2.shape
    n = w.shape[1]
    tm = INPROJ_TM
    steps = t // tm
    resident = pl.Buffered(1)
    const2 = lambda i: (0, 0)
    wn_in, wn_out, wn_shape = _cast_specs(w_next, lambda i: (i, 0), steps)
    return pl.pallas_call(
        _sgu_inproj_kernel,
        grid=(steps,),
        in_specs=[
            pl.BlockSpec((tm, d), lambda i: (i, 0)),
            pl.BlockSpec((1, d), const2),
            pl.BlockSpec(memory_space=pl.ANY),
            wn_in,
            pl.BlockSpec((1, SGU_WIDTH), const2),
            pl.BlockSpec((1, SGU_WIDTH), const2),
            pl.BlockSpec((SGU_GROUPS, SGU_BLOCK, SGU_BLOCK), lambda i: (0, 0, 0),
                         pipeline_mode=resident),
            pl.BlockSpec((SGU_BLOCK, SGU_WIDTH), const2, pipeline_mode=resident),
        ],
        out_specs=[pl.BlockSpec((tm, SGU_WIDTH), lambda i: (i, 0)), wn_out],
        out_shape=[jax.ShapeDtypeStruct((t, SGU_WIDTH), BF16), wn_shape],
        scratch_shapes=[
            pltpu.VMEM((d, n), BF16),
            pltpu.VMEM((SGU_GROUPS, SGU_BLOCK, SGU_BLOCK), BF16),
            pltpu.VMEM((tm, SGU_WIDTH), BF16),
            pltpu.VMEM((tm, SGU_WIDTH), BF16),
            pltpu.VMEM((tm, SGU_WIDTH), BF16),
            pltpu.SemaphoreType.DMA((n // INPROJ_SLAB,)),
        ],
        compiler_params=pltpu.CompilerParams(
            dimension_semantics=("arbitrary",), vmem_limit_bytes=VMEM_LIMIT),
        name="sgu_inproj_mix",
    )(x2, gain, w, w_next, ln_gain, ln_bias, w_spatial, bias_full)


def _gla_kernel(q_ref, k_ref, v_ref, g_ref, lr_ref, w2_ref, bg_ref, og_ref, tri_ref, wn_ref,
                a_ref, wnb_ref, s_ref, sb_ref, kd_ref, dec_ref):
    @pl.when(pl.program_id(1) == 0)
    def _():
        s_ref[...] = jnp.zeros_like(s_ref)

    wnb_ref[...] = wn_ref[...].astype(BF16)
    tile = q_ref.shape[1]
    n_chunks = tile // CHUNK
    z = jnp.dot(lr_ref[0].astype(BF16), w2_ref[...], preferred_element_type=F32) + bg_ref[...]
    la = (jnp.minimum(z, 0.0) - jnp.log(1.0 + jnp.exp(-jnp.abs(z)))) * GLA_INV_TAU
    hi = la.astype(BF16)
    lo = (la - hi.astype(F32)).astype(BF16)
    tri = tri_ref[...]
    for r in range(tile // GLA_TRI):
        rs = slice(r * GLA_TRI, (r + 1) * GLA_TRI)
        bcum = (jnp.dot(tri, hi[rs], preferred_element_type=F32)
                + jnp.dot(tri, lo[rs], preferred_element_type=F32))
        for cc in range(GLA_TRI // CHUNK):
            c = r * (GLA_TRI // CHUNK) + cc
            bc = bcum[cc * CHUNK:(cc + 1) * CHUNK]
            b_end = bc[CHUNK - 1:CHUNK, :]
            rows = slice(c * CHUNK, (c + 1) * CHUNK)
            kd_ref[rows, :] = (k_ref[0, rows, :].astype(F32) * jnp.exp(b_end - bc)).astype(BF16)
            dec_ref[c:c + 1, :] = jnp.exp(b_end)

    def chunk(c, carry):
        r0 = pl.multiple_of(c * CHUNK, CHUNK)
        rows = pl.ds(r0, CHUNK)
        dec = dec_ref[pl.ds(c, 1), :]
        for h in range(GLA_HEADS):
            kc = slice(h * GLA_DKH, (h + 1) * GLA_DKH)
            vc = slice(h * GLA_DVH, (h + 1) * GLA_DVH)
            upd = lax.dot_general(kd_ref[rows, kc], v_ref[0, rows, vc], (((0,), (0,)), ((), ())),
                                  preferred_element_type=F32)
            decay_col = jnp.transpose(jnp.broadcast_to(dec[:, kc], (LANES, GLA_DKH)))
            for t in range(GLA_DVH // LANES):
                cols = slice(t * LANES, (t + 1) * LANES)
                s_new = s_ref[h, :, cols] * decay_col + upd[:, cols]
                s_ref[h, :, cols] = s_new
                sb_ref[h, :, cols] = s_new.astype(BF16)
        for h in range(GLA_HEADS):
            kc = slice(h * GLA_DKH, (h + 1) * GLA_DKH)
            vc = slice(h * GLA_DVH, (h + 1) * GLA_DVH)
            o = jnp.dot(q_ref[0, rows, kc], sb_ref[h],
                        preferred_element_type=F32) * (GLA_DKH ** -0.5)
            o = _rms(o, og_ref[:, vc])
            a_ref[0, rows, vc] = (o * g_ref[0, rows, vc].astype(F32)).astype(a_ref.dtype)
        return carry

    lax.fori_loop(0, n_chunks, chunk, 0, unroll=2)


def _gla_scan(proj3, lr3, w2, b_gate, o_gain, tri, w_next):
    b, s, _ = proj3.shape
    tile = GLA_TILE
    tiles = s // tile
    const2 = lambda i, t: (0, 0)
    wn_in, wn_out, wn_shape = _cast_specs(w_next, lambda i, t: (i * tiles + t, 0), b * tiles)
    in_specs = [
        pl.BlockSpec((1, tile, GLA_DK), lambda i, t: (i, t, 0)),
        pl.BlockSpec((1, tile, GLA_DK), lambda i, t: (i, t, 1)),
        pl.BlockSpec((1, tile, GLA_DV), lambda i, t: (i, t, 1)),
        pl.BlockSpec((1, tile, GLA_DV), lambda i, t: (i, t, 2)),
        pl.BlockSpec((1, tile, LANES), lambda i, t: (i, t, 0)),
        pl.BlockSpec((LANES, GLA_DK), const2),
        pl.BlockSpec((1, GLA_DK), const2),
        pl.BlockSpec((1, GLA_DV), const2),
        pl.BlockSpec((GLA_TRI, GLA_TRI), const2),
        wn_in,
    ]
    return pl.pallas_call(
        _gla_kernel,
        grid=(b, tiles),
        in_specs=in_specs,
        out_specs=[pl.BlockSpec((1, tile, GLA_DV), lambda i, t: (i, t, 0)), wn_out],
        out_shape=[jax.ShapeDtypeStruct((b, s, GLA_DV), BF16), wn_shape],
        scratch_shapes=[
            pltpu.VMEM((GLA_HEADS, GLA_DKH, GLA_DVH), F32),
            pltpu.VMEM((GLA_HEADS, GLA_DKH, GLA_DVH), BF16),
            pltpu.VMEM((tile, GLA_DK), BF16),
            pltpu.VMEM((tile // CHUNK, GLA_DK), F32),
        ],
        compiler_params=pltpu.CompilerParams(
            dimension_semantics=("arbitrary", "arbitrary"), vmem_limit_bytes=VMEM_LIMIT),
        name="gla_scan",
    )(proj3, proj3, proj3, proj3, lr3, w2, b_gate, o_gain, tri, w_next)


def _outproj_kernel(a_ref, w_ref, gain_ref, x_ref, o_ref):
    y = jnp.dot(a_ref[...], w_ref[...], preferred_element_type=F32)
    o_ref[...] = x_ref[...] + _rms(y, gain_ref[...])


def _outproj(a2, w, gain, x2, name):
    t, d = x2.shape
    k = a2.shape[1]
    tm = OUT_TM
    return pl.pallas_call(
        _outproj_kernel,
        grid=(t // tm,),
        in_specs=[
            pl.BlockSpec((tm, k), lambda i: (i, 0)),
            pl.BlockSpec((k, d), lambda i: (0, 0), pipeline_mode=pl.Buffered(1)),
            pl.BlockSpec((1, d), lambda i: (0, 0)),
            pl.BlockSpec((tm, d), lambda i: (i, 0)),
        ],
        out_specs=pl.BlockSpec((tm, d), lambda i: (i, 0)),
        out_shape=jax.ShapeDtypeStruct((t, d), F32),
        compiler_params=pltpu.CompilerParams(
            dimension_semantics=("arbitrary",), vmem_limit_bytes=VMEM_LIMIT),
        name=name,
    )(a2, w, gain, x2)


def kernel(x, norm_pre, norm_post, gla_w_in, gla_w_gate2, gla_b_gate, gla_o_gain, gla_w_out,
           sgu_w_in, sgu_ln_gain, sgu_ln_bias, sgu_w_spatial, sgu_b_spatial, sgu_w_out):
    b, s, d = x.shape
    t = b * s
    x2 = x.reshape(t, d)

    w_in_t = gla_w_in.reshape(d, -1).T
    w_lr = jnp.pad(w_in_t[GLA_MAIN:], ((0, LANES - GLA_RANK), (0, 0))).astype(BF16)
    w2 = jnp.pad(gla_w_gate2[0], ((0, LANES - GLA_RANK), (0, 0))).astype(BF16)
    ri = jnp.arange(GLA_TRI)
    tri = ((ri[:, None] >= ri[None, :]) & (ri[:, None] // CHUNK == ri[None, :] // CHUNK)).astype(BF16)
    proj, lr, gla_w_out_b = _gla_inproj(x2, norm_pre[0:1], w_in_t, w_lr,
                                        gla_w_out.reshape(GLA_DV, d))
    a, sgu_w_in_b = _gla_scan(proj.reshape(b, s, GLA_MAIN), lr.reshape(b, s, LANES), w2,
                              gla_b_gate[0:1], gla_o_gain[0:1], tri,
                              sgu_w_in.reshape(d, 3 * SGU_WIDTH))
    x2 = _outproj(a.reshape(t, GLA_DV), gla_w_out_b, norm_post[0:1], x2, name="gla_outproj")

    bias_full = jnp.repeat(sgu_b_spatial[0].T, SGU_GD, axis=1)
    z, sgu_w_out_b = _sgu_inproj(x2, norm_pre[1:2], sgu_w_in_b, sgu_w_out.reshape(SGU_WIDTH, d),
                                 sgu_ln_gain[0:1], sgu_ln_bias[0:1], sgu_w_spatial[0], bias_full)
    x2 = _outproj(z, sgu_w_out_b, norm_post[1:2], x2, name="sgu_outproj")
    return x2.reshape(b, s, d)
```

```python
import jax
import jax.numpy as jnp
from jax import lax
from jax.experimental import pallas as pl
from jax.experimental.pallas import tpu as pltpu

F32 = jnp.float32
BF16 = jnp.bfloat16

D_MODEL = 2048
EPS = 1e-6
CHUNK = 64

GLA_HEADS = 4
GLA_DK = D_MODEL // 2
GLA_DV = D_MODEL
GLA_DKH = GLA_DK // GLA_HEADS
GLA_DVH = GLA_DV // GLA_HEADS
GLA_RANK = 16
GLA_INV_TAU = 1.0 / 16.0
GLA_MAIN = 2 * GLA_DK + 2 * GLA_DV

SGU_WIDTH = D_MODEL
SGU_BLOCK = 128
SGU_GROUPS = 8
SGU_GD = SGU_WIDTH // SGU_GROUPS

LANES = 128
VMEM_LIMIT = 56 * 1024 * 1024

INPROJ_TM = 512
INPROJ_SLAB = 512
OUT_TM = 512
LN_ROWS = 16
GLA_TILE = 512
GLA_TRI = 256
W_CHUNK = 128
W_INFLIGHT = 3

GELU_C1 = (2.0 / 3.141592653589793) ** 0.5
GELU_C3 = GELU_C1 * 0.044715


def _rms(x, gain):
    return x * lax.rsqrt(jnp.mean(x * x, axis=-1, keepdims=True) + EPS) * gain


def _gelu(r):
    return (0.5 * r) * (1.0 + jnp.tanh(r * (GELU_C1 + GELU_C3 * (r * r))))


def _silu(r):
    hr = 0.5 * r
    return hr * (1.0 + jnp.tanh(hr))


def _dot_nt(a, b_t):
    return lax.dot_general(a, b_t, (((1,), (1,)), ((), ())), preferred_element_type=F32)


def _cast_specs(w_next, index_map, steps):
    rows, cols = w_next.shape
    spec = pl.BlockSpec((rows // steps, cols), index_map)
    return spec, spec, jax.ShapeDtypeStruct((rows, cols), BF16)


def _gla_inproj_kernel(x_ref, gain_ref, w_hbm, wlr_ref, wn_ref, o_ref, lr_ref, wnb_ref,
                       w_ref, stage_ref, sem):
    n_slabs = w_ref.shape[0] // INPROJ_SLAB
    per_slab = INPROJ_SLAB // W_CHUNK
    n_chunks = n_slabs * per_slab

    def chunk_copy(k):
        slot = k % per_slab
        return pltpu.make_async_copy(w_hbm.at[pl.ds(k * W_CHUNK, W_CHUNK), :],
                                     stage_ref.at[slot], sem.at[slot])

    def body(stream_weight):
        wnb_ref[...] = wn_ref[...].astype(BF16)
        h = _rms(x_ref[...], gain_ref[...]).astype(BF16)
        lr_ref[...] = _dot_nt(h, wlr_ref[...])
        for n in range(n_slabs):
            cols = slice(n * INPROJ_SLAB, (n + 1) * INPROJ_SLAB)
            if stream_weight:
                for k in range(n * per_slab, (n + 1) * per_slab):
                    chunk_copy(k).wait()
                    w_ref[k * W_CHUNK:(k + 1) * W_CHUNK, :] = stage_ref[k % per_slab].astype(BF16)
                    if k + per_slab < n_chunks:
                        chunk_copy(k + per_slab).start(priority=k % 2)
            r = _dot_nt(h, w_ref[cols, :])
            if n * INPROJ_SLAB >= 2 * GLA_DK + GLA_DV:
                r = _silu(r)
            o_ref[:, cols] = r.astype(o_ref.dtype)

    @pl.when(pl.program_id(0) == 0)
    def _():
        for k in range(per_slab):
            chunk_copy(k).start(priority=k % 2)
        body(True)

    @pl.when(pl.program_id(0) > 0)
    def _():
        body(False)


def _gla_inproj(x2, gain, w_t, wlr_t, w_next):
    t, d = x2.shape
    n = GLA_MAIN
    tm = INPROJ_TM
    steps = t // tm
    resident = pl.Buffered(1)
    wn_in, wn_out, wn_shape = _cast_specs(w_next, lambda i: (i, 0), steps)
    return pl.pallas_call(
        _gla_inproj_kernel,
        grid=(steps,),
        in_specs=[
            pl.BlockSpec((tm, d), lambda i: (i, 0)),
            pl.BlockSpec((1, d), lambda i: (0, 0)),
            pl.BlockSpec(memory_space=pl.ANY),
            pl.BlockSpec((LANES, d), lambda i: (0, 0), pipeline_mode=resident),
            wn_in,
        ],
        out_specs=[pl.BlockSpec((tm, n), lambda i: (i, 0)),
                   pl.BlockSpec((tm, LANES), lambda i: (i, 0)), wn_out],
        out_shape=[jax.ShapeDtypeStruct((t, n), BF16), jax.ShapeDtypeStruct((t, LANES), F32),
                   wn_shape],
        scratch_shapes=[
            pltpu.VMEM((n, d), BF16),
            pltpu.VMEM((INPROJ_SLAB // W_CHUNK, W_CHUNK, d), F32),
            pltpu.SemaphoreType.DMA((INPROJ_SLAB // W_CHUNK,)),
        ],
        compiler_params=pltpu.CompilerParams(
            dimension_semantics=("arbitrary",), vmem_limit_bytes=VMEM_LIMIT),
        name="gla_inproj",
    )(x2, gain, w_t, wlr_t, w_next)


def _sgu_inproj_kernel(x_ref, gain_ref, w_hbm, wn_ref, lng_ref, lnb_ref, ws_ref, bs_ref,
                       z_ref, wnb_ref, w_ref, wsm_ref, va_ref, vn_ref, tg_ref, sem):
    slabs = SGU_WIDTH // INPROJ_SLAB
    order = (list(range(slabs, 2 * slabs)) + list(range(2 * slabs, 3 * slabs))
             + list(range(slabs)))

    def slab_copy(j):
        n = order[j]
        return pltpu.make_async_copy(w_hbm.at[n], w_ref.at[n], sem.at[n])

    def body(stream_weight):
        wnb_ref[...] = wn_ref[...].astype(BF16)
        h = _rms(x_ref[...], gain_ref[...]).astype(BF16)
        tm = h.shape[0]

        def proj(n):
            if stream_weight:
                j = order.index(n)
                slab_copy(j).wait()
                if j + W_INFLIGHT < len(order):
                    slab_copy(j + W_INFLIGHT).start(priority=j % 2)
            return jnp.dot(h, w_ref[n], preferred_element_type=F32)

        for n in range(slabs):
            va_ref[:, n * INPROJ_SLAB:(n + 1) * INPROJ_SLAB] = _gelu(proj(slabs + n)).astype(BF16)
        for r in range(tm // LN_ROWS):
            rows = slice(r * LN_ROWS, (r + 1) * LN_ROWS)
            v = va_ref[rows, :].astype(F32)
            vc = v - jnp.mean(v, axis=-1, keepdims=True)
            var = jnp.mean(vc * vc, axis=-1, keepdims=True)
            vn_ref[rows, :] = (vc * lax.rsqrt(var + EPS) * lng_ref[...] + lnb_ref[...]).astype(BF16)
        for n in range(slabs):
            tg_ref[:, n * INPROJ_SLAB:(n + 1) * INPROJ_SLAB] = (
                _silu(proj(2 * slabs + n)).astype(BF16))
        for nb in range(tm // SGU_BLOCK):
            rows = slice(nb * SGU_BLOCK, (nb + 1) * SGU_BLOCK)
            for gi in range(SGU_GROUPS):
                cols = slice(gi * SGU_GD, (gi + 1) * SGU_GD)
                vs = (jnp.dot(wsm_ref[gi], vn_ref[rows, cols], preferred_element_type=F32)
                      + bs_ref[:, cols])
                tg_ref[rows, cols] = (vs * tg_ref[rows, cols].astype(F32)).astype(BF16)
        for n in range(slabs):
            cols = slice(n * INPROJ_SLAB, (n + 1) * INPROJ_SLAB)
            z_ref[:, cols] = (_gelu(proj(n)) * tg_ref[:, cols].astype(F32)).astype(BF16)

    @pl.when(pl.program_id(0) == 0)
    def _():
        for j in range(W_INFLIGHT):
            slab_copy(j).start(priority=j % 2)
        ri = lax.broadcasted_iota(jnp.int32, (SGU_BLOCK, SGU_BLOCK), 0) // CHUNK
        ci = lax.broadcasted_iota(jnp.int32, (SGU_BLOCK, SGU_BLOCK), 1) // CHUNK
        for gi in range(SGU_GROUPS):
            wsm_ref[gi] = jnp.where(ri >= ci, ws_ref[gi], 0.0).astype(BF16)
        body(True)

    @pl.when(pl.program_id(0) > 0)
    def _():
        body(False)


def _sgu_inproj(x2, gain, w, w_next, ln_gain, ln_bias, w_spatial, bias_full):
    t, d = x2.shape
    n = w.shape[0] * w.shape[2]
    tm = INPROJ_TM
    steps = t // tm
    resident = pl.Buffered(1)
    const2 = lambda i: (0, 0)
    wn_in, wn_out, wn_shape = _cast_specs(w_next, lambda i: (i, 0), steps)
    return pl.pallas_call(
        _sgu_inproj_kernel,
        grid=(steps,),
        in_specs=[
            pl.BlockSpec((tm, d), lambda i: (i, 0)),
            pl.BlockSpec((1, d), const2),
            pl.BlockSpec(memory_space=pl.ANY),
            wn_in,
            pl.BlockSpec((1, SGU_WIDTH), const2),
            pl.BlockSpec((1, SGU_WIDTH), const2),
            pl.BlockSpec((SGU_GROUPS, SGU_BLOCK, SGU_BLOCK), lambda i: (0, 0, 0),
                         pipeline_mode=resident),
            pl.BlockSpec((SGU_BLOCK, SGU_WIDTH), const2, pipeline_mode=resident),
        ],
        out_specs=[pl.BlockSpec((tm, SGU_WIDTH), lambda i: (i, 0)), wn_out],
        out_shape=[jax.ShapeDtypeStruct((t, SGU_WIDTH), BF16), wn_shape],
        scratch_shapes=[
            pltpu.VMEM((n // INPROJ_SLAB, d, INPROJ_SLAB), BF16),
            pltpu.VMEM((SGU_GROUPS, SGU_BLOCK, SGU_BLOCK), BF16),
            pltpu.VMEM((tm, SGU_WIDTH), BF16),
            pltpu.VMEM((tm, SGU_WIDTH), BF16),
            pltpu.VMEM((tm, SGU_WIDTH), BF16),
            pltpu.SemaphoreType.DMA((n // INPROJ_SLAB,)),
        ],
        compiler_params=pltpu.CompilerParams(
            dimension_semantics=("arbitrary",), vmem_limit_bytes=VMEM_LIMIT),
        name="sgu_inproj_mix",
    )(x2, gain, w, w_next, ln_gain, ln_bias, w_spatial, bias_full)


def _gla_kernel(q_ref, k_ref, v_ref, g_ref, lr_ref, w2_ref, bg_ref, og_ref, tri_ref, wn_ref,
                a_ref, wnb_ref, s_ref, sb_ref, kd_ref, dec_ref):
    @pl.when(pl.program_id(1) == 0)
    def _():
        s_ref[...] = jnp.zeros_like(s_ref)

    for n in range(wnb_ref.shape[0]):
        wnb_ref[n] = wn_ref[:, n * INPROJ_SLAB:(n + 1) * INPROJ_SLAB].astype(BF16)
    tile = q_ref.shape[1]
    n_chunks = tile // CHUNK
    z = jnp.dot(lr_ref[0].astype(BF16), w2_ref[...], preferred_element_type=F32) + bg_ref[...]
    la = (jnp.minimum(z, 0.0) - jnp.log(1.0 + jnp.exp(-jnp.abs(z)))) * GLA_INV_TAU
    hi = la.astype(BF16)
    lo = (la - hi.astype(F32)).astype(BF16)
    tri = tri_ref[...]
    for r in range(tile // GLA_TRI):
        rs = slice(r * GLA_TRI, (r + 1) * GLA_TRI)
        bcum = (jnp.dot(tri, hi[rs], preferred_element_type=F32)
                + jnp.dot(tri, lo[rs], preferred_element_type=F32))
        for cc in range(GLA_TRI // CHUNK):
            c = r * (GLA_TRI // CHUNK) + cc
            bc = bcum[cc * CHUNK:(cc + 1) * CHUNK]
            b_end = bc[CHUNK - 1:CHUNK, :]
            rows = slice(c * CHUNK, (c + 1) * CHUNK)
            kd_ref[rows, :] = (k_ref[0, rows, :].astype(F32) * jnp.exp(b_end - bc)).astype(BF16)
            dec_ref[c:c + 1, :] = jnp.exp(b_end)

    def chunk(c, carry):
        r0 = pl.multiple_of(c * CHUNK, CHUNK)
        rows = pl.ds(r0, CHUNK)
        dec = dec_ref[pl.ds(c, 1), :]
        for h in range(GLA_HEADS):
            kc = slice(h * GLA_DKH, (h + 1) * GLA_DKH)
            vc = slice(h * GLA_DVH, (h + 1) * GLA_DVH)
            upd = lax.dot_general(kd_ref[rows, kc], v_ref[0, rows, vc], (((0,), (0,)), ((), ())),
                                  preferred_element_type=F32)
            decay_col = jnp.transpose(jnp.broadcast_to(dec[:, kc], (LANES, GLA_DKH)))
            for t in range(GLA_DVH // LANES):
                cols = slice(t * LANES, (t + 1) * LANES)
                s_new = s_ref[h, :, cols] * decay_col + upd[:, cols]
                s_ref[h, :, cols] = s_new
                sb_ref[h, :, cols] = s_new.astype(BF16)
        for h in range(GLA_HEADS):
            kc = slice(h * GLA_DKH, (h + 1) * GLA_DKH)
            vc = slice(h * GLA_DVH, (h + 1) * GLA_DVH)
            o = jnp.dot(q_ref[0, rows, kc], sb_ref[h],
                        preferred_element_type=F32) * (GLA_DKH ** -0.5)
            o = _rms(o, og_ref[:, vc])
            a_ref[0, rows, vc] = (o * g_ref[0, rows, vc].astype(F32)).astype(a_ref.dtype)
        return carry

    lax.fori_loop(0, n_chunks, chunk, 0, unroll=2)


def _gla_scan(proj3, lr3, w2, b_gate, o_gain, tri, w_next):
    b, s, _ = proj3.shape
    tile = GLA_TILE
    tiles = s // tile
    const2 = lambda i, t: (0, 0)
    w_rows, w_cols = w_next.shape
    wn_rows = w_rows // (b * tiles)
    wn_in = pl.BlockSpec((wn_rows, w_cols), lambda i, t: (i * tiles + t, 0))
    wn_out = pl.BlockSpec((w_cols // INPROJ_SLAB, wn_rows, INPROJ_SLAB),
                          lambda i, t: (0, i * tiles + t, 0))
    wn_shape = jax.ShapeDtypeStruct((w_cols // INPROJ_SLAB, w_rows, INPROJ_SLAB), BF16)
    in_specs = [
        pl.BlockSpec((1, tile, GLA_DK), lambda i, t: (i, t, 0)),
        pl.BlockSpec((1, tile, GLA_DK), lambda i, t: (i, t, 1)),
        pl.BlockSpec((1, tile, GLA_DV), lambda i, t: (i, t, 1)),
        pl.BlockSpec((1, tile, GLA_DV), lambda i, t: (i, t, 2)),
        pl.BlockSpec((1, tile, LANES), lambda i, t: (i, t, 0)),
        pl.BlockSpec((LANES, GLA_DK), const2),
        pl.BlockSpec((1, GLA_DK), const2),
        pl.BlockSpec((1, GLA_DV), const2),
        pl.BlockSpec((GLA_TRI, GLA_TRI), const2),
        wn_in,
    ]
    return pl.pallas_call(
        _gla_kernel,
        grid=(b, tiles),
        in_specs=in_specs,
        out_specs=[pl.BlockSpec((1, tile, GLA_DV), lambda i, t: (i, t, 0)), wn_out],
        out_shape=[jax.ShapeDtypeStruct((b, s, GLA_DV), BF16), wn_shape],
        scratch_shapes=[
            pltpu.VMEM((GLA_HEADS, GLA_DKH, GLA_DVH), F32),
            pltpu.VMEM((GLA_HEADS, GLA_DKH, GLA_DVH), BF16),
            pltpu.VMEM((tile, GLA_DK), BF16),
            pltpu.VMEM((tile // CHUNK, GLA_DK), F32),
        ],
        compiler_params=pltpu.CompilerParams(
            dimension_semantics=("arbitrary", "arbitrary"), vmem_limit_bytes=VMEM_LIMIT),
        name="gla_scan",
    )(proj3, proj3, proj3, proj3, lr3, w2, b_gate, o_gain, tri, w_next)


def _outproj_kernel(a_ref, w_ref, gain_ref, x_ref, o_ref):
    y = jnp.dot(a_ref[...], w_ref[...], preferred_element_type=F32)
    o_ref[...] = x_ref[...] + _rms(y, gain_ref[...])


def _outproj(a2, w, gain, x2, name):
    t, d = x2.shape
    k = a2.shape[1]
    tm = OUT_TM
    return pl.pallas_call(
        _outproj_kernel,
        grid=(t // tm,),
        in_specs=[
            pl.BlockSpec((tm, k), lambda i: (i, 0)),
            pl.BlockSpec((k, d), lambda i: (0, 0), pipeline_mode=pl.Buffered(1)),
            pl.BlockSpec((1, d), lambda i: (0, 0)),
            pl.BlockSpec((tm, d), lambda i: (i, 0)),
        ],
        out_specs=pl.BlockSpec((tm, d), lambda i: (i, 0)),
        out_shape=jax.ShapeDtypeStruct((t, d), F32),
        compiler_params=pltpu.CompilerParams(
            dimension_semantics=("arbitrary",), vmem_limit_bytes=VMEM_LIMIT),
        name=name,
    )(a2, w, gain, x2)


def kernel(x, norm_pre, norm_post, gla_w_in, gla_w_gate2, gla_b_gate, gla_o_gain, gla_w_out,
           sgu_w_in, sgu_ln_gain, sgu_ln_bias, sgu_w_spatial, sgu_b_spatial, sgu_w_out):
    b, s, d = x.shape
    t = b * s
    x2 = x.reshape(t, d)

    w_in_t = gla_w_in.reshape(d, -1).T
    w_lr = jnp.pad(w_in_t[GLA_MAIN:], ((0, LANES - GLA_RANK), (0, 0))).astype(BF16)
    w2 = jnp.pad(gla_w_gate2[0], ((0, LANES - GLA_RANK), (0, 0))).astype(BF16)
    ri = jnp.arange(GLA_TRI)
    tri = ((ri[:, None] >= ri[None, :]) & (ri[:, None] // CHUNK == ri[None, :] // CHUNK)).astype(BF16)
    proj, lr, gla_w_out_b = _gla_inproj(x2, norm_pre[0:1], w_in_t, w_lr,
                                        gla_w_out.reshape(GLA_DV, d))
    a, sgu_w_in_b = _gla_scan(proj.reshape(b, s, GLA_MAIN), lr.reshape(b, s, LANES), w2,
                              gla_b_gate[0:1], gla_o_gain[0:1], tri,
                              sgu_w_in.reshape(d, 3 * SGU_WIDTH))
    x2 = _outproj(a.reshape(t, GLA_DV), gla_w_out_b, norm_post[0:1], x2, name="gla_outproj")

    bias_full = jnp.repeat(sgu_b_spatial[0].T, SGU_GD, axis=1)
    z, sgu_w_out_b = _sgu_inproj(x2, norm_pre[1:2], sgu_w_in_b, sgu_w_out.reshape(SGU_WIDTH, d),
                                 sgu_ln_gain[0:1], sgu_ln_bias[0:1], sgu_w_spatial[0], bias_full)
    x2 = _outproj(z, sgu_w_out_b, norm_post[1:2], x2, name="sgu_outproj")
    return x2.reshape(b, s, d)
```

```python
import jax
import jax.numpy as jnp
from jax import lax
from jax.experimental import pallas as pl
from jax.experimental.pallas import tpu as pltpu

F32 = jnp.float32
BF16 = jnp.bfloat16

D_MODEL = 2048
EPS = 1e-6
CHUNK = 64

GLA_HEADS = 4
GLA_DK = D_MODEL // 2
GLA_DV = D_MODEL
GLA_DKH = GLA_DK // GLA_HEADS
GLA_DVH = GLA_DV // GLA_HEADS
GLA_RANK = 16
GLA_INV_TAU = 1.0 / 16.0
GLA_MAIN = 2 * GLA_DK + 2 * GLA_DV

SGU_WIDTH = D_MODEL
SGU_BLOCK = 128
SGU_GROUPS = 8
SGU_GD = SGU_WIDTH // SGU_GROUPS

LANES = 128
VMEM_LIMIT = 56 * 1024 * 1024
GLA_INPROJ_VMEM_LIMIT = 60 * 1024 * 1024

INPROJ_TM = 512
INPROJ_SLAB = 512
OUT_TM = 512
LN_ROWS = 16
GLA_TILE = 512
GLA_TRI = 256
W_CHUNK = 128
W_INFLIGHT = 3

GELU_C1 = (2.0 / 3.141592653589793) ** 0.5
GELU_C3 = GELU_C1 * 0.044715


def _rms(x, gain):
    return x * lax.rsqrt(jnp.mean(x * x, axis=-1, keepdims=True) + EPS) * gain


def _gelu(r):
    return (0.5 * r) * (1.0 + jnp.tanh(r * (GELU_C1 + GELU_C3 * (r * r))))


def _silu(r):
    hr = 0.5 * r
    return hr * (1.0 + jnp.tanh(hr))


def _dot_nt(a, b_t):
    return lax.dot_general(a, b_t, (((1,), (1,)), ((), ())), preferred_element_type=F32)


def _cast_specs(w_next, index_map, steps):
    rows, cols = w_next.shape
    spec = pl.BlockSpec((rows // steps, cols), index_map)
    return spec, spec, jax.ShapeDtypeStruct((rows, cols), BF16)


def _gla_inproj_kernel(x_ref, gain_ref, w_hbm, wlr_ref, w2_ref, bg_ref, tri_ref, wn_ref,
                       o_ref, dec_ref, wnb_ref, w_ref, stage_ref, e_ref, sem):
    per_slab = INPROJ_SLAB // W_CHUNK
    q_slabs = GLA_DK // INPROJ_SLAB
    v_slabs = GLA_DV // INPROJ_SLAB
    q_ids = list(range(q_slabs))
    k_ids = list(range(q_slabs, 2 * q_slabs))
    v_ids = list(range(2 * q_slabs, 2 * q_slabs + v_slabs))
    g_ids = list(range(2 * q_slabs + v_slabs, 2 * q_slabs + 2 * v_slabs))
    order = q_ids + v_ids[:-1] + k_ids + v_ids[-1:] + g_ids
    n_chunks = len(order) * per_slab

    def chunk_copy(s):
        row0 = order[s // per_slab] * INPROJ_SLAB + (s % per_slab) * W_CHUNK
        slot = s % per_slab
        return pltpu.make_async_copy(w_hbm.at[pl.ds(row0, W_CHUNK), :],
                                     stage_ref.at[slot], sem.at[slot])

    def body(stream_weight):
        wnb_ref[...] = wn_ref[...].astype(BF16)
        h = _rms(x_ref[...], gain_ref[...]).astype(BF16)
        tm = h.shape[0]

        def proj(p):
            n = order[p]
            if stream_weight:
                for s in range(p * per_slab, (p + 1) * per_slab):
                    row0 = n * INPROJ_SLAB + (s % per_slab) * W_CHUNK
                    chunk_copy(s).wait()
                    w_ref[row0:row0 + W_CHUNK, :] = stage_ref[s % per_slab].astype(BF16)
                    if s + per_slab < n_chunks:
                        chunk_copy(s + per_slab).start()
            return _dot_nt(h, w_ref[n * INPROJ_SLAB:(n + 1) * INPROJ_SLAB, :])

        def store(n, r):
            o_ref[:, n * INPROJ_SLAB:(n + 1) * INPROJ_SLAB] = r.astype(o_ref.dtype)

        def chunk_decays(r, hi, lo):
            rs = slice(r * GLA_TRI, (r + 1) * GLA_TRI)
            tri = tri_ref[...]
            bcum = (jnp.dot(tri, hi[rs], preferred_element_type=F32)
                    + jnp.dot(tri, lo[rs], preferred_element_type=F32))
            for cc in range(GLA_TRI // CHUNK):
                c = r * (GLA_TRI // CHUNK) + cc
                bc = bcum[cc * CHUNK:(cc + 1) * CHUNK]
                b_end = bc[CHUNK - 1:CHUNK, :]
                e_ref[c * CHUNK:(c + 1) * CHUNK, :] = jnp.exp(b_end - bc).astype(BF16)
                dec_ref[c:c + 1, :] = jnp.exp(b_end)

        p = 0
        lr = _dot_nt(h, wlr_ref[...])
        store(order[p], proj(p)); p += 1
        z = jnp.dot(lr.astype(BF16), w2_ref[...], preferred_element_type=F32) + bg_ref[...]
        for _ in range(1, q_slabs):
            store(order[p], proj(p)); p += 1
        la = (jnp.minimum(z, 0.0) - jnp.log(1.0 + jnp.exp(-jnp.abs(z)))) * GLA_INV_TAU
        hi = la.astype(BF16)
        lo = (la - hi.astype(F32)).astype(BF16)
        for r in range(tm // GLA_TRI):
            store(order[p], proj(p)); p += 1
            chunk_decays(r, hi, lo)
        while order[p] not in k_ids:
            store(order[p], proj(p)); p += 1
        for j in range(q_slabs):
            kc = slice(j * INPROJ_SLAB, (j + 1) * INPROJ_SLAB)
            store(order[p], proj(p) * e_ref[:, kc].astype(F32)); p += 1
        while p < len(order):
            r = proj(p)
            store(order[p], _silu(r) if order[p] in g_ids else r); p += 1

    @pl.when(pl.program_id(0) == 0)
    def _():
        for s in range(per_slab):
            chunk_copy(s).start()
        body(True)

    @pl.when(pl.program_id(0) > 0)
    def _():
        body(False)


def _gla_inproj(x2, gain, w_t, wlr_t, w2, b_gate, tri, w_next):
    t, d = x2.shape
    n = GLA_MAIN
    tm = INPROJ_TM
    steps = t // tm
    resident = pl.Buffered(1)
    const2 = lambda i: (0, 0)
    wn_in, wn_out, wn_shape = _cast_specs(w_next, lambda i: (i, 0), steps)
    return pl.pallas_call(
        _gla_inproj_kernel,
        grid=(steps,),
        in_specs=[
            pl.BlockSpec((tm, d), lambda i: (i, 0)),
            pl.BlockSpec((1, d), const2),
            pl.BlockSpec(memory_space=pl.ANY),
            pl.BlockSpec((LANES, d), const2, pipeline_mode=resident),
            pl.BlockSpec((LANES, GLA_DK), const2, pipeline_mode=resident),
            pl.BlockSpec((1, GLA_DK), const2),
            pl.BlockSpec((GLA_TRI, GLA_TRI), const2, pipeline_mode=resident),
            wn_in,
        ],
        out_specs=[pl.BlockSpec((tm, n), lambda i: (i, 0)),
                   pl.BlockSpec((tm // CHUNK, GLA_DK), lambda i: (i, 0)), wn_out],
        out_shape=[jax.ShapeDtypeStruct((t, n), BF16),
                   jax.ShapeDtypeStruct((t // CHUNK, GLA_DK), F32), wn_shape],
        scratch_shapes=[
            pltpu.VMEM((n, d), BF16),
            pltpu.VMEM((INPROJ_SLAB // W_CHUNK, W_CHUNK, d), F32),
            pltpu.VMEM((tm, GLA_DK), BF16),
            pltpu.SemaphoreType.DMA((INPROJ_SLAB // W_CHUNK,)),
        ],
        compiler_params=pltpu.CompilerParams(
            dimension_semantics=("arbitrary",), vmem_limit_bytes=GLA_INPROJ_VMEM_LIMIT),
        name="gla_inproj",
    )(x2, gain, w_t, wlr_t, w2, b_gate, tri, w_next)


def _sgu_inproj_kernel(x_ref, gain_ref, w_hbm, wn_ref, lng_ref, lnb_ref, ws_ref, bs_ref,
                       z_ref, wnb_ref, w_ref, wsm_ref, va_ref, vn_ref, tg_ref, sem):
    slabs = SGU_WIDTH // INPROJ_SLAB
    order = (list(range(slabs, 2 * slabs)) + list(range(2 * slabs, 3 * slabs))
             + list(range(slabs)))

    def slab_copy(j):
        n = order[j]
        return pltpu.make_async_copy(w_hbm.at[n], w_ref.at[n], sem.at[n])

    def body(stream_weight):
        wnb_ref[...] = wn_ref[...].astype(BF16)
        h = _rms(x_ref[...], gain_ref[...]).astype(BF16)
        tm = h.shape[0]

        def proj(n):
            if stream_weight:
                j = order.index(n)
                slab_copy(j).wait()
                if j + W_INFLIGHT < len(order):
                    slab_copy(j + W_INFLIGHT).start(priority=j % 2)
            return jnp.dot(h, w_ref[n], preferred_element_type=F32)

        for n in range(slabs):
            va_ref[:, n * INPROJ_SLAB:(n + 1) * INPROJ_SLAB] = _gelu(proj(slabs + n)).astype(BF16)
        for r in range(tm // LN_ROWS):
            rows = slice(r * LN_ROWS, (r + 1) * LN_ROWS)
            v = va_ref[rows, :].astype(F32)
            vc = v - jnp.mean(v, axis=-1, keepdims=True)
            var = jnp.mean(vc * vc, axis=-1, keepdims=True)
            vn_ref[rows, :] = (vc * lax.rsqrt(var + EPS) * lng_ref[...] + lnb_ref[...]).astype(BF16)
        for n in range(slabs):
            tg_ref[:, n * INPROJ_SLAB:(n + 1) * INPROJ_SLAB] = (
                _silu(proj(2 * slabs + n)).astype(BF16))
        for nb in range(tm // SGU_BLOCK):
            rows = slice(nb * SGU_BLOCK, (nb + 1) * SGU_BLOCK)
            for gi in range(SGU_GROUPS):
                cols = slice(gi * SGU_GD, (gi + 1) * SGU_GD)
                vs = (jnp.dot(wsm_ref[gi], vn_ref[rows, cols], preferred_element_type=F32)
                      + bs_ref[:, cols])
                tg_ref[rows, cols] = (vs * tg_ref[rows, cols].astype(F32)).astype(BF16)
        for n in range(slabs):
            cols = slice(n * INPROJ_SLAB, (n + 1) * INPROJ_SLAB)
            z_ref[:, cols] = (_gelu(proj(n)) * tg_ref[:, cols].astype(F32)).astype(BF16)

    @pl.when(pl.program_id(0) == 0)
    def _():
        for j in range(W_INFLIGHT):
            slab_copy(j).start(priority=j % 2)
        ri = lax.broadcasted_iota(jnp.int32, (SGU_BLOCK, SGU_BLOCK), 0) // CHUNK
        ci = lax.broadcasted_iota(jnp.int32, (SGU_BLOCK, SGU_BLOCK), 1) // CHUNK
        for gi in range(SGU_GROUPS):
            wsm_ref[gi] = jnp.where(ri >= ci, ws_ref[gi], 0.0).astype(BF16)
        body(True)

    @pl.when(pl.program_id(0) > 0)
    def _():
        body(False)


def _sgu_inproj(x2, gain, w, w_next, ln_gain, ln_bias, w_spatial, bias_full):
    t, d = x2.shape
    n = w.shape[0] * w.shape[2]
    tm = INPROJ_TM
    steps = t // tm
    resident = pl.Buffered(1)
    const2 = lambda i: (0, 0)
    wn_in, wn_out, wn_shape = _cast_specs(w_next, lambda i: (i, 0), steps)
    return pl.pallas_call(
        _sgu_inproj_kernel,
        grid=(steps,),
        in_specs=[
            pl.BlockSpec((tm, d), lambda i: (i, 0)),
            pl.BlockSpec((1, d), const2),
            pl.BlockSpec(memory_space=pl.ANY),
            wn_in,
            pl.BlockSpec((1, SGU_WIDTH), const2),
            pl.BlockSpec((1, SGU_WIDTH), const2),
            pl.BlockSpec((SGU_GROUPS, SGU_BLOCK, SGU_BLOCK), lambda i: (0, 0, 0),
                         pipeline_mode=resident),
            pl.BlockSpec((SGU_BLOCK, SGU_WIDTH), const2, pipeline_mode=resident),
        ],
        out_specs=[pl.BlockSpec((tm, SGU_WIDTH), lambda i: (i, 0)), wn_out],
        out_shape=[jax.ShapeDtypeStruct((t, SGU_WIDTH), BF16), wn_shape],
        scratch_shapes=[
            pltpu.VMEM((n // INPROJ_SLAB, d, INPROJ_SLAB), BF16),
            pltpu.VMEM((SGU_GROUPS, SGU_BLOCK, SGU_BLOCK), BF16),
            pltpu.VMEM((tm, SGU_WIDTH), BF16),
            pltpu.VMEM((tm, SGU_WIDTH), BF16),
            pltpu.VMEM((tm, SGU_WIDTH), BF16),
            pltpu.SemaphoreType.DMA((n // INPROJ_SLAB,)),
        ],
        compiler_params=pltpu.CompilerParams(
            dimension_semantics=("arbitrary",), vmem_limit_bytes=VMEM_LIMIT),
        name="sgu_inproj_mix",
    )(x2, gain, w, w_next, ln_gain, ln_bias, w_spatial, bias_full)


def _gla_kernel(q_ref, k_ref, v_ref, g_ref, dec_ref, og_ref, wn_ref, a_ref, wnb_ref,
                s_ref, sb_ref):
    @pl.when(pl.program_id(1) == 0)
    def _():
        s_ref[...] = jnp.zeros_like(s_ref)

    for n in range(wnb_ref.shape[0]):
        wnb_ref[n] = wn_ref[:, n * INPROJ_SLAB:(n + 1) * INPROJ_SLAB].astype(BF16)
    n_chunks = q_ref.shape[1] // CHUNK

    def chunk(c, carry):
        r0 = pl.multiple_of(c * CHUNK, CHUNK)
        rows = pl.ds(r0, CHUNK)
        dec = dec_ref[0, pl.ds(c, 1), :]
        for h in range(GLA_HEADS):
            kc = slice(h * GLA_DKH, (h + 1) * GLA_DKH)
            vc = slice(h * GLA_DVH, (h + 1) * GLA_DVH)
            upd = lax.dot_general(k_ref[0, rows, kc], v_ref[0, rows, vc], (((0,), (0,)), ((), ())),
                                  preferred_element_type=F32)
            decay_col = jnp.transpose(jnp.broadcast_to(dec[:, kc], (LANES, GLA_DKH)))
            for t in range(GLA_DVH // LANES):
                cols = slice(t * LANES, (t + 1) * LANES)
                s_new = s_ref[h, :, cols] * decay_col + upd[:, cols]
                s_ref[h, :, cols] = s_new
                sb_ref[h, :, cols] = s_new.astype(BF16)
        for h in range(GLA_HEADS):
            kc = slice(h * GLA_DKH, (h + 1) * GLA_DKH)
            vc = slice(h * GLA_DVH, (h + 1) * GLA_DVH)
            o = jnp.dot(q_ref[0, rows, kc], sb_ref[h],
                        preferred_element_type=F32) * (GLA_DKH ** -0.5)
            o = _rms(o, og_ref[:, vc])
            a_ref[0, rows, vc] = (o * g_ref[0, rows, vc].astype(F32)).astype(a_ref.dtype)
        return carry

    lax.fori_loop(0, n_chunks, chunk, 0, unroll=2)


def _gla_scan(proj3, dec3, o_gain, w_next):
    b, s, _ = proj3.shape
    tile = GLA_TILE
    tiles = s // tile
    w_rows, w_cols = w_next.shape
    wn_rows = w_rows // (b * tiles)
    wn_in = pl.BlockSpec((wn_rows, w_cols), lambda i, t: (i * tiles + t, 0))
    wn_out = pl.BlockSpec((w_cols // INPROJ_SLAB, wn_rows, INPROJ_SLAB),
                          lambda i, t: (0, i * tiles + t, 0))
    wn_shape = jax.ShapeDtypeStruct((w_cols // INPROJ_SLAB, w_rows, INPROJ_SLAB), BF16)
    in_specs = [
        pl.BlockSpec((1, tile, GLA_DK), lambda i, t: (i, t, 0)),
        pl.BlockSpec((1, tile, GLA_DK), lambda i, t: (i, t, 1)),
        pl.BlockSpec((1, tile, GLA_DV), lambda i, t: (i, t, 1)),
        pl.BlockSpec((1, tile, GLA_DV), lambda i, t: (i, t, 2)),
        pl.BlockSpec((1, tile // CHUNK, GLA_DK), lambda i, t: (i, t, 0)),
        pl.BlockSpec((1, GLA_DV), lambda i, t: (0, 0)),
        wn_in,
    ]
    return pl.pallas_call(
        _gla_kernel,
        grid=(b, tiles),
        in_specs=in_specs,
        out_specs=[pl.BlockSpec((1, tile, GLA_DV), lambda i, t: (i, t, 0)), wn_out],
        out_shape=[jax.ShapeDtypeStruct((b, s, GLA_DV), BF16), wn_shape],
        scratch_shapes=[
            pltpu.VMEM((GLA_HEADS, GLA_DKH, GLA_DVH), F32),
            pltpu.VMEM((GLA_HEADS, GLA_DKH, GLA_DVH), BF16),
        ],
        compiler_params=pltpu.CompilerParams(
            dimension_semantics=("arbitrary", "arbitrary"), vmem_limit_bytes=VMEM_LIMIT),
        name="gla_scan",
    )(proj3, proj3, proj3, proj3, dec3, o_gain, w_next)


def _outproj_kernel(a_ref, w_ref, gain_ref, x_ref, o_ref):
    y = jnp.dot(a_ref[...], w_ref[...], preferred_element_type=F32)
    o_ref[...] = x_ref[...] + _rms(y, gain_ref[...])


def _outproj(a2, w, gain, x2, name):
    t, d = x2.shape
    k = a2.shape[1]
    tm = OUT_TM
    return pl.pallas_call(
        _outproj_kernel,
        grid=(t // tm,),
        in_specs=[
            pl.BlockSpec((tm, k), lambda i: (i, 0)),
            pl.BlockSpec((k, d), lambda i: (0, 0), pipeline_mode=pl.Buffered(1)),
            pl.BlockSpec((1, d), lambda i: (0, 0)),
            pl.BlockSpec((tm, d), lambda i: (i, 0)),
        ],
        out_specs=pl.BlockSpec((tm, d), lambda i: (i, 0)),
        out_shape=jax.ShapeDtypeStruct((t, d), F32),
        compiler_params=pltpu.CompilerParams(
            dimension_semantics=("arbitrary",), vmem_limit_bytes=VMEM_LIMIT),
        name=name,
    )(a2, w, gain, x2)


def kernel(x, norm_pre, norm_post, gla_w_in, gla_w_gate2, gla_b_gate, gla_o_gain, gla_w_out,
           sgu_w_in, sgu_ln_gain, sgu_ln_bias, sgu_w_spatial, sgu_b_spatial, sgu_w_out):
    b, s, d = x.shape
    t = b * s
    x2 = x.reshape(t, d)

    w_in_t = gla_w_in.reshape(d, -1).T
    w_lr = jnp.pad(w_in_t[GLA_MAIN:], ((0, LANES - GLA_RANK), (0, 0))).astype(BF16)
    w2 = jnp.pad(gla_w_gate2[0], ((0, LANES - GLA_RANK), (0, 0))).astype(BF16)
    ri = jnp.arange(GLA_TRI)
    tri = ((ri[:, None] >= ri[None, :]) & (ri[:, None] // CHUNK == ri[None, :] // CHUNK)).astype(BF16)
    proj, dec, gla_w_out_b = _gla_inproj(x2, norm_pre[0:1], w_in_t, w_lr, w2, gla_b_gate[0:1], tri,
                                         gla_w_out.reshape(GLA_DV, d))
    a, sgu_w_in_b = _gla_scan(proj.reshape(b, s, GLA_MAIN), dec.reshape(b, s // CHUNK, GLA_DK),
                              gla_o_gain[0:1], sgu_w_in.reshape(d, 3 * SGU_WIDTH))
    x2 = _outproj(a.reshape(t, GLA_DV), gla_w_out_b, norm_post[0:1], x2, name="gla_outproj")

    bias_full = jnp.repeat(sgu_b_spatial[0].T, SGU_GD, axis=1)
    z, sgu_w_out_b = _sgu_inproj(x2, norm_pre[1:2], sgu_w_in_b, sgu_w_out.reshape(SGU_WIDTH, d),
                                 sgu_ln_gain[0:1], sgu_ln_bias[0:1], sgu_w_spatial[0], bias_full)
    x2 = _outproj(z, sgu_w_out_b, norm_post[1:2], x2, name="sgu_outproj")
    return x2.reshape(b, s, d)
```

```python
import jax
import jax.numpy as jnp
from jax import lax
from jax.experimental import pallas as pl
from jax.experimental.pallas import tpu as pltpu

F32 = jnp.float32
BF16 = jnp.bfloat16

D_MODEL = 2048
EPS = 1e-6
CHUNK = 64

GLA_HEADS = 4
GLA_DK = D_MODEL // 2
GLA_DV = D_MODEL
GLA_DKH = GLA_DK // GLA_HEADS
GLA_DVH = GLA_DV // GLA_HEADS
GLA_RANK = 16
GLA_INV_TAU = 1.0 / 16.0
GLA_MAIN = 2 * GLA_DK + 2 * GLA_DV

SGU_WIDTH = D_MODEL
SGU_BLOCK = 128
SGU_GROUPS = 8
SGU_GD = SGU_WIDTH // SGU_GROUPS

LANES = 128
VMEM_LIMIT = 56 * 1024 * 1024
GLA_INPROJ_VMEM_LIMIT = 60 * 1024 * 1024

INPROJ_TM = 512
INPROJ_SLAB = 512
OUT_TM = 512
LN_ROWS = 16
GLA_TILE = 1024
GLA_TRI = 256
W_CHUNK = 128

GELU_C1 = (2.0 / 3.141592653589793) ** 0.5
GELU_C3 = GELU_C1 * 0.044715


def _rms(x, gain):
    return x * lax.rsqrt(jnp.mean(x * x, axis=-1, keepdims=True) + EPS) * gain


def _gelu(r):
    return (0.5 * r) * (1.0 + jnp.tanh(r * (GELU_C1 + GELU_C3 * (r * r))))


def _silu(r):
    hr = 0.5 * r
    return hr * (1.0 + jnp.tanh(hr))


def _dot_nt(a, b_t):
    return lax.dot_general(a, b_t, (((1,), (1,)), ((), ())), preferred_element_type=F32)


def _cast_specs(w_next, index_map, steps):
    rows, cols = w_next.shape
    spec = pl.BlockSpec((rows // steps, cols), index_map)
    return spec, spec, jax.ShapeDtypeStruct((rows, cols), BF16)


def _gla_inproj_kernel(x_ref, gain_ref, w_hbm, wlr_ref, w2_ref, bg_ref, tri_ref, wn_ref,
                       o_ref, dec_ref, wnb_ref, w_ref, stage_ref, e_ref, sem):
    per_slab = INPROJ_SLAB // W_CHUNK
    q_slabs = GLA_DK // INPROJ_SLAB
    v_slabs = GLA_DV // INPROJ_SLAB
    q_ids = list(range(q_slabs))
    k_ids = list(range(q_slabs, 2 * q_slabs))
    v_ids = list(range(2 * q_slabs, 2 * q_slabs + v_slabs))
    g_ids = list(range(2 * q_slabs + v_slabs, 2 * q_slabs + 2 * v_slabs))
    order = q_ids + v_ids[:-1] + k_ids + v_ids[-1:] + g_ids
    n_chunks = len(order) * per_slab

    def chunk_copy(s):
        row0 = order[s // per_slab] * INPROJ_SLAB + (s % per_slab) * W_CHUNK
        slot = s % per_slab
        return pltpu.make_async_copy(w_hbm.at[pl.ds(row0, W_CHUNK), :],
                                     stage_ref.at[slot], sem.at[slot])

    def body(stream_weight):
        wnb_ref[...] = wn_ref[...].astype(BF16)
        h = _rms(x_ref[...], gain_ref[...]).astype(BF16)
        tm = h.shape[0]

        def proj(p):
            n = order[p]
            if stream_weight:
                for s in range(p * per_slab, (p + 1) * per_slab):
                    row0 = n * INPROJ_SLAB + (s % per_slab) * W_CHUNK
                    chunk_copy(s).wait()
                    w_ref[row0:row0 + W_CHUNK, :] = stage_ref[s % per_slab].astype(BF16)
                    if s + per_slab < n_chunks:
                        chunk_copy(s + per_slab).start()
            return _dot_nt(h, w_ref[n * INPROJ_SLAB:(n + 1) * INPROJ_SLAB, :])

        def store(n, r):
            o_ref[:, n * INPROJ_SLAB:(n + 1) * INPROJ_SLAB] = r.astype(o_ref.dtype)

        def chunk_decays(r, hi, lo):
            rs = slice(r * GLA_TRI, (r + 1) * GLA_TRI)
            tri = tri_ref[...]
            bcum = (jnp.dot(tri, hi[rs], preferred_element_type=F32)
                    + jnp.dot(tri, lo[rs], preferred_element_type=F32))
            for cc in range(GLA_TRI // CHUNK):
                c = r * (GLA_TRI // CHUNK) + cc
                bc = bcum[cc * CHUNK:(cc + 1) * CHUNK]
                b_end = bc[CHUNK - 1:CHUNK, :]
                e_ref[c * CHUNK:(c + 1) * CHUNK, :] = jnp.exp(b_end - bc).astype(BF16)
                dec_ref[c:c + 1, :] = jnp.exp(b_end)

        p = 0
        lr = _dot_nt(h, wlr_ref[...])
        store(order[p], proj(p)); p += 1
        z = jnp.dot(lr.astype(BF16), w2_ref[...], preferred_element_type=F32) + bg_ref[...]
        for _ in range(1, q_slabs):
            store(order[p], proj(p)); p += 1
        la = (jnp.minimum(z, 0.0) - jnp.log(1.0 + jnp.exp(-jnp.abs(z)))) * GLA_INV_TAU
        hi = la.astype(BF16)
        lo = (la - hi.astype(F32)).astype(BF16)
        for r in range(tm // GLA_TRI):
            store(order[p], proj(p)); p += 1
            chunk_decays(r, hi, lo)
        while order[p] not in k_ids:
            store(order[p], proj(p)); p += 1
        for j in range(q_slabs):
            kc = slice(j * INPROJ_SLAB, (j + 1) * INPROJ_SLAB)
            store(order[p], proj(p) * e_ref[:, kc].astype(F32)); p += 1
        while p < len(order):
            r = proj(p)
            store(order[p], _silu(r) if order[p] in g_ids else r); p += 1

    @pl.when(pl.program_id(0) == 0)
    def _():
        for s in range(per_slab):
            chunk_copy(s).start()
        body(True)

    @pl.when(pl.program_id(0) > 0)
    def _():
        body(False)


def _gla_inproj(x2, gain, w_t, wlr_t, w2, b_gate, tri, w_next):
    t, d = x2.shape
    n = GLA_MAIN
    tm = INPROJ_TM
    steps = t // tm
    resident = pl.Buffered(1)
    const2 = lambda i: (0, 0)
    wn_in, wn_out, wn_shape = _cast_specs(w_next, lambda i: (i, 0), steps)
    return pl.pallas_call(
        _gla_inproj_kernel,
        grid=(steps,),
        in_specs=[
            pl.BlockSpec((tm, d), lambda i: (i, 0)),
            pl.BlockSpec((1, d), const2),
            pl.BlockSpec(memory_space=pl.ANY),
            pl.BlockSpec((LANES, d), const2, pipeline_mode=resident),
            pl.BlockSpec((LANES, GLA_DK), const2, pipeline_mode=resident),
            pl.BlockSpec((1, GLA_DK), const2),
            pl.BlockSpec((GLA_TRI, GLA_TRI), const2, pipeline_mode=resident),
            wn_in,
        ],
        out_specs=[pl.BlockSpec((tm, n), lambda i: (i, 0)),
                   pl.BlockSpec((tm // CHUNK, GLA_DK), lambda i: (i, 0)), wn_out],
        out_shape=[jax.ShapeDtypeStruct((t, n), BF16),
                   jax.ShapeDtypeStruct((t // CHUNK, GLA_DK), F32), wn_shape],
        scratch_shapes=[
            pltpu.VMEM((n, d), BF16),
            pltpu.VMEM((INPROJ_SLAB // W_CHUNK, W_CHUNK, d), F32),
            pltpu.VMEM((tm, GLA_DK), BF16),
            pltpu.SemaphoreType.DMA((INPROJ_SLAB // W_CHUNK,)),
        ],
        compiler_params=pltpu.CompilerParams(
            dimension_semantics=("arbitrary",), vmem_limit_bytes=GLA_INPROJ_VMEM_LIMIT),
        name="gla_inproj",
    )(x2, gain, w_t, wlr_t, w2, b_gate, tri, w_next)


def _sgu_inproj_kernel(x_ref, gain_ref, w_ref, wn_ref, lng_ref, lnb_ref, ws_ref, bs_ref,
                       z_ref, wnb_ref, wsm_ref, va_ref, vn_ref, tg_ref):
    @pl.when(pl.program_id(0) == 0)
    def _():
        ri = lax.broadcasted_iota(jnp.int32, (SGU_BLOCK, SGU_BLOCK), 0) // CHUNK
        ci = lax.broadcasted_iota(jnp.int32, (SGU_BLOCK, SGU_BLOCK), 1) // CHUNK
        for gi in range(SGU_GROUPS):
            wsm_ref[gi] = jnp.where(ri >= ci, ws_ref[gi], 0.0).astype(BF16)

    wnb_ref[...] = wn_ref[...].astype(BF16)
    h = _rms(x_ref[...], gain_ref[...]).astype(BF16)
    tm = h.shape[0]
    slabs = SGU_WIDTH // INPROJ_SLAB

    def proj(n):
        return jnp.dot(h, w_ref[n], preferred_element_type=F32)

    for n in range(slabs):
        va_ref[:, n * INPROJ_SLAB:(n + 1) * INPROJ_SLAB] = _gelu(proj(slabs + n)).astype(BF16)
    for r in range(tm // LN_ROWS):
        rows = slice(r * LN_ROWS, (r + 1) * LN_ROWS)
        v = va_ref[rows, :].astype(F32)
        vc = v - jnp.mean(v, axis=-1, keepdims=True)
        var = jnp.mean(vc * vc, axis=-1, keepdims=True)
        vn_ref[rows, :] = (vc * lax.rsqrt(var + EPS) * lng_ref[...] + lnb_ref[...]).astype(BF16)
    for n in range(slabs):
        tg_ref[:, n * INPROJ_SLAB:(n + 1) * INPROJ_SLAB] = _silu(proj(2 * slabs + n)).astype(BF16)
    for nb in range(tm // SGU_BLOCK):
        rows = slice(nb * SGU_BLOCK, (nb + 1) * SGU_BLOCK)
        for gi in range(SGU_GROUPS):
            cols = slice(gi * SGU_GD, (gi + 1) * SGU_GD)
            vs = (jnp.dot(wsm_ref[gi], vn_ref[rows, cols], preferred_element_type=F32)
                  + bs_ref[:, cols])
            tg_ref[rows, cols] = (vs * tg_ref[rows, cols].astype(F32)).astype(BF16)
    for n in range(slabs):
        cols = slice(n * INPROJ_SLAB, (n + 1) * INPROJ_SLAB)
        z_ref[:, cols] = (_gelu(proj(n)) * tg_ref[:, cols].astype(F32)).astype(BF16)


def _sgu_inproj(x2, gain, w, w_next, ln_gain, ln_bias, w_spatial, bias_full):
    t, d = x2.shape
    tm = INPROJ_TM
    steps = t // tm
    resident = pl.Buffered(1)
    const2 = lambda i: (0, 0)
    wn_in, wn_out, wn_shape = _cast_specs(w_next, lambda i: (i, 0), steps)
    return pl.pallas_call(
        _sgu_inproj_kernel,
        grid=(steps,),
        in_specs=[
            pl.BlockSpec((tm, d), lambda i: (i, 0)),
            pl.BlockSpec((1, d), const2),
            pl.BlockSpec(w.shape, lambda i: (0, 0, 0), pipeline_mode=resident),
            wn_in,
            pl.BlockSpec((1, SGU_WIDTH), const2),
            pl.BlockSpec((1, SGU_WIDTH), const2),
            pl.BlockSpec((SGU_GROUPS, SGU_BLOCK, SGU_BLOCK), lambda i: (0, 0, 0),
                         pipeline_mode=resident),
            pl.BlockSpec((SGU_BLOCK, SGU_WIDTH), const2, pipeline_mode=resident),
        ],
        out_specs=[pl.BlockSpec((tm, SGU_WIDTH), lambda i: (i, 0)), wn_out],
        out_shape=[jax.ShapeDtypeStruct((t, SGU_WIDTH), BF16), wn_shape],
        scratch_shapes=[
            pltpu.VMEM((SGU_GROUPS, SGU_BLOCK, SGU_BLOCK), BF16),
            pltpu.VMEM((tm, SGU_WIDTH), BF16),
            pltpu.VMEM((tm, SGU_WIDTH), BF16),
            pltpu.VMEM((tm, SGU_WIDTH), BF16),
        ],
        compiler_params=pltpu.CompilerParams(
            dimension_semantics=("arbitrary",), vmem_limit_bytes=VMEM_LIMIT),
        name="sgu_inproj_mix",
    )(x2, gain, w, w_next, ln_gain, ln_bias, w_spatial, bias_full)


def _gla_kernel(q_ref, k_ref, v_ref, g_ref, dec_ref, og_ref, wn_ref, a_ref, wnb_ref,
                s_ref, sb_ref):
    @pl.when(pl.program_id(1) == 0)
    def _():
        s_ref[...] = jnp.zeros_like(s_ref)

    for n in range(wnb_ref.shape[0]):
        wnb_ref[n] = wn_ref[:, n * INPROJ_SLAB:(n + 1) * INPROJ_SLAB].astype(BF16)
    n_chunks = q_ref.shape[1] // CHUNK

    def chunk(c, carry):
        r0 = pl.multiple_of(c * CHUNK, CHUNK)
        rows = pl.ds(r0, CHUNK)
        dec = dec_ref[0, pl.ds(c, 1), :]
        for h in range(GLA_HEADS):
            kc = slice(h * GLA_DKH, (h + 1) * GLA_DKH)
            vc = slice(h * GLA_DVH, (h + 1) * GLA_DVH)
            upd = lax.dot_general(k_ref[0, rows, kc], v_ref[0, rows, vc], (((0,), (0,)), ((), ())),
                                  preferred_element_type=F32)
            decay_col = jnp.transpose(jnp.broadcast_to(dec[:, kc], (LANES, GLA_DKH)))
            for t in range(GLA_DVH // LANES):
                cols = slice(t * LANES, (t + 1) * LANES)
                s_new = s_ref[h, :, cols] * decay_col + upd[:, cols]
                s_ref[h, :, cols] = s_new
                sb_ref[h, :, cols] = s_new.astype(BF16)
        for h in range(GLA_HEADS):
            kc = slice(h * GLA_DKH, (h + 1) * GLA_DKH)
            vc = slice(h * GLA_DVH, (h + 1) * GLA_DVH)
            o = jnp.dot(q_ref[0, rows, kc], sb_ref[h],
                        preferred_element_type=F32) * (GLA_DKH ** -0.5)
            o = _rms(o, og_ref[:, vc])
            a_ref[0, rows, vc] = (o * g_ref[0, rows, vc].astype(F32)).astype(a_ref.dtype)
        return carry

    lax.fori_loop(0, n_chunks, chunk, 0, unroll=2)


def _gla_scan(proj3, dec3, o_gain, w_next):
    b, s, _ = proj3.shape
    tile = GLA_TILE
    tiles = s // tile
    w_rows, w_cols = w_next.shape
    wn_rows = w_rows // (b * tiles)
    wn_in = pl.BlockSpec((wn_rows, w_cols), lambda i, t: (i * tiles + t, 0))
    wn_out = pl.BlockSpec((w_cols // INPROJ_SLAB, wn_rows, INPROJ_SLAB),
                          lambda i, t: (0, i * tiles + t, 0))
    wn_shape = jax.ShapeDtypeStruct((w_cols // INPROJ_SLAB, w_rows, INPROJ_SLAB), BF16)
    in_specs = [
        pl.BlockSpec((1, tile, GLA_DK), lambda i, t: (i, t, 0)),
        pl.BlockSpec((1, tile, GLA_DK), lambda i, t: (i, t, 1)),
        pl.BlockSpec((1, tile, GLA_DV), lambda i, t: (i, t, 1)),
        pl.BlockSpec((1, tile, GLA_DV), lambda i, t: (i, t, 2)),
        pl.BlockSpec((1, tile // CHUNK, GLA_DK), lambda i, t: (i, t, 0)),
        pl.BlockSpec((1, GLA_DV), lambda i, t: (0, 0)),
        wn_in,
    ]
    return pl.pallas_call(
        _gla_kernel,
        grid=(b, tiles),
        in_specs=in_specs,
        out_specs=[pl.BlockSpec((1, tile, GLA_DV), lambda i, t: (i, t, 0)), wn_out],
        out_shape=[jax.ShapeDtypeStruct((b, s, GLA_DV), BF16), wn_shape],
        scratch_shapes=[
            pltpu.VMEM((GLA_HEADS, GLA_DKH, GLA_DVH), F32),
            pltpu.VMEM((GLA_HEADS, GLA_DKH, GLA_DVH), BF16),
        ],
        compiler_params=pltpu.CompilerParams(
            dimension_semantics=("arbitrary", "arbitrary"), vmem_limit_bytes=VMEM_LIMIT),
        name="gla_scan",
    )(proj3, proj3, proj3, proj3, dec3, o_gain, w_next)


def _outproj_kernel(a_ref, w_ref, gain_ref, x_ref, o_ref):
    y = jnp.dot(a_ref[...], w_ref[...], preferred_element_type=F32)
    o_ref[...] = x_ref[...] + _rms(y, gain_ref[...])


def _outproj(a2, w, gain, x2, name):
    t, d = x2.shape
    k = a2.shape[1]
    tm = OUT_TM
    return pl.pallas_call(
        _outproj_kernel,
        grid=(t // tm,),
        in_specs=[
            pl.BlockSpec((tm, k), lambda i: (i, 0)),
            pl.BlockSpec((k, d), lambda i: (0, 0), pipeline_mode=pl.Buffered(1)),
            pl.BlockSpec((1, d), lambda i: (0, 0)),
            pl.BlockSpec((tm, d), lambda i: (i, 0)),
        ],
        out_specs=pl.BlockSpec((tm, d), lambda i: (i, 0)),
        out_shape=jax.ShapeDtypeStruct((t, d), F32),
        compiler_params=pltpu.CompilerParams(
            dimension_semantics=("arbitrary",), vmem_limit_bytes=VMEM_LIMIT),
        name=name,
    )(a2, w, gain, x2)


def kernel(x, norm_pre, norm_post, gla_w_in, gla_w_gate2, gla_b_gate, gla_o_gain, gla_w_out,
           sgu_w_in, sgu_ln_gain, sgu_ln_bias, sgu_w_spatial, sgu_b_spatial, sgu_w_out):
    b, s, d = x.shape
    t = b * s
    x2 = x.reshape(t, d)

    w_in_t = gla_w_in.reshape(d, -1).T
    w_lr = jnp.pad(w_in_t[GLA_MAIN:], ((0, LANES - GLA_RANK), (0, 0))).astype(BF16)
    w2 = jnp.pad(gla_w_gate2[0], ((0, LANES - GLA_RANK), (0, 0))).astype(BF16)
    ri = jnp.arange(GLA_TRI)
    tri = ((ri[:, None] >= ri[None, :]) & (ri[:, None] // CHUNK == ri[None, :] // CHUNK)).astype(BF16)
    proj, dec, gla_w_out_b = _gla_inproj(x2, norm_pre[0:1], w_in_t, w_lr, w2, gla_b_gate[0:1], tri,
                                         gla_w_out.reshape(GLA_DV, d))
    a, sgu_w_in_b = _gla_scan(proj.reshape(b, s, GLA_MAIN), dec.reshape(b, s // CHUNK, GLA_DK),
                              gla_o_gain[0:1], sgu_w_in.reshape(d, 3 * SGU_WIDTH))
    x2 = _outproj(a.reshape(t, GLA_DV), gla_w_out_b, norm_post[0:1], x2, name="gla_outproj")

    bias_full = jnp.repeat(sgu_b_spatial[0].T, SGU_GD, axis=1)
    z, sgu_w_out_b = _sgu_inproj(x2, norm_pre[1:2], sgu_w_in_b, sgu_w_out.reshape(SGU_WIDTH, d),
                                 sgu_ln_gain[0:1], sgu_ln_bias[0:1], sgu_w_spatial[0], bias_full)
    x2 = _outproj(z, sgu_w_out_b, norm_post[1:2], x2, name="sgu_outproj")
    return x2.reshape(b, s, d)
```

```python
import jax
import jax.numpy as jnp
from jax import lax
from jax.experimental import pallas as pl
from jax.experimental.pallas import tpu as pltpu

F32 = jnp.float32
BF16 = jnp.bfloat16

D_MODEL = 2048
EPS = 1e-6
CHUNK = 64

GLA_HEADS = 4
GLA_DK = D_MODEL // 2
GLA_DV = D_MODEL
GLA_DKH = GLA_DK // GLA_HEADS
GLA_DVH = GLA_DV // GLA_HEADS
GLA_RANK = 16
GLA_INV_TAU = 1.0 / 16.0
GLA_MAIN = 2 * GLA_DK + 2 * GLA_DV

SGU_WIDTH = D_MODEL
SGU_BLOCK = 128
SGU_GROUPS = 8
SGU_GD = SGU_WIDTH // SGU_GROUPS

LANES = 128
VMEM_LIMIT = 56 * 1024 * 1024
GLA_INPROJ_VMEM_LIMIT = 60 * 1024 * 1024

INPROJ_TM = 512
INPROJ_SLAB = 512
OUT_TM = 512
LN_ROWS = 16
GLA_TILE = 512
GLA_TRI = 256
W_CHUNK = 128

GELU_C1 = (2.0 / 3.141592653589793) ** 0.5
GELU_C3 = GELU_C1 * 0.044715


def _rms(x, gain):
    return x * lax.rsqrt(jnp.mean(x * x, axis=-1, keepdims=True) + EPS) * gain


def _gelu(r):
    return (0.5 * r) * (1.0 + jnp.tanh(r * (GELU_C1 + GELU_C3 * (r * r))))


def _silu(r):
    hr = 0.5 * r
    return hr * (1.0 + jnp.tanh(hr))


def _dot_nt(a, b_t):
    return lax.dot_general(a, b_t, (((1,), (1,)), ((), ())), preferred_element_type=F32)


def _cast_specs(w_next, index_map, steps):
    rows, cols = w_next.shape
    spec = pl.BlockSpec((rows // steps, cols), index_map)
    return spec, spec, jax.ShapeDtypeStruct((rows, cols), BF16)


def _gla_inproj_kernel(x_ref, gain_ref, w_hbm, wlr_ref, w2_ref, bg_ref, tri_ref, wn_ref,
                       o_ref, dec_ref, wnb_ref, w_ref, stage_ref, e_ref, sem):
    per_slab = INPROJ_SLAB // W_CHUNK
    q_slabs = GLA_DK // INPROJ_SLAB
    v_slabs = GLA_DV // INPROJ_SLAB
    q_ids = list(range(q_slabs))
    k_ids = list(range(q_slabs, 2 * q_slabs))
    v_ids = list(range(2 * q_slabs, 2 * q_slabs + v_slabs))
    g_ids = list(range(2 * q_slabs + v_slabs, 2 * q_slabs + 2 * v_slabs))
    order = q_ids + v_ids[:-1] + k_ids + v_ids[-1:] + g_ids
    n_chunks = len(order) * per_slab

    def chunk_copy(s):
        row0 = order[s // per_slab] * INPROJ_SLAB + (s % per_slab) * W_CHUNK
        slot = s % per_slab
        return pltpu.make_async_copy(w_hbm.at[pl.ds(row0, W_CHUNK), :],
                                     stage_ref.at[slot], sem.at[slot])

    def body(stream_weight):
        wnb_ref[...] = wn_ref[...].astype(BF16)
        h = _rms(x_ref[...], gain_ref[...]).astype(BF16)
        tm = h.shape[0]

        def proj(p):
            n = order[p]
            if stream_weight:
                for s in range(p * per_slab, (p + 1) * per_slab):
                    row0 = n * INPROJ_SLAB + (s % per_slab) * W_CHUNK
                    chunk_copy(s).wait()
                    w_ref[row0:row0 + W_CHUNK, :] = stage_ref[s % per_slab].astype(BF16)
                    if s + per_slab < n_chunks:
                        chunk_copy(s + per_slab).start()
            return _dot_nt(h, w_ref[n * INPROJ_SLAB:(n + 1) * INPROJ_SLAB, :])

        def store(n, r):
            o_ref[:, n * INPROJ_SLAB:(n + 1) * INPROJ_SLAB] = r.astype(o_ref.dtype)

        def chunk_decays(r, hi, lo):
            rs = slice(r * GLA_TRI, (r + 1) * GLA_TRI)
            tri = tri_ref[...]
            bcum = (jnp.dot(tri, hi[rs], preferred_element_type=F32)
                    + jnp.dot(tri, lo[rs], preferred_element_type=F32))
            for cc in range(GLA_TRI // CHUNK):
                c = r * (GLA_TRI // CHUNK) + cc
                bc = bcum[cc * CHUNK:(cc + 1) * CHUNK]
                b_end = bc[CHUNK - 1:CHUNK, :]
                e_ref[c * CHUNK:(c + 1) * CHUNK, :] = jnp.exp(b_end - bc).astype(BF16)
                dec_ref[c:c + 1, :] = jnp.exp(b_end)

        p = 0
        lr = _dot_nt(h, wlr_ref[...])
        store(order[p], proj(p)); p += 1
        z = jnp.dot(lr.astype(BF16), w2_ref[...], preferred_element_type=F32) + bg_ref[...]
        for _ in range(1, q_slabs):
            store(order[p], proj(p)); p += 1
        la = (jnp.minimum(z, 0.0) - jnp.log(1.0 + jnp.exp(-jnp.abs(z)))) * GLA_INV_TAU
        hi = la.astype(BF16)
        lo = (la - hi.astype(F32)).astype(BF16)
        for r in range(tm // GLA_TRI):
            store(order[p], proj(p)); p += 1
            chunk_decays(r, hi, lo)
        while order[p] not in k_ids:
            store(order[p], proj(p)); p += 1
        for j in range(q_slabs):
            kc = slice(j * INPROJ_SLAB, (j + 1) * INPROJ_SLAB)
            store(order[p], proj(p) * e_ref[:, kc].astype(F32)); p += 1
        while p < len(order):
            r = proj(p)
            store(order[p], _silu(r) if order[p] in g_ids else r); p += 1

    @pl.when(pl.program_id(0) == 0)
    def _():
        for s in range(per_slab):
            chunk_copy(s).start()
        body(True)

    @pl.when(pl.program_id(0) > 0)
    def _():
        body(False)


def _gla_inproj(x2, gain, w_t, wlr_t, w2, b_gate, tri, w_next):
    t, d = x2.shape
    n = GLA_MAIN
    tm = INPROJ_TM
    steps = t // tm
    resident = pl.Buffered(1)
    const2 = lambda i: (0, 0)
    wn_in, wn_out, wn_shape = _cast_specs(w_next, lambda i: (i, 0), steps)
    return pl.pallas_call(
        _gla_inproj_kernel,
        grid=(steps,),
        in_specs=[
            pl.BlockSpec((tm, d), lambda i: (i, 0)),
            pl.BlockSpec((1, d), const2),
            pl.BlockSpec(memory_space=pl.ANY),
            pl.BlockSpec((LANES, d), const2, pipeline_mode=resident),
            pl.BlockSpec((LANES, GLA_DK), const2, pipeline_mode=resident),
            pl.BlockSpec((1, GLA_DK), const2),
            pl.BlockSpec((GLA_TRI, GLA_TRI), const2, pipeline_mode=resident),
            wn_in,
        ],
        out_specs=[pl.BlockSpec((tm, n), lambda i: (i, 0)),
                   pl.BlockSpec((tm // CHUNK, GLA_DK), lambda i: (i, 0)), wn_out],
        out_shape=[jax.ShapeDtypeStruct((t, n), BF16),
                   jax.ShapeDtypeStruct((t // CHUNK, GLA_DK), F32), wn_shape],
        scratch_shapes=[
            pltpu.VMEM((n, d), BF16),
            pltpu.VMEM((INPROJ_SLAB // W_CHUNK, W_CHUNK, d), F32),
            pltpu.VMEM((tm, GLA_DK), BF16),
            pltpu.SemaphoreType.DMA((INPROJ_SLAB // W_CHUNK,)),
        ],
        compiler_params=pltpu.CompilerParams(
            dimension_semantics=("arbitrary",), vmem_limit_bytes=GLA_INPROJ_VMEM_LIMIT),
        name="gla_inproj",
    )(x2, gain, w_t, wlr_t, w2, b_gate, tri, w_next)


def _sgu_inproj_kernel(x_ref, gain_ref, w_ref, wn_ref, lng_ref, lnb_ref, ws_ref, bs_ref,
                       z_ref, wnb_ref, wsm_ref, va_ref, vn_ref, tg_ref):
    @pl.when(pl.program_id(0) == 0)
    def _():
        ri = lax.broadcasted_iota(jnp.int32, (SGU_BLOCK, SGU_BLOCK), 0) // CHUNK
        ci = lax.broadcasted_iota(jnp.int32, (SGU_BLOCK, SGU_BLOCK), 1) // CHUNK
        for gi in range(SGU_GROUPS):
            wsm_ref[gi] = jnp.where(ri >= ci, ws_ref[gi], 0.0).astype(BF16)

    wnb_ref[...] = wn_ref[...].astype(BF16)
    h = _rms(x_ref[...], gain_ref[...]).astype(BF16)
    tm = h.shape[0]
    slabs = SGU_WIDTH // INPROJ_SLAB

    def proj(n):
        return jnp.dot(h, w_ref[n], preferred_element_type=F32)

    for n in range(slabs):
        va_ref[:, n * INPROJ_SLAB:(n + 1) * INPROJ_SLAB] = _gelu(proj(slabs + n)).astype(BF16)
    for r in range(tm // LN_ROWS):
        rows = slice(r * LN_ROWS, (r + 1) * LN_ROWS)
        v = va_ref[rows, :].astype(F32)
        vc = v - jnp.mean(v, axis=-1, keepdims=True)
        var = jnp.mean(vc * vc, axis=-1, keepdims=True)
        vn_ref[rows, :] = (vc * lax.rsqrt(var + EPS) * lng_ref[...] + lnb_ref[...]).astype(BF16)
    for n in range(slabs):
        tg_ref[:, n * INPROJ_SLAB:(n + 1) * INPROJ_SLAB] = _silu(proj(2 * slabs + n)).astype(BF16)
    for nb in range(tm // SGU_BLOCK):
        rows = slice(nb * SGU_BLOCK, (nb + 1) * SGU_BLOCK)
        for gi in range(SGU_GROUPS):
            cols = slice(gi * SGU_GD, (gi + 1) * SGU_GD)
            vs = (jnp.dot(wsm_ref[gi], vn_ref[rows, cols], preferred_element_type=F32)
                  + bs_ref[:, cols])
            tg_ref[rows, cols] = (vs * tg_ref[rows, cols].astype(F32)).astype(BF16)
    for n in range(slabs):
        cols = slice(n * INPROJ_SLAB, (n + 1) * INPROJ_SLAB)
        z_ref[:, cols] = (_gelu(proj(n)) * tg_ref[:, cols].astype(F32)).astype(BF16)


def _sgu_inproj(x2, gain, w, w_next, ln_gain, ln_bias, w_spatial, bias_full):
    t, d = x2.shape
    tm = INPROJ_TM
    steps = t // tm
    resident = pl.Buffered(1)
    const2 = lambda i: (0, 0)
    wn_in, wn_out, wn_shape = _cast_specs(w_next, lambda i: (i, 0), steps)
    return pl.pallas_call(
        _sgu_inproj_kernel,
        grid=(steps,),
        in_specs=[
            pl.BlockSpec((tm, d), lambda i: (i, 0)),
            pl.BlockSpec((1, d), const2),
            pl.BlockSpec(w.shape, lambda i: (0, 0, 0), pipeline_mode=resident),
            wn_in,
            pl.BlockSpec((1, SGU_WIDTH), const2),
            pl.BlockSpec((1, SGU_WIDTH), const2),
            pl.BlockSpec((SGU_GROUPS, SGU_BLOCK, SGU_BLOCK), lambda i: (0, 0, 0),
                         pipeline_mode=resident),
            pl.BlockSpec((SGU_BLOCK, SGU_WIDTH), const2, pipeline_mode=resident),
        ],
        out_specs=[pl.BlockSpec((tm, SGU_WIDTH), lambda i: (i, 0)), wn_out],
        out_shape=[jax.ShapeDtypeStruct((t, SGU_WIDTH), BF16), wn_shape],
        scratch_shapes=[
            pltpu.VMEM((SGU_GROUPS, SGU_BLOCK, SGU_BLOCK), BF16),
            pltpu.VMEM((tm, SGU_WIDTH), BF16),
            pltpu.VMEM((tm, SGU_WIDTH), BF16),
            pltpu.VMEM((tm, SGU_WIDTH), BF16),
        ],
        compiler_params=pltpu.CompilerParams(
            dimension_semantics=("arbitrary",), vmem_limit_bytes=VMEM_LIMIT),
        name="sgu_inproj_mix",
    )(x2, gain, w, w_next, ln_gain, ln_bias, w_spatial, bias_full)


def _gla_kernel(q_ref, k_ref, v_ref, g_ref, dec_ref, og_ref, wn_ref, a_ref, wnb_ref,
                s_ref, sb_ref):
    @pl.when(pl.program_id(1) == 0)
    def _():
        s_ref[...] = jnp.zeros_like(s_ref)

    for n in range(wnb_ref.shape[0]):
        wnb_ref[n] = wn_ref[:, n * INPROJ_SLAB:(n + 1) * INPROJ_SLAB].astype(BF16)
    n_chunks = q_ref.shape[1] // CHUNK

    def chunk(c, carry):
        r0 = pl.multiple_of(c * CHUNK, CHUNK)
        rows = pl.ds(r0, CHUNK)
        dec = dec_ref[0, pl.ds(c, 1), :]
        for h in range(GLA_HEADS):
            kc = slice(h * GLA_DKH, (h + 1) * GLA_DKH)
            vc = slice(h * GLA_DVH, (h + 1) * GLA_DVH)
            upd = lax.dot_general(k_ref[0, rows, kc], v_ref[0, rows, vc], (((0,), (0,)), ((), ())),
                                  preferred_element_type=F32)
            decay_col = jnp.transpose(jnp.broadcast_to(dec[:, kc], (LANES, GLA_DKH)))
            for t in range(GLA_DVH // LANES):
                cols = slice(t * LANES, (t + 1) * LANES)
                s_new = s_ref[h, :, cols] * decay_col + upd[:, cols]
                s_ref[h, :, cols] = s_new
                sb_ref[h, :, cols] = s_new.astype(BF16)
        for h in range(GLA_HEADS):
            kc = slice(h * GLA_DKH, (h + 1) * GLA_DKH)
            vc = slice(h * GLA_DVH, (h + 1) * GLA_DVH)
            o = jnp.dot(q_ref[0, rows, kc], sb_ref[h],
                        preferred_element_type=F32) * (GLA_DKH ** -0.5)
            o = _rms(o, og_ref[:, vc])
            a_ref[0, rows, vc] = (o * g_ref[0, rows, vc].astype(F32)).astype(a_ref.dtype)
        return carry

    lax.fori_loop(0, n_chunks, chunk, 0, unroll=2)


def _gla_scan(proj3, dec3, o_gain, w_next):
    b, s, _ = proj3.shape
    tile = GLA_TILE
    tiles = s // tile
    w_rows, w_cols = w_next.shape
    wn_rows = w_rows // (b * tiles)
    wn_in = pl.BlockSpec((wn_rows, w_cols), lambda i, t: (i * tiles + t, 0))
    wn_out = pl.BlockSpec((w_cols // INPROJ_SLAB, wn_rows, INPROJ_SLAB),
                          lambda i, t: (0, i * tiles + t, 0))
    wn_shape = jax.ShapeDtypeStruct((w_cols // INPROJ_SLAB, w_rows, INPROJ_SLAB), BF16)
    in_specs = [
        pl.BlockSpec((1, tile, GLA_DK), lambda i, t: (i, t, 0)),
        pl.BlockSpec((1, tile, GLA_DK), lambda i, t: (i, t, 1)),
        pl.BlockSpec((1, tile, GLA_DV), lambda i, t: (i, t, 1)),
        pl.BlockSpec((1, tile, GLA_DV), lambda i, t: (i, t, 2)),
        pl.BlockSpec((1, tile // CHUNK, GLA_DK), lambda i, t: (i, t, 0)),
        pl.BlockSpec((1, GLA_DV), lambda i, t: (0, 0)),
        wn_in,
    ]
    return pl.pallas_call(
        _gla_kernel,
        grid=(b, tiles),
        in_specs=in_specs,
        out_specs=[pl.BlockSpec((1, tile, GLA_DV), lambda i, t: (i, t, 0)), wn_out],
        out_shape=[jax.ShapeDtypeStruct((b, s, GLA_DV), BF16), wn_shape],
        scratch_shapes=[
            pltpu.VMEM((GLA_HEADS, GLA_DKH, GLA_DVH), F32),
            pltpu.VMEM((GLA_HEADS, GLA_DKH, GLA_DVH), BF16),
        ],
        compiler_params=pltpu.CompilerParams(
            dimension_semantics=("arbitrary", "arbitrary"), vmem_limit_bytes=VMEM_LIMIT),
        name="gla_scan",
    )(proj3, proj3, proj3, proj3, dec3, o_gain, w_next)


def _outproj_kernel(a_ref, w_ref, gain_ref, x_ref, o_ref):
    y = jnp.dot(a_ref[...], w_ref[...], preferred_element_type=F32)
    o_ref[...] = x_ref[...] + _rms(y, gain_ref[...])


def _outproj(a2, w, gain, x2, name):
    t, d = x2.shape
    k = a2.shape[1]
    tm = OUT_TM
    return pl.pallas_call(
        _outproj_kernel,
        grid=(t // tm,),
        in_specs=[
            pl.BlockSpec((tm, k), lambda i: (i, 0)),
            pl.BlockSpec((k, d), lambda i: (0, 0), pipeline_mode=pl.Buffered(1)),
            pl.BlockSpec((1, d), lambda i: (0, 0)),
            pl.BlockSpec((tm, d), lambda i: (i, 0)),
        ],
        out_specs=pl.BlockSpec((tm, d), lambda i: (i, 0)),
        out_shape=jax.ShapeDtypeStruct((t, d), F32),
        compiler_params=pltpu.CompilerParams(
            dimension_semantics=("arbitrary",), vmem_limit_bytes=VMEM_LIMIT),
        name=name,
    )(a2, w, gain, x2)


def kernel(x, norm_pre, norm_post, gla_w_in, gla_w_gate2, gla_b_gate, gla_o_gain, gla_w_out,
           sgu_w_in, sgu_ln_gain, sgu_ln_bias, sgu_w_spatial, sgu_b_spatial, sgu_w_out):
    b, s, d = x.shape
    t = b * s
    x2 = x.reshape(t, d)

    w_in_t = gla_w_in.reshape(d, -1).T
    w_lr = jnp.pad(w_in_t[GLA_MAIN:], ((0, LANES - GLA_RANK), (0, 0))).astype(BF16)
    w2 = jnp.pad(gla_w_gate2[0], ((0, LANES - GLA_RANK), (0, 0))).astype(BF16)
    ri = jnp.arange(GLA_TRI)
    tri = ((ri[:, None] >= ri[None, :]) & (ri[:, None] // CHUNK == ri[None, :] // CHUNK)).astype(BF16)
    proj, dec, gla_w_out_b = _gla_inproj(x2, norm_pre[0:1], w_in_t, w_lr, w2, gla_b_gate[0:1], tri,
                                         gla_w_out.reshape(GLA_DV, d))
    a, sgu_w_in_b = _gla_scan(proj.reshape(b, s, GLA_MAIN), dec.reshape(b, s // CHUNK, GLA_DK),
                              gla_o_gain[0:1], sgu_w_in.reshape(d, 3 * SGU_WIDTH))
    x2 = _outproj(a.reshape(t, GLA_DV), gla_w_out_b, norm_post[0:1], x2, name="gla_outproj")

    bias_full = jnp.repeat(sgu_b_spatial[0].T, SGU_GD, axis=1)
    z, sgu_w_out_b = _sgu_inproj(x2, norm_pre[1:2], sgu_w_in_b, sgu_w_out.reshape(SGU_WIDTH, d),
                                 sgu_ln_gain[0:1], sgu_ln_bias[0:1], sgu_w_spatial[0], bias_full)
    x2 = _outproj(z, sgu_w_out_b, norm_post[1:2], x2, name="sgu_outproj")
    return x2.reshape(b, s, d)
```

```python
import jax
import jax.numpy as jnp
from jax import lax
from jax.experimental import pallas as pl
from jax.experimental.pallas import tpu as pltpu

F32 = jnp.float32
BF16 = jnp.bfloat16

D_MODEL = 2048
EPS = 1e-6
CHUNK = 64

GLA_HEADS = 4
GLA_DK = D_MODEL // 2
GLA_DV = D_MODEL
GLA_DKH = GLA_DK // GLA_HEADS
GLA_DVH = GLA_DV // GLA_HEADS
GLA_RANK = 16
GLA_INV_TAU = 1.0 / 16.0
GLA_MAIN = 2 * GLA_DK + 2 * GLA_DV

SGU_WIDTH = D_MODEL
SGU_BLOCK = 128
SGU_GROUPS = 8
SGU_GD = SGU_WIDTH // SGU_GROUPS

LANES = 128
VMEM_LIMIT = 56 * 1024 * 1024
VMEM_LIMIT_LARGE = 60 * 1024 * 1024

INPROJ_TM = 512
INPROJ_SLAB = 512
OUT_TM = 512
LN_ROWS = 16
GLA_TILE = 512
GLA_TRI = 256
W_CHUNK = 128

GELU_C1 = (2.0 / 3.141592653589793) ** 0.5
GELU_C3 = GELU_C1 * 0.044715


def _rms(x, gain):
    return x * lax.rsqrt(jnp.mean(x * x, axis=-1, keepdims=True) + EPS) * gain


def _gelu(r):
    return (0.5 * r) * (1.0 + jnp.tanh(r * (GELU_C1 + GELU_C3 * (r * r))))


def _silu(r):
    hr = 0.5 * r
    return hr * (1.0 + jnp.tanh(hr))


def _dot_nt(a, b_t):
    return lax.dot_general(a, b_t, (((1,), (1,)), ((), ())), preferred_element_type=F32)


def _cast_specs(w_next, index_map, steps):
    rows, cols = w_next.shape
    spec = pl.BlockSpec((rows // steps, cols), index_map)
    return spec, spec, jax.ShapeDtypeStruct((rows, cols), BF16)


def _gla_inproj_kernel(x_ref, gain_ref, w_hbm, wlr_ref, w2_ref, bg_ref, tri_ref, wn_ref,
                       o_ref, dec_ref, wnb_ref, w_ref, stage_ref, e_ref, sem):
    per_slab = INPROJ_SLAB // W_CHUNK
    q_slabs = GLA_DK // INPROJ_SLAB
    v_slabs = GLA_DV // INPROJ_SLAB
    q_ids = list(range(q_slabs))
    k_ids = list(range(q_slabs, 2 * q_slabs))
    v_ids = list(range(2 * q_slabs, 2 * q_slabs + v_slabs))
    g_ids = list(range(2 * q_slabs + v_slabs, 2 * q_slabs + 2 * v_slabs))
    order = q_ids + v_ids[:-1] + k_ids + v_ids[-1:] + g_ids
    n_chunks = len(order) * per_slab

    def chunk_copy(s):
        row0 = order[s // per_slab] * INPROJ_SLAB + (s % per_slab) * W_CHUNK
        slot = s % per_slab
        return pltpu.make_async_copy(w_hbm.at[pl.ds(row0, W_CHUNK), :],
                                     stage_ref.at[slot], sem.at[slot])

    def body(stream_weight):
        wnb_ref[...] = wn_ref[...].astype(BF16)
        h = _rms(x_ref[...], gain_ref[...]).astype(BF16)
        tm = h.shape[0]

        def proj(p):
            n = order[p]
            if stream_weight:
                for s in range(p * per_slab, (p + 1) * per_slab):
                    row0 = n * INPROJ_SLAB + (s % per_slab) * W_CHUNK
                    chunk_copy(s).wait()
                    w_ref[row0:row0 + W_CHUNK, :] = stage_ref[s % per_slab].astype(BF16)
                    if s + per_slab < n_chunks:
                        chunk_copy(s + per_slab).start()
            return _dot_nt(h, w_ref[n * INPROJ_SLAB:(n + 1) * INPROJ_SLAB, :])

        def store(n, r):
            o_ref[:, n * INPROJ_SLAB:(n + 1) * INPROJ_SLAB] = r.astype(o_ref.dtype)

        def chunk_decays(r, hi, lo):
            rs = slice(r * GLA_TRI, (r + 1) * GLA_TRI)
            tri = tri_ref[...]
            bcum = (jnp.dot(tri, hi[rs], preferred_element_type=F32)
                    + jnp.dot(tri, lo[rs], preferred_element_type=F32))
            for cc in range(GLA_TRI // CHUNK):
                c = r * (GLA_TRI // CHUNK) + cc
                bc = bcum[cc * CHUNK:(cc + 1) * CHUNK]
                b_end = bc[CHUNK - 1:CHUNK, :]
                e_ref[c * CHUNK:(c + 1) * CHUNK, :] = jnp.exp(b_end - bc).astype(BF16)
                dec_ref[c:c + 1, :] = jnp.exp(b_end)

        p = 0
        lr = _dot_nt(h, wlr_ref[...])
        store(order[p], proj(p)); p += 1
        z = jnp.dot(lr.astype(BF16), w2_ref[...], preferred_element_type=F32) + bg_ref[...]
        for _ in range(1, q_slabs):
            store(order[p], proj(p)); p += 1
        la = (jnp.minimum(z, 0.0) - jnp.log(1.0 + jnp.exp(-jnp.abs(z)))) * GLA_INV_TAU
        hi = la.astype(BF16)
        lo = (la - hi.astype(F32)).astype(BF16)
        for r in range(tm // GLA_TRI):
            store(order[p], proj(p)); p += 1
            chunk_decays(r, hi, lo)
        while order[p] not in k_ids:
            store(order[p], proj(p)); p += 1
        for j in range(q_slabs):
            kc = slice(j * INPROJ_SLAB, (j + 1) * INPROJ_SLAB)
            store(order[p], proj(p) * e_ref[:, kc].astype(F32)); p += 1
        while p < len(order):
            r = proj(p)
            store(order[p], _silu(r) if order[p] in g_ids else r); p += 1

    @pl.when(pl.program_id(0) == 0)
    def _():
        for s in range(per_slab):
            chunk_copy(s).start()
        body(True)

    @pl.when(pl.program_id(0) > 0)
    def _():
        body(False)


def _gla_inproj(x2, gain, w_t, wlr_t, w2, b_gate, tri, w_next):
    t, d = x2.shape
    n = GLA_MAIN
    tm = INPROJ_TM
    steps = t // tm
    resident = pl.Buffered(1)
    const2 = lambda i: (0, 0)
    wn_in, wn_out, wn_shape = _cast_specs(w_next, lambda i: (i, 0), steps)
    return pl.pallas_call(
        _gla_inproj_kernel,
        grid=(steps,),
        in_specs=[
            pl.BlockSpec((tm, d), lambda i: (i, 0)),
            pl.BlockSpec((1, d), const2),
            pl.BlockSpec(memory_space=pl.ANY),
            pl.BlockSpec((LANES, d), const2, pipeline_mode=resident),
            pl.BlockSpec((LANES, GLA_DK), const2, pipeline_mode=resident),
            pl.BlockSpec((1, GLA_DK), const2),
            pl.BlockSpec((GLA_TRI, GLA_TRI), const2, pipeline_mode=resident),
            wn_in,
        ],
        out_specs=[pl.BlockSpec((tm, n), lambda i: (i, 0)),
                   pl.BlockSpec((tm // CHUNK, GLA_DK), lambda i: (i, 0)), wn_out],
        out_shape=[jax.ShapeDtypeStruct((t, n), BF16),
                   jax.ShapeDtypeStruct((t // CHUNK, GLA_DK), F32), wn_shape],
        scratch_shapes=[
            pltpu.VMEM((n, d), BF16),
            pltpu.VMEM((INPROJ_SLAB // W_CHUNK, W_CHUNK, d), F32),
            pltpu.VMEM((tm, GLA_DK), BF16),
            pltpu.SemaphoreType.DMA((INPROJ_SLAB // W_CHUNK,)),
        ],
        compiler_params=pltpu.CompilerParams(
            dimension_semantics=("arbitrary",), vmem_limit_bytes=VMEM_LIMIT_LARGE),
        name="gla_inproj",
    )(x2, gain, w_t, wlr_t, w2, b_gate, tri, w_next)


def _sgu_inproj_kernel(x_ref, gain_ref, w_ref, wn_ref, lng_ref, lnb_ref, ws_ref, bs_ref,
                       z_ref, wnb_ref, wsm_ref, va_ref, vn_ref, tg_ref):
    @pl.when(pl.program_id(0) == 0)
    def _():
        ri = lax.broadcasted_iota(jnp.int32, (SGU_BLOCK, SGU_BLOCK), 0) // CHUNK
        ci = lax.broadcasted_iota(jnp.int32, (SGU_BLOCK, SGU_BLOCK), 1) // CHUNK
        for gi in range(SGU_GROUPS):
            wsm_ref[gi] = jnp.where(ri >= ci, ws_ref[gi], 0.0).astype(BF16)

    wnb_ref[...] = wn_ref[...].astype(BF16)
    h = _rms(x_ref[...], gain_ref[...]).astype(BF16)
    tm = h.shape[0]
    slabs = SGU_WIDTH // INPROJ_SLAB

    def proj(n):
        return jnp.dot(h, w_ref[n], preferred_element_type=F32)

    for n in range(slabs):
        va_ref[:, n * INPROJ_SLAB:(n + 1) * INPROJ_SLAB] = _gelu(proj(slabs + n)).astype(BF16)
    for r in range(tm // LN_ROWS):
        rows = slice(r * LN_ROWS, (r + 1) * LN_ROWS)
        v = va_ref[rows, :].astype(F32)
        vc = v - jnp.mean(v, axis=-1, keepdims=True)
        var = jnp.mean(vc * vc, axis=-1, keepdims=True)
        vn_ref[rows, :] = (vc * lax.rsqrt(var + EPS) * lng_ref[...] + lnb_ref[...]).astype(BF16)
    for n in range(slabs):
        tg_ref[:, n * INPROJ_SLAB:(n + 1) * INPROJ_SLAB] = _silu(proj(2 * slabs + n)).astype(BF16)
    for nb in range(tm // SGU_BLOCK):
        rows = slice(nb * SGU_BLOCK, (nb + 1) * SGU_BLOCK)
        for gi in range(SGU_GROUPS):
            cols = slice(gi * SGU_GD, (gi + 1) * SGU_GD)
            vs = (jnp.dot(wsm_ref[gi], vn_ref[rows, cols], preferred_element_type=F32)
                  + bs_ref[:, cols])
            tg_ref[rows, cols] = (vs * tg_ref[rows, cols].astype(F32)).astype(BF16)
    for n in range(slabs):
        cols = slice(n * INPROJ_SLAB, (n + 1) * INPROJ_SLAB)
        z_ref[:, cols] = (_gelu(proj(n)) * tg_ref[:, cols].astype(F32)).astype(BF16)


def _sgu_inproj(x2, gain, w, w_next, ln_gain, ln_bias, w_spatial, bias_full):
    t, d = x2.shape
    tm = INPROJ_TM
    steps = t // tm
    resident = pl.Buffered(1)
    const2 = lambda i: (0, 0)
    wn_in, wn_out, wn_shape = _cast_specs(w_next, lambda i: (i, 0), steps)
    return pl.pallas_call(
        _sgu_inproj_kernel,
        grid=(steps,),
        in_specs=[
            pl.BlockSpec((tm, d), lambda i: (i, 0)),
            pl.BlockSpec((1, d), const2),
            pl.BlockSpec(w.shape, lambda i: (0, 0, 0), pipeline_mode=resident),
            wn_in,
            pl.BlockSpec((1, SGU_WIDTH), const2),
            pl.BlockSpec((1, SGU_WIDTH), const2),
            pl.BlockSpec((SGU_GROUPS, SGU_BLOCK, SGU_BLOCK), lambda i: (0, 0, 0),
                         pipeline_mode=resident),
            pl.BlockSpec((SGU_BLOCK, SGU_WIDTH), const2, pipeline_mode=resident),
        ],
        out_specs=[pl.BlockSpec((tm, SGU_WIDTH), lambda i: (i, 0)), wn_out],
        out_shape=[jax.ShapeDtypeStruct((t, SGU_WIDTH), BF16), wn_shape],
        scratch_shapes=[
            pltpu.VMEM((SGU_GROUPS, SGU_BLOCK, SGU_BLOCK), BF16),
            pltpu.VMEM((tm, SGU_WIDTH), BF16),
            pltpu.VMEM((tm, SGU_WIDTH), BF16),
            pltpu.VMEM((tm, SGU_WIDTH), BF16),
        ],
        compiler_params=pltpu.CompilerParams(
            dimension_semantics=("arbitrary",), vmem_limit_bytes=VMEM_LIMIT),
        name="sgu_inproj_mix",
    )(x2, gain, w, w_next, ln_gain, ln_bias, w_spatial, bias_full)


def _gla_kernel(q_ref, k_ref, v_ref, g_ref, dec_ref, og_ref, wo_ref, gain_ref, x_ref, wn_ref,
                o_ref, wnb_ref, s_ref, sb_ref, a_ref):
    @pl.when(pl.program_id(1) == 0)
    def _():
        s_ref[...] = jnp.zeros_like(s_ref)

    for n in range(wnb_ref.shape[0]):
        wnb_ref[n] = wn_ref[:, n * INPROJ_SLAB:(n + 1) * INPROJ_SLAB].astype(BF16)
    n_chunks = q_ref.shape[1] // CHUNK

    def chunk(c, carry):
        r0 = pl.multiple_of(c * CHUNK, CHUNK)
        rows = pl.ds(r0, CHUNK)
        dec = dec_ref[0, pl.ds(c, 1), :]
        for h in range(GLA_HEADS):
            kc = slice(h * GLA_DKH, (h + 1) * GLA_DKH)
            vc = slice(h * GLA_DVH, (h + 1) * GLA_DVH)
            upd = lax.dot_general(k_ref[0, rows, kc], v_ref[0, rows, vc], (((0,), (0,)), ((), ())),
                                  preferred_element_type=F32)
            decay_col = jnp.transpose(jnp.broadcast_to(dec[:, kc], (LANES, GLA_DKH)))
            for t in range(GLA_DVH // LANES):
                cols = slice(t * LANES, (t + 1) * LANES)
                s_new = s_ref[h, :, cols] * decay_col + upd[:, cols]
                s_ref[h, :, cols] = s_new
                sb_ref[h, :, cols] = s_new.astype(BF16)
        for h in range(GLA_HEADS):
            kc = slice(h * GLA_DKH, (h + 1) * GLA_DKH)
            vc = slice(h * GLA_DVH, (h + 1) * GLA_DVH)
            o = jnp.dot(q_ref[0, rows, kc], sb_ref[h],
                        preferred_element_type=F32) * (GLA_DKH ** -0.5)
            o = _rms(o, og_ref[:, vc])
            a_ref[rows, vc] = (o * g_ref[0, rows, vc].astype(F32)).astype(a_ref.dtype)
        return carry

    lax.fori_loop(0, n_chunks, chunk, 0, unroll=2)

    y = jnp.dot(a_ref[...], wo_ref[...], preferred_element_type=F32)
    o_ref[0] = x_ref[0] + _rms(y, gain_ref[...])


def _gla_scan_outproj(proj3, dec3, o_gain, w_out, gain, x3, w_next):
    b, s, d = x3.shape
    tile = GLA_TILE
    tiles = s // tile
    const2 = lambda i, t: (0, 0)
    w_rows, w_cols = w_next.shape
    wn_rows = w_rows // (b * tiles)
    wn_in = pl.BlockSpec((wn_rows, w_cols), lambda i, t: (i * tiles + t, 0))
    wn_out = pl.BlockSpec((w_cols // INPROJ_SLAB, wn_rows, INPROJ_SLAB),
                          lambda i, t: (0, i * tiles + t, 0))
    wn_shape = jax.ShapeDtypeStruct((w_cols // INPROJ_SLAB, w_rows, INPROJ_SLAB), BF16)
    in_specs = [
        pl.BlockSpec((1, tile, GLA_DK), lambda i, t: (i, t, 0)),
        pl.BlockSpec((1, tile, GLA_DK), lambda i, t: (i, t, 1)),
        pl.BlockSpec((1, tile, GLA_DV), lambda i, t: (i, t, 1)),
        pl.BlockSpec((1, tile, GLA_DV), lambda i, t: (i, t, 2)),
        pl.BlockSpec((1, tile // CHUNK, GLA_DK), lambda i, t: (i, t, 0)),
        pl.BlockSpec((1, GLA_DV), const2),
        pl.BlockSpec((GLA_DV, d), const2, pipeline_mode=pl.Buffered(1)),
        pl.BlockSpec((1, d), const2),
        pl.BlockSpec((1, tile, d), lambda i, t: (i, t, 0)),
        wn_in,
    ]
    return pl.pallas_call(
        _gla_kernel,
        grid=(b, tiles),
        in_specs=in_specs,
        out_specs=[pl.BlockSpec((1, tile, d), lambda i, t: (i, t, 0)), wn_out],
        out_shape=[jax.ShapeDtypeStruct((b, s, d), F32), wn_shape],
        scratch_shapes=[
            pltpu.VMEM((GLA_HEADS, GLA_DKH, GLA_DVH), F32),
            pltpu.VMEM((GLA_HEADS, GLA_DKH, GLA_DVH), BF16),
            pltpu.VMEM((tile, GLA_DV), BF16),
        ],
        compiler_params=pltpu.CompilerParams(
            dimension_semantics=("arbitrary", "arbitrary"), vmem_limit_bytes=VMEM_LIMIT_LARGE),
        name="gla_scan_outproj",
    )(proj3, proj3, proj3, proj3, dec3, o_gain, w_out, gain, x3, w_next)


def _outproj_kernel(a_ref, w_ref, gain_ref, x_ref, o_ref):
    y = jnp.dot(a_ref[...], w_ref[...], preferred_element_type=F32)
    o_ref[...] = x_ref[...] + _rms(y, gain_ref[...])


def _outproj(a2, w, gain, x2, name):
    t, d = x2.shape
    k = a2.shape[1]
    tm = OUT_TM
    return pl.pallas_call(
        _outproj_kernel,
        grid=(t // tm,),
        in_specs=[
            pl.BlockSpec((tm, k), lambda i: (i, 0)),
            pl.BlockSpec((k, d), lambda i: (0, 0), pipeline_mode=pl.Buffered(1)),
            pl.BlockSpec((1, d), lambda i: (0, 0)),
            pl.BlockSpec((tm, d), lambda i: (i, 0)),
        ],
        out_specs=pl.BlockSpec((tm, d), lambda i: (i, 0)),
        out_shape=jax.ShapeDtypeStruct((t, d), F32),
        compiler_params=pltpu.CompilerParams(
            dimension_semantics=("arbitrary",), vmem_limit_bytes=VMEM_LIMIT),
        name=name,
    )(a2, w, gain, x2)


def kernel(x, norm_pre, norm_post, gla_w_in, gla_w_gate2, gla_b_gate, gla_o_gain, gla_w_out,
           sgu_w_in, sgu_ln_gain, sgu_ln_bias, sgu_w_spatial, sgu_b_spatial, sgu_w_out):
    b, s, d = x.shape
    t = b * s
    x2 = x.reshape(t, d)

    w_in_t = gla_w_in.reshape(d, -1).T
    w_lr = jnp.pad(w_in_t[GLA_MAIN:], ((0, LANES - GLA_RANK), (0, 0))).astype(BF16)
    w2 = jnp.pad(gla_w_gate2[0], ((0, LANES - GLA_RANK), (0, 0))).astype(BF16)
    ri = jnp.arange(GLA_TRI)
    tri = ((ri[:, None] >= ri[None, :]) & (ri[:, None] // CHUNK == ri[None, :] // CHUNK)).astype(BF16)
    proj, dec, gla_w_out_b = _gla_inproj(x2, norm_pre[0:1], w_in_t, w_lr, w2, gla_b_gate[0:1], tri,
                                         gla_w_out.reshape(GLA_DV, d))
    x3, sgu_w_in_b = _gla_scan_outproj(
        proj.reshape(b, s, GLA_MAIN), dec.reshape(b, s // CHUNK, GLA_DK), gla_o_gain[0:1],
        gla_w_out_b, norm_post[0:1], x, sgu_w_in.reshape(d, 3 * SGU_WIDTH))
    x2 = x3.reshape(t, d)

    bias_full = jnp.repeat(sgu_b_spatial[0].T, SGU_GD, axis=1)
    z, sgu_w_out_b = _sgu_inproj(x2, norm_pre[1:2], sgu_w_in_b, sgu_w_out.reshape(SGU_WIDTH, d),
                                 sgu_ln_gain[0:1], sgu_ln_bias[0:1], sgu_w_spatial[0], bias_full)
    x2 = _outproj(z, sgu_w_out_b, norm_post[1:2], x2, name="sgu_outproj")
    return x2.reshape(b, s, d)
```

```python
import jax
import jax.numpy as jnp
from jax import lax
from jax.experimental import pallas as pl
from jax.experimental.pallas import tpu as pltpu

F32 = jnp.float32
BF16 = jnp.bfloat16

D_MODEL = 2048
EPS = 1e-6
CHUNK = 64

GLA_HEADS = 4
GLA_DK = D_MODEL // 2
GLA_DV = D_MODEL
GLA_DKH = GLA_DK // GLA_HEADS
GLA_DVH = GLA_DV // GLA_HEADS
GLA_RANK = 16
GLA_INV_TAU = 1.0 / 16.0
GLA_MAIN = 2 * GLA_DK + 2 * GLA_DV

SGU_WIDTH = D_MODEL
SGU_BLOCK = 128
SGU_GROUPS = 8
SGU_GD = SGU_WIDTH // SGU_GROUPS

LANES = 128
VMEM_LIMIT = 56 * 1024 * 1024
VMEM_LIMIT_LARGE = 60 * 1024 * 1024

INPROJ_TM = 512
INPROJ_SLAB = 512
OUT_TM = 512
LN_ROWS = 16
GLA_TILE = 512
GLA_TRI = 256
W_CHUNK = 128

GELU_C1 = (2.0 / 3.141592653589793) ** 0.5
GELU_C3 = GELU_C1 * 0.044715


def _rms(x, gain):
    return x * lax.rsqrt(jnp.mean(x * x, axis=-1, keepdims=True) + EPS) * gain


def _gelu(r):
    return (0.5 * r) * (1.0 + jnp.tanh(r * (GELU_C1 + GELU_C3 * (r * r))))


def _silu(r):
    hr = 0.5 * r
    return hr * (1.0 + jnp.tanh(hr))


def _dot_nt(a, b_t):
    return lax.dot_general(a, b_t, (((1,), (1,)), ((), ())), preferred_element_type=F32)


def _cast_specs(w_next, index_map, steps):
    rows, cols = w_next.shape
    spec = pl.BlockSpec((rows // steps, cols), index_map)
    return spec, spec, jax.ShapeDtypeStruct((rows, cols), BF16)


def _gla_inproj_kernel(x_ref, gain_ref, w_hbm, wlr_ref, w2_ref, bg_ref, tri_ref, wn_ref,
                       o_ref, dec_ref, wnb_ref, w_ref, stage_ref, e_ref, sem):
    per_slab = INPROJ_SLAB // W_CHUNK
    q_slabs = GLA_DK // INPROJ_SLAB
    v_slabs = GLA_DV // INPROJ_SLAB
    q_ids = list(range(q_slabs))
    k_ids = list(range(q_slabs, 2 * q_slabs))
    v_ids = list(range(2 * q_slabs, 2 * q_slabs + v_slabs))
    g_ids = list(range(2 * q_slabs + v_slabs, 2 * q_slabs + 2 * v_slabs))
    order = q_ids + v_ids[:-1] + k_ids + v_ids[-1:] + g_ids
    n_chunks = len(order) * per_slab

    def chunk_copy(s):
        row0 = order[s // per_slab] * INPROJ_SLAB + (s % per_slab) * W_CHUNK
        slot = s % per_slab
        return pltpu.make_async_copy(w_hbm.at[pl.ds(row0, W_CHUNK), :],
                                     stage_ref.at[slot], sem.at[slot])

    def body(stream_weight):
        wnb_ref[...] = wn_ref[...].astype(BF16)
        h = _rms(x_ref[...], gain_ref[...]).astype(BF16)
        tm = h.shape[0]

        def proj(p):
            n = order[p]
            if stream_weight:
                for s in range(p * per_slab, (p + 1) * per_slab):
                    row0 = n * INPROJ_SLAB + (s % per_slab) * W_CHUNK
                    chunk_copy(s).wait()
                    w_ref[row0:row0 + W_CHUNK, :] = stage_ref[s % per_slab].astype(BF16)
                    if s + per_slab < n_chunks:
                        chunk_copy(s + per_slab).start()
            return _dot_nt(h, w_ref[n * INPROJ_SLAB:(n + 1) * INPROJ_SLAB, :])

        def store(n, r):
            o_ref[:, n * INPROJ_SLAB:(n + 1) * INPROJ_SLAB] = r.astype(o_ref.dtype)

        def chunk_decays(r, hi, lo):
            rs = slice(r * GLA_TRI, (r + 1) * GLA_TRI)
            tri = tri_ref[...]
            bcum = (jnp.dot(tri, hi[rs], preferred_element_type=F32)
                    + jnp.dot(tri, lo[rs], preferred_element_type=F32))
            for cc in range(GLA_TRI // CHUNK):
                c = r * (GLA_TRI // CHUNK) + cc
                bc = bcum[cc * CHUNK:(cc + 1) * CHUNK]
                b_end = bc[CHUNK - 1:CHUNK, :]
                e_ref[c * CHUNK:(c + 1) * CHUNK, :] = jnp.exp(b_end - bc).astype(BF16)
                dec_ref[c:c + 1, :] = jnp.exp(b_end)

        p = 0
        lr = _dot_nt(h, wlr_ref[...].astype(BF16))
        store(order[p], proj(p)); p += 1
        z = jnp.dot(lr.astype(BF16), w2_ref[...].astype(BF16),
                    preferred_element_type=F32) + bg_ref[...]
        for _ in range(1, q_slabs):
            store(order[p], proj(p)); p += 1
        la = (jnp.minimum(z, 0.0) - jnp.log(1.0 + jnp.exp(-jnp.abs(z)))) * GLA_INV_TAU
        hi = la.astype(BF16)
        lo = (la - hi.astype(F32)).astype(BF16)
        for r in range(tm // GLA_TRI):
            store(order[p], proj(p)); p += 1
            chunk_decays(r, hi, lo)
        while order[p] not in k_ids:
            store(order[p], proj(p)); p += 1
        for j in range(q_slabs):
            kc = slice(j * INPROJ_SLAB, (j + 1) * INPROJ_SLAB)
            store(order[p], proj(p) * e_ref[:, kc].astype(F32)); p += 1
        while p < len(order):
            r = proj(p)
            store(order[p], _silu(r) if order[p] in g_ids else r); p += 1

    @pl.when(pl.program_id(0) == 0)
    def _():
        for s in range(per_slab):
            chunk_copy(s).start()
        body(True)

    @pl.when(pl.program_id(0) > 0)
    def _():
        body(False)


def _gla_inproj(x2, gain, w_t, w2, b_gate, tri, w_next):
    t, d = x2.shape
    n = GLA_MAIN
    tm = INPROJ_TM
    steps = t // tm
    resident = pl.Buffered(1)
    const2 = lambda i: (0, 0)
    wn_in, wn_out, wn_shape = _cast_specs(w_next, lambda i: (i, 0), steps)
    return pl.pallas_call(
        _gla_inproj_kernel,
        grid=(steps,),
        in_specs=[
            pl.BlockSpec((tm, d), lambda i: (i, 0)),
            pl.BlockSpec((1, d), const2),
            pl.BlockSpec(memory_space=pl.ANY),
            pl.BlockSpec((GLA_RANK, d), lambda i: (GLA_MAIN // GLA_RANK, 0), pipeline_mode=resident),
            pl.BlockSpec((GLA_RANK, GLA_DK), const2, pipeline_mode=resident),
            pl.BlockSpec((1, GLA_DK), const2),
            pl.BlockSpec((GLA_TRI, GLA_TRI), const2, pipeline_mode=resident),
            wn_in,
        ],
        out_specs=[pl.BlockSpec((tm, n), lambda i: (i, 0)),
                   pl.BlockSpec((tm // CHUNK, GLA_DK), lambda i: (i, 0)), wn_out],
        out_shape=[jax.ShapeDtypeStruct((t, n), BF16),
                   jax.ShapeDtypeStruct((t // CHUNK, GLA_DK), F32), wn_shape],
        scratch_shapes=[
            pltpu.VMEM((n, d), BF16),
            pltpu.VMEM((INPROJ_SLAB // W_CHUNK, W_CHUNK, d), F32),
            pltpu.VMEM((tm, GLA_DK), BF16),
            pltpu.SemaphoreType.DMA((INPROJ_SLAB // W_CHUNK,)),
        ],
        compiler_params=pltpu.CompilerParams(
            dimension_semantics=("arbitrary",), vmem_limit_bytes=VMEM_LIMIT_LARGE),
        name="gla_inproj",
    )(x2, gain, w_t, w_t, w2, b_gate, tri, w_next)


def _sgu_inproj_kernel(x_ref, gain_ref, w_ref, wn_ref, lng_ref, lnb_ref, ws_ref, bs_ref,
                       z_ref, wnb_ref, wsm_ref, va_ref, vn_ref, tg_ref):
    @pl.when(pl.program_id(0) == 0)
    def _():
        ri = lax.broadcasted_iota(jnp.int32, (SGU_BLOCK, SGU_BLOCK), 0) // CHUNK
        ci = lax.broadcasted_iota(jnp.int32, (SGU_BLOCK, SGU_BLOCK), 1) // CHUNK
        for gi in range(SGU_GROUPS):
            wsm_ref[gi] = jnp.where(ri >= ci, ws_ref[gi], 0.0).astype(BF16)

    wnb_ref[...] = wn_ref[...].astype(BF16)
    h = _rms(x_ref[...], gain_ref[...]).astype(BF16)
    tm = h.shape[0]
    slabs = SGU_WIDTH // INPROJ_SLAB

    def proj(n):
        return jnp.dot(h, w_ref[n], preferred_element_type=F32)

    for n in range(slabs):
        va_ref[:, n * INPROJ_SLAB:(n + 1) * INPROJ_SLAB] = _gelu(proj(slabs + n)).astype(BF16)
    for r in range(tm // LN_ROWS):
        rows = slice(r * LN_ROWS, (r + 1) * LN_ROWS)
        v = va_ref[rows, :].astype(F32)
        vc = v - jnp.mean(v, axis=-1, keepdims=True)
        var = jnp.mean(vc * vc, axis=-1, keepdims=True)
        vn_ref[rows, :] = (vc * lax.rsqrt(var + EPS) * lng_ref[...] + lnb_ref[...]).astype(BF16)
    for n in range(slabs):
        tg_ref[:, n * INPROJ_SLAB:(n + 1) * INPROJ_SLAB] = _silu(proj(2 * slabs + n)).astype(BF16)
    for nb in range(tm // SGU_BLOCK):
        rows = slice(nb * SGU_BLOCK, (nb + 1) * SGU_BLOCK)
        for gi in range(SGU_GROUPS):
            cols = slice(gi * SGU_GD, (gi + 1) * SGU_GD)
            vs = (jnp.dot(wsm_ref[gi], vn_ref[rows, cols], preferred_element_type=F32)
                  + bs_ref[:, cols])
            tg_ref[rows, cols] = (vs * tg_ref[rows, cols].astype(F32)).astype(BF16)
    for n in range(slabs):
        cols = slice(n * INPROJ_SLAB, (n + 1) * INPROJ_SLAB)
        z_ref[:, cols] = (_gelu(proj(n)) * tg_ref[:, cols].astype(F32)).astype(BF16)


def _sgu_inproj(x2, gain, w, w_next, ln_gain, ln_bias, w_spatial, bias_full):
    t, d = x2.shape
    tm = INPROJ_TM
    steps = t // tm
    resident = pl.Buffered(1)
    const2 = lambda i: (0, 0)
    wn_in, wn_out, wn_shape = _cast_specs(w_next, lambda i: (i, 0), steps)
    return pl.pallas_call(
        _sgu_inproj_kernel,
        grid=(steps,),
        in_specs=[
            pl.BlockSpec((tm, d), lambda i: (i, 0)),
            pl.BlockSpec((1, d), const2),
            pl.BlockSpec(w.shape, lambda i: (0, 0, 0), pipeline_mode=resident),
            wn_in,
            pl.BlockSpec((1, SGU_WIDTH), const2),
            pl.BlockSpec((1, SGU_WIDTH), const2),
            pl.BlockSpec((SGU_GROUPS, SGU_BLOCK, SGU_BLOCK), lambda i: (0, 0, 0),
                         pipeline_mode=resident),
            pl.BlockSpec((SGU_BLOCK, SGU_WIDTH), const2, pipeline_mode=resident),
        ],
        out_specs=[pl.BlockSpec((tm, SGU_WIDTH), lambda i: (i, 0)), wn_out],
        out_shape=[jax.ShapeDtypeStruct((t, SGU_WIDTH), BF16), wn_shape],
        scratch_shapes=[
            pltpu.VMEM((SGU_GROUPS, SGU_BLOCK, SGU_BLOCK), BF16),
            pltpu.VMEM((tm, SGU_WIDTH), BF16),
            pltpu.VMEM((tm, SGU_WIDTH), BF16),
            pltpu.VMEM((tm, SGU_WIDTH), BF16),
        ],
        compiler_params=pltpu.CompilerParams(
            dimension_semantics=("arbitrary",), vmem_limit_bytes=VMEM_LIMIT),
        name="sgu_inproj_mix",
    )(x2, gain, w, w_next, ln_gain, ln_bias, w_spatial, bias_full)


def _gla_kernel(q_ref, k_ref, v_ref, g_ref, dec_ref, og_ref, wo_ref, gain_ref, x_ref, wn_ref,
                o_ref, wnb_ref, s_ref, sb_ref, a_ref):
    @pl.when(pl.program_id(1) == 0)
    def _():
        s_ref[...] = jnp.zeros_like(s_ref)

    for n in range(wnb_ref.shape[0]):
        wnb_ref[n] = wn_ref[:, n * INPROJ_SLAB:(n + 1) * INPROJ_SLAB].astype(BF16)
    n_chunks = q_ref.shape[1] // CHUNK

    def chunk(c, carry):
        r0 = pl.multiple_of(c * CHUNK, CHUNK)
        rows = pl.ds(r0, CHUNK)
        dec = dec_ref[0, pl.ds(c, 1), :]
        for h in range(GLA_HEADS):
            kc = slice(h * GLA_DKH, (h + 1) * GLA_DKH)
            vc = slice(h * GLA_DVH, (h + 1) * GLA_DVH)
            upd = lax.dot_general(k_ref[0, rows, kc], v_ref[0, rows, vc], (((0,), (0,)), ((), ())),
                                  preferred_element_type=F32)
            decay_col = jnp.transpose(jnp.broadcast_to(dec[:, kc], (LANES, GLA_DKH)))
            for t in range(GLA_DVH // LANES):
                cols = slice(t * LANES, (t + 1) * LANES)
                s_new = s_ref[h, :, cols] * decay_col + upd[:, cols]
                s_ref[h, :, cols] = s_new
                sb_ref[h, :, cols] = s_new.astype(BF16)
        for h in range(GLA_HEADS):
            kc = slice(h * GLA_DKH, (h + 1) * GLA_DKH)
            vc = slice(h * GLA_DVH, (h + 1) * GLA_DVH)
            o = jnp.dot(q_ref[0, rows, kc], sb_ref[h],
                        preferred_element_type=F32) * (GLA_DKH ** -0.5)
            o = _rms(o, og_ref[:, vc])
            a_ref[rows, vc] = (o * g_ref[0, rows, vc].astype(F32)).astype(a_ref.dtype)
        return carry

    lax.fori_loop(0, n_chunks, chunk, 0, unroll=2)

    y = jnp.dot(a_ref[...], wo_ref[...], preferred_element_type=F32)
    o_ref[0] = x_ref[0] + _rms(y, gain_ref[...])


def _gla_scan_outproj(proj3, dec3, o_gain, w_out, gain, x3, w_next):
    b, s, d = x3.shape
    tile = GLA_TILE
    tiles = s // tile
    const2 = lambda i, t: (0, 0)
    w_rows, w_cols = w_next.shape
    wn_rows = w_rows // (b * tiles)
    wn_in = pl.BlockSpec((wn_rows, w_cols), lambda i, t: (i * tiles + t, 0))
    wn_out = pl.BlockSpec((w_cols // INPROJ_SLAB, wn_rows, INPROJ_SLAB),
                          lambda i, t: (0, i * tiles + t, 0))
    wn_shape = jax.ShapeDtypeStruct((w_cols // INPROJ_SLAB, w_rows, INPROJ_SLAB), BF16)
    in_specs = [
        pl.BlockSpec((1, tile, GLA_DK), lambda i, t: (i, t, 0)),
        pl.BlockSpec((1, tile, GLA_DK), lambda i, t: (i, t, 1)),
        pl.BlockSpec((1, tile, GLA_DV), lambda i, t: (i, t, 1)),
        pl.BlockSpec((1, tile, GLA_DV), lambda i, t: (i, t, 2)),
        pl.BlockSpec((1, tile // CHUNK, GLA_DK), lambda i, t: (i, t, 0)),
        pl.BlockSpec((1, GLA_DV), const2),
        pl.BlockSpec((GLA_DV, d), const2, pipeline_mode=pl.Buffered(1)),
        pl.BlockSpec((1, d), const2),
        pl.BlockSpec((1, tile, d), lambda i, t: (i, t, 0)),
        wn_in,
    ]
    return pl.pallas_call(
        _gla_kernel,
        grid=(b, tiles),
        in_specs=in_specs,
        out_specs=[pl.BlockSpec((1, tile, d), lambda i, t: (i, t, 0)), wn_out],
        out_shape=[jax.ShapeDtypeStruct((b, s, d), F32), wn_shape],
        scratch_shapes=[
            pltpu.VMEM((GLA_HEADS, GLA_DKH, GLA_DVH), F32),
            pltpu.VMEM((GLA_HEADS, GLA_DKH, GLA_DVH), BF16),
            pltpu.VMEM((tile, GLA_DV), BF16),
        ],
        compiler_params=pltpu.CompilerParams(
            dimension_semantics=("arbitrary", "arbitrary"), vmem_limit_bytes=VMEM_LIMIT_LARGE),
        name="gla_scan_outproj",
    )(proj3, proj3, proj3, proj3, dec3, o_gain, w_out, gain, x3, w_next)


def _outproj_kernel(a_ref, w_ref, gain_ref, x_ref, o_ref):
    y = jnp.dot(a_ref[...], w_ref[...], preferred_element_type=F32)
    o_ref[...] = x_ref[...] + _rms(y, gain_ref[...])


def _outproj(a2, w, gain, x2, name):
    t, d = x2.shape
    k = a2.shape[1]
    tm = OUT_TM
    return pl.pallas_call(
        _outproj_kernel,
        grid=(t // tm,),
        in_specs=[
            pl.BlockSpec((tm, k), lambda i: (i, 0)),
            pl.BlockSpec((k, d), lambda i: (0, 0), pipeline_mode=pl.Buffered(1)),
            pl.BlockSpec((1, d), lambda i: (0, 0)),
            pl.BlockSpec((tm, d), lambda i: (i, 0)),
        ],
        out_specs=pl.BlockSpec((tm, d), lambda i: (i, 0)),
        out_shape=jax.ShapeDtypeStruct((t, d), F32),
        compiler_params=pltpu.CompilerParams(
            dimension_semantics=("arbitrary",), vmem_limit_bytes=VMEM_LIMIT),
        name=name,
    )(a2, w, gain, x2)


def kernel(x, norm_pre, norm_post, gla_w_in, gla_w_gate2, gla_b_gate, gla_o_gain, gla_w_out,
           sgu_w_in, sgu_ln_gain, sgu_ln_bias, sgu_w_spatial, sgu_b_spatial, sgu_w_out):
    b, s, d = x.shape
    t = b * s
    x2 = x.reshape(t, d)

    w_in_t = gla_w_in.reshape(d, -1).T
    ri = jnp.arange(GLA_TRI)
    tri = ((ri[:, None] >= ri[None, :]) & (ri[:, None] // CHUNK == ri[None, :] // CHUNK)).astype(BF16)
    proj, dec, gla_w_out_b = _gla_inproj(x2, norm_pre[0:1], w_in_t, gla_w_gate2[0],
                                         gla_b_gate[0:1], tri,
                                         gla_w_out.reshape(GLA_DV, d))
    x3, sgu_w_in_b = _gla_scan_outproj(
        proj.reshape(b, s, GLA_MAIN), dec.reshape(b, s // CHUNK, GLA_DK), gla_o_gain[0:1],
        gla_w_out_b, norm_post[0:1], x, sgu_w_in.reshape(d, 3 * SGU_WIDTH))
    x2 = x3.reshape(t, d)

    bias_full = jnp.repeat(sgu_b_spatial[0].T, SGU_GD, axis=1)
    z, sgu_w_out_b = _sgu_inproj(x2, norm_pre[1:2], sgu_w_in_b, sgu_w_out.reshape(SGU_WIDTH, d),
                                 sgu_ln_gain[0:1], sgu_ln_bias[0:1], sgu_w_spatial[0], bias_full)
    x2 = _outproj(z, sgu_w_out_b, norm_post[1:2], x2, name="sgu_outproj")
    return x2.reshape(b, s, d)
```

```python
import jax
import jax.numpy as jnp
from jax import lax
from jax.experimental import pallas as pl
from jax.experimental.pallas import tpu as pltpu

F32 = jnp.float32
BF16 = jnp.bfloat16

D_MODEL = 2048
EPS = 1e-6
CHUNK = 64

GLA_LAYER = 0
SGU_LAYER = 1

GLA_HEADS = 4
GLA_DK = D_MODEL // 2
GLA_DV = D_MODEL
GLA_DKH = GLA_DK // GLA_HEADS
GLA_DVH = GLA_DV // GLA_HEADS
GLA_RANK = 16
GLA_INV_TAU = 1.0 / 16.0
GLA_MAIN = 2 * GLA_DK + 2 * GLA_DV

SGU_WIDTH = D_MODEL
SGU_BLOCK = 128
SGU_GROUPS = 8
SGU_GD = SGU_WIDTH // SGU_GROUPS

LANES = 128
VMEM_LIMIT = 56 * 1024 * 1024
VMEM_LIMIT_LARGE = 60 * 1024 * 1024

INPROJ_TM = 512
INPROJ_SLAB = 512
OUT_TM = 512
LN_ROWS = 16
GLA_TILE = 512
GLA_TRI = 256
W_CHUNK = 128

GELU_C1 = (2.0 / 3.141592653589793) ** 0.5
GELU_C3 = GELU_C1 * 0.044715


def _rms(x, gain):
    return x * lax.rsqrt(jnp.mean(x * x, axis=-1, keepdims=True) + EPS) * gain


def _gelu(r):
    return (0.5 * r) * (1.0 + jnp.tanh(r * (GELU_C1 + GELU_C3 * (r * r))))


def _silu(r):
    hr = 0.5 * r
    return hr * (1.0 + jnp.tanh(hr))


def _dot_nt(a, b_t):
    return lax.dot_general(a, b_t, (((1,), (1,)), ((), ())), preferred_element_type=F32)


def _cast_specs(w_next, index_map, steps):
    rows, cols = w_next.shape
    spec = pl.BlockSpec((rows // steps, cols), index_map)
    return spec, spec, jax.ShapeDtypeStruct((rows, cols), BF16)


def _gla_inproj_kernel(x_ref, gain_ref, w_hbm, wlr_ref, w2_ref, bg_ref, wn_ref,
                       o_ref, dec_ref, wnb_ref, w_ref, stage_ref, e_ref, sem):
    per_slab = INPROJ_SLAB // W_CHUNK
    q_slabs = GLA_DK // INPROJ_SLAB
    v_slabs = GLA_DV // INPROJ_SLAB
    q_ids = list(range(q_slabs))
    k_ids = list(range(q_slabs, 2 * q_slabs))
    v_ids = list(range(2 * q_slabs, 2 * q_slabs + v_slabs))
    g_ids = list(range(2 * q_slabs + v_slabs, 2 * q_slabs + 2 * v_slabs))
    order = q_ids + v_ids[:-1] + k_ids + v_ids[-1:] + g_ids
    n_chunks = len(order) * per_slab

    def chunk_copy(s):
        row0 = order[s // per_slab] * INPROJ_SLAB + (s % per_slab) * W_CHUNK
        slot = s % per_slab
        return pltpu.make_async_copy(w_hbm.at[pl.ds(row0, W_CHUNK), :],
                                     stage_ref.at[slot], sem.at[slot])

    def body(stream_weight):
        wnb_ref[...] = wn_ref[...].astype(BF16)
        h = _rms(x_ref[...], gain_ref[GLA_LAYER:GLA_LAYER + 1, :]).astype(BF16)
        tm = h.shape[0]

        def proj(p):
            n = order[p]
            if stream_weight:
                for s in range(p * per_slab, (p + 1) * per_slab):
                    row0 = n * INPROJ_SLAB + (s % per_slab) * W_CHUNK
                    chunk_copy(s).wait()
                    w_ref[row0:row0 + W_CHUNK, :] = stage_ref[s % per_slab].astype(BF16)
                    if s + per_slab < n_chunks:
                        chunk_copy(s + per_slab).start()
            return _dot_nt(h, w_ref[n * INPROJ_SLAB:(n + 1) * INPROJ_SLAB, :])

        def store(n, r):
            o_ref[:, n * INPROJ_SLAB:(n + 1) * INPROJ_SLAB] = r.astype(o_ref.dtype)

        def chunk_decays(r, hi, lo):
            rs = slice(r * GLA_TRI, (r + 1) * GLA_TRI)
            bcum = (jnp.dot(tri, hi[rs], preferred_element_type=F32)
                    + jnp.dot(tri, lo[rs], preferred_element_type=F32))
            for cc in range(GLA_TRI // CHUNK):
                c = r * (GLA_TRI // CHUNK) + cc
                bc = bcum[cc * CHUNK:(cc + 1) * CHUNK]
                b_end = bc[CHUNK - 1:CHUNK, :]
                e_ref[c * CHUNK:(c + 1) * CHUNK, :] = jnp.exp(b_end - bc).astype(BF16)
                dec_ref[c:c + 1, :] = jnp.exp(b_end)

        ri = lax.broadcasted_iota(jnp.int32, (GLA_TRI, GLA_TRI), 0)
        ci = lax.broadcasted_iota(jnp.int32, (GLA_TRI, GLA_TRI), 1)
        tri = jnp.where((ri >= ci) & (ri // CHUNK == ci // CHUNK), 1.0, 0.0).astype(BF16)

        p = 0
        lr = _dot_nt(h, wlr_ref[...].astype(BF16))
        store(order[p], proj(p)); p += 1
        z = jnp.dot(lr.astype(BF16), w2_ref[...].astype(BF16),
                    preferred_element_type=F32) + bg_ref[...]
        for _ in range(1, q_slabs):
            store(order[p], proj(p)); p += 1
        la = (jnp.minimum(z, 0.0) - jnp.log(1.0 + jnp.exp(-jnp.abs(z)))) * GLA_INV_TAU
        hi = la.astype(BF16)
        lo = (la - hi.astype(F32)).astype(BF16)
        for r in range(tm // GLA_TRI):
            store(order[p], proj(p)); p += 1
            chunk_decays(r, hi, lo)
        while order[p] not in k_ids:
            store(order[p], proj(p)); p += 1
        for j in range(q_slabs):
            kc = slice(j * INPROJ_SLAB, (j + 1) * INPROJ_SLAB)
            store(order[p], proj(p) * e_ref[:, kc].astype(F32)); p += 1
        while p < len(order):
            r = proj(p)
            store(order[p], _silu(r) if order[p] in g_ids else r); p += 1

    @pl.when(pl.program_id(0) == 0)
    def _():
        for s in range(per_slab):
            chunk_copy(s).start()
        body(True)

    @pl.when(pl.program_id(0) > 0)
    def _():
        body(False)


def _gla_inproj(x2, gains, w_t, w2, b_gate, w_next):
    t, d = x2.shape
    n = GLA_MAIN
    tm = INPROJ_TM
    steps = t // tm
    resident = pl.Buffered(1)
    const2 = lambda i: (0, 0)
    wn_in, wn_out, wn_shape = _cast_specs(w_next, lambda i: (i, 0), steps)
    return pl.pallas_call(
        _gla_inproj_kernel,
        grid=(steps,),
        in_specs=[
            pl.BlockSpec((tm, d), lambda i: (i, 0)),
            pl.BlockSpec(gains.shape, const2),
            pl.BlockSpec(memory_space=pl.ANY),
            pl.BlockSpec((GLA_RANK, d), lambda i: (GLA_MAIN // GLA_RANK, 0), pipeline_mode=resident),
            pl.BlockSpec((GLA_RANK, GLA_DK), const2, pipeline_mode=resident),
            pl.BlockSpec((1, GLA_DK), const2),
            wn_in,
        ],
        out_specs=[pl.BlockSpec((tm, n), lambda i: (i, 0)),
                   pl.BlockSpec((tm // CHUNK, GLA_DK), lambda i: (i, 0)), wn_out],
        out_shape=[jax.ShapeDtypeStruct((t, n), BF16),
                   jax.ShapeDtypeStruct((t // CHUNK, GLA_DK), F32), wn_shape],
        scratch_shapes=[
            pltpu.VMEM((n, d), BF16),
            pltpu.VMEM((INPROJ_SLAB // W_CHUNK, W_CHUNK, d), F32),
            pltpu.VMEM((tm, GLA_DK), BF16),
            pltpu.SemaphoreType.DMA((INPROJ_SLAB // W_CHUNK,)),
        ],
        compiler_params=pltpu.CompilerParams(
            dimension_semantics=("arbitrary",), vmem_limit_bytes=VMEM_LIMIT_LARGE),
        name="gla_inproj",
    )(x2, gains, w_t, w_t, w2, b_gate, w_next)


def _sgu_inproj_kernel(x_ref, gain_ref, w_ref, wn_ref, lng_ref, lnb_ref, ws_ref, b_ref,
                       z_ref, wnb_ref, wsm_ref, bs_ref, va_ref, vn_ref, tg_ref):
    @pl.when(pl.program_id(0) == 0)
    def _():
        ri = lax.broadcasted_iota(jnp.int32, (SGU_BLOCK, SGU_BLOCK), 0) // CHUNK
        ci = lax.broadcasted_iota(jnp.int32, (SGU_BLOCK, SGU_BLOCK), 1) // CHUNK
        for gi in range(SGU_GROUPS):
            wsm_ref[gi] = jnp.where(ri >= ci, ws_ref[gi], 0.0).astype(BF16)
            col = jnp.transpose(jnp.broadcast_to(b_ref[gi:gi + 1, :], (SGU_BLOCK, SGU_BLOCK)))
            for t in range(SGU_GD // LANES):
                c0 = gi * SGU_GD + t * LANES
                bs_ref[:, c0:c0 + LANES] = col

    wnb_ref[...] = wn_ref[...].astype(BF16)
    h = _rms(x_ref[...], gain_ref[SGU_LAYER:SGU_LAYER + 1, :]).astype(BF16)
    tm = h.shape[0]
    slabs = SGU_WIDTH // INPROJ_SLAB

    def proj(n):
        return jnp.dot(h, w_ref[n], preferred_element_type=F32)

    for n in range(slabs):
        va_ref[:, n * INPROJ_SLAB:(n + 1) * INPROJ_SLAB] = _gelu(proj(slabs + n)).astype(BF16)
    for r in range(tm // LN_ROWS):
        rows = slice(r * LN_ROWS, (r + 1) * LN_ROWS)
        v = va_ref[rows, :].astype(F32)
        vc = v - jnp.mean(v, axis=-1, keepdims=True)
        var = jnp.mean(vc * vc, axis=-1, keepdims=True)
        vn_ref[rows, :] = (vc * lax.rsqrt(var + EPS) * lng_ref[...] + lnb_ref[...]).astype(BF16)
    for n in range(slabs):
        tg_ref[:, n * INPROJ_SLAB:(n + 1) * INPROJ_SLAB] = _silu(proj(2 * slabs + n)).astype(BF16)
    for nb in range(tm // SGU_BLOCK):
        rows = slice(nb * SGU_BLOCK, (nb + 1) * SGU_BLOCK)
        for gi in range(SGU_GROUPS):
            cols = slice(gi * SGU_GD, (gi + 1) * SGU_GD)
            vs = (jnp.dot(wsm_ref[gi], vn_ref[rows, cols], preferred_element_type=F32)
                  + bs_ref[:, cols])
            tg_ref[rows, cols] = (vs * tg_ref[rows, cols].astype(F32)).astype(BF16)
    for n in range(slabs):
        cols = slice(n * INPROJ_SLAB, (n + 1) * INPROJ_SLAB)
        z_ref[:, cols] = (_gelu(proj(n)) * tg_ref[:, cols].astype(F32)).astype(BF16)


def _sgu_inproj(x2, gains, w, w_next, ln_gain, ln_bias, w_spatial, b_spatial):
    t, d = x2.shape
    tm = INPROJ_TM
    steps = t // tm
    resident = pl.Buffered(1)
    const2 = lambda i: (0, 0)
    wn_in, wn_out, wn_shape = _cast_specs(w_next, lambda i: (i, 0), steps)
    return pl.pallas_call(
        _sgu_inproj_kernel,
        grid=(steps,),
        in_specs=[
            pl.BlockSpec((tm, d), lambda i: (i, 0)),
            pl.BlockSpec(gains.shape, const2),
            pl.BlockSpec(w.shape, lambda i: (0, 0, 0), pipeline_mode=resident),
            wn_in,
            pl.BlockSpec((1, SGU_WIDTH), const2),
            pl.BlockSpec((1, SGU_WIDTH), const2),
            pl.BlockSpec((SGU_GROUPS, SGU_BLOCK, SGU_BLOCK), lambda i: (0, 0, 0),
                         pipeline_mode=resident),
            pl.BlockSpec((SGU_GROUPS, SGU_BLOCK), const2, pipeline_mode=resident),
        ],
        out_specs=[pl.BlockSpec((tm, SGU_WIDTH), lambda i: (i, 0)), wn_out],
        out_shape=[jax.ShapeDtypeStruct((t, SGU_WIDTH), BF16), wn_shape],
        scratch_shapes=[
            pltpu.VMEM((SGU_GROUPS, SGU_BLOCK, SGU_BLOCK), BF16),
            pltpu.VMEM((SGU_BLOCK, SGU_WIDTH), F32),
            pltpu.VMEM((tm, SGU_WIDTH), BF16),
            pltpu.VMEM((tm, SGU_WIDTH), BF16),
            pltpu.VMEM((tm, SGU_WIDTH), BF16),
        ],
        compiler_params=pltpu.CompilerParams(
            dimension_semantics=("arbitrary",), vmem_limit_bytes=VMEM_LIMIT),
        name="sgu_inproj_mix",
    )(x2, gains, w, w_next, ln_gain, ln_bias, w_spatial, b_spatial)


def _gla_kernel(q_ref, k_ref, v_ref, g_ref, dec_ref, og_ref, wo_ref, gain_ref, x_ref, wn_ref,
                o_ref, wnb_ref, s_ref, sb_ref, a_ref):
    @pl.when(pl.program_id(1) == 0)
    def _():
        s_ref[...] = jnp.zeros_like(s_ref)

    for n in range(wnb_ref.shape[0]):
        wnb_ref[n] = wn_ref[:, n * INPROJ_SLAB:(n + 1) * INPROJ_SLAB].astype(BF16)
    n_chunks = q_ref.shape[1] // CHUNK

    def chunk(c, carry):
        r0 = pl.multiple_of(c * CHUNK, CHUNK)
        rows = pl.ds(r0, CHUNK)
        dec = dec_ref[0, pl.ds(c, 1), :]
        for h in range(GLA_HEADS):
            kc = slice(h * GLA_DKH, (h + 1) * GLA_DKH)
            vc = slice(h * GLA_DVH, (h + 1) * GLA_DVH)
            upd = lax.dot_general(k_ref[0, rows, kc], v_ref[0, rows, vc], (((0,), (0,)), ((), ())),
                                  preferred_element_type=F32)
            decay_col = jnp.transpose(jnp.broadcast_to(dec[:, kc], (LANES, GLA_DKH)))
            for t in range(GLA_DVH // LANES):
                cols = slice(t * LANES, (t + 1) * LANES)
                s_new = s_ref[h, :, cols] * decay_col + upd[:, cols]
                s_ref[h, :, cols] = s_new
                sb_ref[h, :, cols] = s_new.astype(BF16)
        for h in range(GLA_HEADS):
            kc = slice(h * GLA_DKH, (h + 1) * GLA_DKH)
            vc = slice(h * GLA_DVH, (h + 1) * GLA_DVH)
            o = jnp.dot(q_ref[0, rows, kc], sb_ref[h],
                        preferred_element_type=F32) * (GLA_DKH ** -0.5)
            o = _rms(o, og_ref[:, vc])
            a_ref[rows, vc] = (o * g_ref[0, rows, vc].astype(F32)).astype(a_ref.dtype)
        return carry

    lax.fori_loop(0, n_chunks, chunk, 0, unroll=2)

    y = jnp.dot(a_ref[...], wo_ref[...], preferred_element_type=F32)
    o_ref[0] = x_ref[0] + _rms(y, gain_ref[GLA_LAYER:GLA_LAYER + 1, :])


def _gla_scan_outproj(proj3, dec3, o_gain, w_out, gains, x3, w_next):
    b, s, d = x3.shape
    tile = GLA_TILE
    tiles = s // tile
    const2 = lambda i, t: (0, 0)
    w_rows, w_cols = w_next.shape
    wn_rows = w_rows // (b * tiles)
    wn_in = pl.BlockSpec((wn_rows, w_cols), lambda i, t: (i * tiles + t, 0))
    wn_out = pl.BlockSpec((w_cols // INPROJ_SLAB, wn_rows, INPROJ_SLAB),
                          lambda i, t: (0, i * tiles + t, 0))
    wn_shape = jax.ShapeDtypeStruct((w_cols // INPROJ_SLAB, w_rows, INPROJ_SLAB), BF16)
    in_specs = [
        pl.BlockSpec((1, tile, GLA_DK), lambda i, t: (i, t, 0)),
        pl.BlockSpec((1, tile, GLA_DK), lambda i, t: (i, t, 1)),
        pl.BlockSpec((1, tile, GLA_DV), lambda i, t: (i, t, 1)),
        pl.BlockSpec((1, tile, GLA_DV), lambda i, t: (i, t, 2)),
        pl.BlockSpec((1, tile // CHUNK, GLA_DK), lambda i, t: (i, t, 0)),
        pl.BlockSpec((1, GLA_DV), const2),
        pl.BlockSpec((GLA_DV, d), const2, pipeline_mode=pl.Buffered(1)),
        pl.BlockSpec(gains.shape, const2),
        pl.BlockSpec((1, tile, d), lambda i, t: (i, t, 0)),
        wn_in,
    ]
    return pl.pallas_call(
        _gla_kernel,
        grid=(b, tiles),
        in_specs=in_specs,
        out_specs=[pl.BlockSpec((1, tile, d), lambda i, t: (i, t, 0)), wn_out],
        out_shape=[jax.ShapeDtypeStruct((b, s, d), F32), wn_shape],
        scratch_shapes=[
            pltpu.VMEM((GLA_HEADS, GLA_DKH, GLA_DVH), F32),
            pltpu.VMEM((GLA_HEADS, GLA_DKH, GLA_DVH), BF16),
            pltpu.VMEM((tile, GLA_DV), BF16),
        ],
        compiler_params=pltpu.CompilerParams(
            dimension_semantics=("arbitrary", "arbitrary"), vmem_limit_bytes=VMEM_LIMIT_LARGE),
        name="gla_scan_outproj",
    )(proj3, proj3, proj3, proj3, dec3, o_gain, w_out, gains, x3, w_next)


def _sgu_outproj_kernel(a_ref, w_ref, gain_ref, x_ref, o_ref):
    y = jnp.dot(a_ref[...], w_ref[...], preferred_element_type=F32)
    o_ref[...] = x_ref[...] + _rms(y, gain_ref[SGU_LAYER:SGU_LAYER + 1, :])


def _sgu_outproj(a2, w, gains, x2):
    t, d = x2.shape
    k = a2.shape[1]
    tm = OUT_TM
    return pl.pallas_call(
        _sgu_outproj_kernel,
        grid=(t // tm,),
        in_specs=[
            pl.BlockSpec((tm, k), lambda i: (i, 0)),
            pl.BlockSpec((k, d), lambda i: (0, 0), pipeline_mode=pl.Buffered(1)),
            pl.BlockSpec(gains.shape, lambda i: (0, 0)),
            pl.BlockSpec((tm, d), lambda i: (i, 0)),
        ],
        out_specs=pl.BlockSpec((tm, d), lambda i: (i, 0)),
        out_shape=jax.ShapeDtypeStruct((t, d), F32),
        compiler_params=pltpu.CompilerParams(
            dimension_semantics=("arbitrary",), vmem_limit_bytes=VMEM_LIMIT),
        name="sgu_outproj",
    )(a2, w, gains, x2)


def kernel(x, norm_pre, norm_post, gla_w_in, gla_w_gate2, gla_b_gate, gla_o_gain, gla_w_out,
           sgu_w_in, sgu_ln_gain, sgu_ln_bias, sgu_w_spatial, sgu_b_spatial, sgu_w_out):
    b, s, d = x.shape
    t = b * s
    x2 = x.reshape(t, d)

    w_in_t = gla_w_in.reshape(d, -1).T
    proj, dec, gla_w_out_b = _gla_inproj(x2, norm_pre, w_in_t, gla_w_gate2.reshape(GLA_RANK, GLA_DK),
                                         gla_b_gate, gla_w_out.reshape(GLA_DV, d))
    x3, sgu_w_in_b = _gla_scan_outproj(
        proj.reshape(b, s, GLA_MAIN), dec.reshape(b, s // CHUNK, GLA_DK), gla_o_gain,
        gla_w_out_b, norm_post, x, sgu_w_in.reshape(d, 3 * SGU_WIDTH))
    x2 = x3.reshape(t, d)

    z, sgu_w_out_b = _sgu_inproj(
        x2, norm_pre, sgu_w_in_b, sgu_w_out.reshape(SGU_WIDTH, d), sgu_ln_gain, sgu_ln_bias,
        sgu_w_spatial.reshape(SGU_GROUPS, SGU_BLOCK, SGU_BLOCK),
        sgu_b_spatial.reshape(SGU_GROUPS, SGU_BLOCK))
    x2 = _sgu_outproj(z, sgu_w_out_b, norm_post, x2)
    return x2.reshape(b, s, d)
```

```python
import jax
import jax.numpy as jnp
from jax import lax
from jax.experimental import pallas as pl
from jax.experimental.pallas import tpu as pltpu

F32 = jnp.float32
BF16 = jnp.bfloat16

D_MODEL = 2048
EPS = 1e-6
CHUNK = 64

GLA_LAYER = 0
SGU_LAYER = 1

GLA_HEADS = 4
GLA_DK = D_MODEL // 2
GLA_DV = D_MODEL
GLA_DKH = GLA_DK // GLA_HEADS
GLA_DVH = GLA_DV // GLA_HEADS
GLA_RANK = 16
GLA_INV_TAU = 1.0 / 16.0
GLA_MAIN = 2 * GLA_DK + 2 * GLA_DV

SGU_WIDTH = D_MODEL
SGU_BLOCK = 128
SGU_GROUPS = 8
SGU_GD = SGU_WIDTH // SGU_GROUPS

LANES = 128
VMEM_LIMIT = 56 * 1024 * 1024
VMEM_LIMIT_LARGE = 60 * 1024 * 1024

INPROJ_TM = 512
INPROJ_SLAB = 512
OUT_TM = 512
LN_ROWS = 16
GLA_TILE = 512
GLA_TRI = 256
W_CHUNK = 128

GELU_C1 = (2.0 / 3.141592653589793) ** 0.5
GELU_C3 = GELU_C1 * 0.044715


def _rms(x, gain):
    return x * lax.rsqrt(jnp.mean(x * x, axis=-1, keepdims=True) + EPS) * gain


def _gelu(r):
    return (0.5 * r) * (1.0 + jnp.tanh(r * (GELU_C1 + GELU_C3 * (r * r))))


def _silu(r):
    hr = 0.5 * r
    return hr * (1.0 + jnp.tanh(hr))


def _dot_nt(a, b_t):
    return lax.dot_general(a, b_t, (((1,), (1,)), ((), ())), preferred_element_type=F32)


def _cast_specs(w_next, index_map, steps):
    rows, cols = w_next.shape
    spec = pl.BlockSpec((rows // steps, cols), index_map)
    return spec, spec, jax.ShapeDtypeStruct((rows, cols), BF16)


def _gla_inproj_kernel(x_ref, gain_ref, w_hbm, wlr_ref, w2_ref, bg_ref, og_ref, wn_ref,
                       o_ref, dec_ref, wnb_ref, w_ref, stage_ref, e_ref, sem):
    per_slab = INPROJ_SLAB // W_CHUNK
    q_slabs = GLA_DK // INPROJ_SLAB
    v_slabs = GLA_DV // INPROJ_SLAB
    q_ids = list(range(q_slabs))
    k_ids = list(range(q_slabs, 2 * q_slabs))
    v_ids = list(range(2 * q_slabs, 2 * q_slabs + v_slabs))
    g_ids = list(range(2 * q_slabs + v_slabs, 2 * q_slabs + 2 * v_slabs))
    order = q_ids + v_ids[:-1] + k_ids + v_ids[-1:] + g_ids
    n_chunks = len(order) * per_slab

    def chunk_copy(s):
        row0 = order[s // per_slab] * INPROJ_SLAB + (s % per_slab) * W_CHUNK
        slot = s % per_slab
        return pltpu.make_async_copy(w_hbm.at[pl.ds(row0, W_CHUNK), :],
                                     stage_ref.at[slot], sem.at[slot])

    def body(stream_weight):
        wnb_ref[...] = wn_ref[...].astype(BF16)
        h = _rms(x_ref[...], gain_ref[GLA_LAYER:GLA_LAYER + 1, :]).astype(BF16)
        tm = h.shape[0]

        def proj(p):
            n = order[p]
            if stream_weight:
                for s in range(p * per_slab, (p + 1) * per_slab):
                    row0 = n * INPROJ_SLAB + (s % per_slab) * W_CHUNK
                    chunk_copy(s).wait()
                    w_ref[row0:row0 + W_CHUNK, :] = stage_ref[s % per_slab].astype(BF16)
                    if s + per_slab < n_chunks:
                        chunk_copy(s + per_slab).start()
            return _dot_nt(h, w_ref[n * INPROJ_SLAB:(n + 1) * INPROJ_SLAB, :])

        def store(n, r):
            o_ref[:, n * INPROJ_SLAB:(n + 1) * INPROJ_SLAB] = r.astype(o_ref.dtype)

        def chunk_decays(r, hi, lo):
            rs = slice(r * GLA_TRI, (r + 1) * GLA_TRI)
            bcum = (jnp.dot(tri, hi[rs], preferred_element_type=F32)
                    + jnp.dot(tri, lo[rs], preferred_element_type=F32))
            for cc in range(GLA_TRI // CHUNK):
                c = r * (GLA_TRI // CHUNK) + cc
                bc = bcum[cc * CHUNK:(cc + 1) * CHUNK]
                b_end = bc[CHUNK - 1:CHUNK, :]
                e_ref[c * CHUNK:(c + 1) * CHUNK, :] = jnp.exp(b_end - bc).astype(BF16)
                dec_ref[c:c + 1, :] = jnp.exp(b_end)

        ri = lax.broadcasted_iota(jnp.int32, (GLA_TRI, GLA_TRI), 0)
        ci = lax.broadcasted_iota(jnp.int32, (GLA_TRI, GLA_TRI), 1)
        tri = jnp.where((ri >= ci) & (ri // CHUNK == ci // CHUNK), 1.0, 0.0).astype(BF16)

        p = 0
        lr = _dot_nt(h, wlr_ref[...].astype(BF16))
        store(order[p], proj(p)); p += 1
        z = jnp.dot(lr.astype(BF16), w2_ref[...].astype(BF16),
                    preferred_element_type=F32) + bg_ref[...]
        for _ in range(1, q_slabs):
            store(order[p], proj(p)); p += 1
        la = (jnp.minimum(z, 0.0) - jnp.log(1.0 + jnp.exp(-jnp.abs(z)))) * GLA_INV_TAU
        hi = la.astype(BF16)
        lo = (la - hi.astype(F32)).astype(BF16)
        for r in range(tm // GLA_TRI):
            store(order[p], proj(p)); p += 1
            chunk_decays(r, hi, lo)
        while order[p] not in k_ids:
            store(order[p], proj(p)); p += 1
        for j in range(q_slabs):
            kc = slice(j * INPROJ_SLAB, (j + 1) * INPROJ_SLAB)
            store(order[p], proj(p) * e_ref[:, kc].astype(F32)); p += 1
        while p < len(order):
            r = proj(p)
            if order[p] in g_ids:
                gc = slice((order[p] - g_ids[0]) * INPROJ_SLAB, (order[p] - g_ids[0] + 1) * INPROJ_SLAB)
                r = _silu(r) * og_ref[:, gc]
            store(order[p], r); p += 1

    @pl.when(pl.program_id(0) == 0)
    def _():
        for s in range(per_slab):
            chunk_copy(s).start()
        body(True)

    @pl.when(pl.program_id(0) > 0)
    def _():
        body(False)


def _gla_inproj(x2, gains, w_t, w2, b_gate, o_gain, w_next):
    t, d = x2.shape
    n = GLA_MAIN
    tm = INPROJ_TM
    steps = t // tm
    resident = pl.Buffered(1)
    const2 = lambda i: (0, 0)
    wn_in, wn_out, wn_shape = _cast_specs(w_next, lambda i: (i, 0), steps)
    return pl.pallas_call(
        _gla_inproj_kernel,
        grid=(steps,),
        in_specs=[
            pl.BlockSpec((tm, d), lambda i: (i, 0)),
            pl.BlockSpec(gains.shape, const2),
            pl.BlockSpec(memory_space=pl.ANY),
            pl.BlockSpec((GLA_RANK, d), lambda i: (GLA_MAIN // GLA_RANK, 0), pipeline_mode=resident),
            pl.BlockSpec((GLA_RANK, GLA_DK), const2, pipeline_mode=resident),
            pl.BlockSpec((1, GLA_DK), const2),
            pl.BlockSpec((1, GLA_DV), const2),
            wn_in,
        ],
        out_specs=[pl.BlockSpec((tm, n), lambda i: (i, 0)),
                   pl.BlockSpec((tm // CHUNK, GLA_DK), lambda i: (i, 0)), wn_out],
        out_shape=[jax.ShapeDtypeStruct((t, n), BF16),
                   jax.ShapeDtypeStruct((t // CHUNK, GLA_DK), F32), wn_shape],
        scratch_shapes=[
            pltpu.VMEM((n, d), BF16),
            pltpu.VMEM((INPROJ_SLAB // W_CHUNK, W_CHUNK, d), F32),
            pltpu.VMEM((tm, GLA_DK), BF16),
            pltpu.SemaphoreType.DMA((INPROJ_SLAB // W_CHUNK,)),
        ],
        compiler_params=pltpu.CompilerParams(
            dimension_semantics=("arbitrary",), vmem_limit_bytes=VMEM_LIMIT_LARGE),
        name="gla_inproj",
    )(x2, gains, w_t, w_t, w2, b_gate, o_gain, w_next)


def _sgu_inproj_kernel(x_ref, gain_ref, w_ref, wn_ref, lng_ref, lnb_ref, ws_ref, b_ref,
                       z_ref, wnb_ref, wsm_ref, bs_ref, va_ref, vn_ref, tg_ref):
    @pl.when(pl.program_id(0) == 0)
    def _():
        ri = lax.broadcasted_iota(jnp.int32, (SGU_BLOCK, SGU_BLOCK), 0) // CHUNK
        ci = lax.broadcasted_iota(jnp.int32, (SGU_BLOCK, SGU_BLOCK), 1) // CHUNK
        for gi in range(SGU_GROUPS):
            wsm_ref[gi] = jnp.where(ri >= ci, ws_ref[gi], 0.0).astype(BF16)
            col = jnp.transpose(jnp.broadcast_to(b_ref[gi:gi + 1, :], (SGU_BLOCK, SGU_BLOCK)))
            for t in range(SGU_GD // LANES):
                c0 = gi * SGU_GD + t * LANES
                bs_ref[:, c0:c0 + LANES] = col

    wnb_ref[...] = wn_ref[...].astype(BF16)
    h = _rms(x_ref[...], gain_ref[SGU_LAYER:SGU_LAYER + 1, :]).astype(BF16)
    tm = h.shape[0]
    slabs = SGU_WIDTH // INPROJ_SLAB

    def proj(n):
        return jnp.dot(h, w_ref[n], preferred_element_type=F32)

    for n in range(slabs):
        va_ref[:, n * INPROJ_SLAB:(n + 1) * INPROJ_SLAB] = _gelu(proj(slabs + n)).astype(BF16)
    for r in range(tm // LN_ROWS):
        rows = slice(r * LN_ROWS, (r + 1) * LN_ROWS)
        v = va_ref[rows, :].astype(F32)
        vc = v - jnp.mean(v, axis=-1, keepdims=True)
        var = jnp.mean(vc * vc, axis=-1, keepdims=True)
        vn_ref[rows, :] = (vc * lax.rsqrt(var + EPS) * lng_ref[...] + lnb_ref[...]).astype(BF16)
    for n in range(slabs):
        tg_ref[:, n * INPROJ_SLAB:(n + 1) * INPROJ_SLAB] = _silu(proj(2 * slabs + n)).astype(BF16)
    for nb in range(tm // SGU_BLOCK):
        rows = slice(nb * SGU_BLOCK, (nb + 1) * SGU_BLOCK)
        for gi in range(SGU_GROUPS):
            cols = slice(gi * SGU_GD, (gi + 1) * SGU_GD)
            vs = (jnp.dot(wsm_ref[gi], vn_ref[rows, cols], preferred_element_type=F32)
                  + bs_ref[:, cols])
            tg_ref[rows, cols] = (vs * tg_ref[rows, cols].astype(F32)).astype(BF16)
    for n in range(slabs):
        cols = slice(n * INPROJ_SLAB, (n + 1) * INPROJ_SLAB)
        z_ref[:, cols] = (_gelu(proj(n)) * tg_ref[:, cols].astype(F32)).astype(BF16)


def _sgu_inproj(x2, gains, w, w_next, ln_gain, ln_bias, w_spatial, b_spatial):
    t, d = x2.shape
    tm = INPROJ_TM
    steps = t // tm
    resident = pl.Buffered(1)
    const2 = lambda i: (0, 0)
    wn_in, wn_out, wn_shape = _cast_specs(w_next, lambda i: (i, 0), steps)
    return pl.pallas_call(
        _sgu_inproj_kernel,
        grid=(steps,),
        in_specs=[
            pl.BlockSpec((tm, d), lambda i: (i, 0)),
            pl.BlockSpec(gains.shape, const2),
            pl.BlockSpec(w.shape, lambda i: (0, 0, 0), pipeline_mode=resident),
            wn_in,
            pl.BlockSpec((1, SGU_WIDTH), const2),
            pl.BlockSpec((1, SGU_WIDTH), const2),
            pl.BlockSpec((SGU_GROUPS, SGU_BLOCK, SGU_BLOCK), lambda i: (0, 0, 0),
                         pipeline_mode=resident),
            pl.BlockSpec((SGU_GROUPS, SGU_BLOCK), const2, pipeline_mode=resident),
        ],
        out_specs=[pl.BlockSpec((tm, SGU_WIDTH), lambda i: (i, 0)), wn_out],
        out_shape=[jax.ShapeDtypeStruct((t, SGU_WIDTH), BF16), wn_shape],
        scratch_shapes=[
            pltpu.VMEM((SGU_GROUPS, SGU_BLOCK, SGU_BLOCK), BF16),
            pltpu.VMEM((SGU_BLOCK, SGU_WIDTH), F32),
            pltpu.VMEM((tm, SGU_WIDTH), BF16),
            pltpu.VMEM((tm, SGU_WIDTH), BF16),
            pltpu.VMEM((tm, SGU_WIDTH), BF16),
        ],
        compiler_params=pltpu.CompilerParams(
            dimension_semantics=("arbitrary",), vmem_limit_bytes=VMEM_LIMIT),
        name="sgu_inproj_mix",
    )(x2, gains, w, w_next, ln_gain, ln_bias, w_spatial, b_spatial)


def _gla_kernel(q_ref, k_ref, v_ref, g_ref, dec_ref, wo_ref, gain_ref, x_ref, wn_ref,
                o_ref, wnb_ref, s_ref, a_ref):
    @pl.when(pl.program_id(1) == 0)
    def _():
        s_ref[...] = jnp.zeros_like(s_ref)

    for n in range(wnb_ref.shape[0]):
        wnb_ref[n] = wn_ref[:, n * INPROJ_SLAB:(n + 1) * INPROJ_SLAB].astype(BF16)
    n_chunks = q_ref.shape[1] // CHUNK

    def chunk(c, carry):
        r0 = pl.multiple_of(c * CHUNK, CHUNK)
        rows = pl.ds(r0, CHUNK)
        dec = dec_ref[0, pl.ds(c, 1), :]
        for h in range(GLA_HEADS):
            kc = slice(h * GLA_DKH, (h + 1) * GLA_DKH)
            vc = slice(h * GLA_DVH, (h + 1) * GLA_DVH)
            upd = lax.dot_general(k_ref[0, rows, kc], v_ref[0, rows, vc], (((0,), (0,)), ((), ())),
                                  preferred_element_type=F32)
            decay_col = jnp.transpose(jnp.broadcast_to(dec[:, kc], (LANES, GLA_DKH)))
            for t in range(GLA_DVH // LANES):
                cols = slice(t * LANES, (t + 1) * LANES)
                s_new = s_ref[h, :, cols] * decay_col + upd[:, cols]
                s_ref[h, :, cols] = s_new
        for h in range(GLA_HEADS):
            kc = slice(h * GLA_DKH, (h + 1) * GLA_DKH)
            vc = slice(h * GLA_DVH, (h + 1) * GLA_DVH)
            o = jnp.dot(q_ref[0, rows, kc], s_ref[h].astype(BF16),
                        preferred_element_type=F32)
            ms = jnp.mean(o * o, axis=-1, keepdims=True) * (1.0 / GLA_DKH)
            scale = lax.rsqrt(ms + EPS) * (GLA_DKH ** -0.5)
            a_ref[rows, vc] = (o * scale * g_ref[0, rows, vc].astype(F32)).astype(a_ref.dtype)
        return carry

    lax.fori_loop(0, n_chunks, chunk, 0, unroll=2)

    y = jnp.dot(a_ref[...], wo_ref[...], preferred_element_type=F32)
    o_ref[0] = x_ref[0] + _rms(y, gain_ref[GLA_LAYER:GLA_LAYER + 1, :])


def _gla_scan_outproj(proj3, dec3, w_out, gains, x3, w_next):
    b, s, d = x3.shape
    tile = GLA_TILE
    tiles = s // tile
    const2 = lambda i, t: (0, 0)
    w_rows, w_cols = w_next.shape
    wn_rows = w_rows // (b * tiles)
    wn_in = pl.BlockSpec((wn_rows, w_cols), lambda i, t: (i * tiles + t, 0))
    wn_out = pl.BlockSpec((w_cols // INPROJ_SLAB, wn_rows, INPROJ_SLAB),
                          lambda i, t: (0, i * tiles + t, 0))
    wn_shape = jax.ShapeDtypeStruct((w_cols // INPROJ_SLAB, w_rows, INPROJ_SLAB), BF16)
    in_specs = [
        pl.BlockSpec((1, tile, GLA_DK), lambda i, t: (i, t, 0)),
        pl.BlockSpec((1, tile, GLA_DK), lambda i, t: (i, t, 1)),
        pl.BlockSpec((1, tile, GLA_DV), lambda i, t: (i, t, 1)),
        pl.BlockSpec((1, tile, GLA_DV), lambda i, t: (i, t, 2)),
        pl.BlockSpec((1, tile // CHUNK, GLA_DK), lambda i, t: (i, t, 0)),
        pl.BlockSpec((GLA_DV, d), const2, pipeline_mode=pl.Buffered(1)),
        pl.BlockSpec(gains.shape, const2),
        pl.BlockSpec((1, tile, d), lambda i, t: (i, t, 0)),
        wn_in,
    ]
    return pl.pallas_call(
        _gla_kernel,
        grid=(b, tiles),
        in_specs=in_specs,
        out_specs=[pl.BlockSpec((1, tile, d), lambda i, t: (i, t, 0)), wn_out],
        out_shape=[jax.ShapeDtypeStruct((b, s, d), F32), wn_shape],
        scratch_shapes=[
            pltpu.VMEM((GLA_HEADS, GLA_DKH, GLA_DVH), F32),
            pltpu.VMEM((tile, GLA_DV), BF16),
        ],
        compiler_params=pltpu.CompilerParams(
            dimension_semantics=("arbitrary", "arbitrary"), vmem_limit_bytes=VMEM_LIMIT_LARGE),
        name="gla_scan_outproj",
    )(proj3, proj3, proj3, proj3, dec3, w_out, gains, x3, w_next)


def _sgu_outproj_kernel(a_ref, w_ref, gain_ref, x_ref, o_ref):
    y = jnp.dot(a_ref[...], w_ref[...], preferred_element_type=F32)
    o_ref[...] = x_ref[...] + _rms(y, gain_ref[SGU_LAYER:SGU_LAYER + 1, :])


def _sgu_outproj(a2, w, gains, x2):
    t, d = x2.shape
    k = a2.shape[1]
    tm = OUT_TM
    return pl.pallas_call(
        _sgu_outproj_kernel,
        grid=(t // tm,),
        in_specs=[
            pl.BlockSpec((tm, k), lambda i: (i, 0)),
            pl.BlockSpec((k, d), lambda i: (0, 0), pipeline_mode=pl.Buffered(1)),
            pl.BlockSpec(gains.shape, lambda i: (0, 0)),
            pl.BlockSpec((tm, d), lambda i: (i, 0)),
        ],
        out_specs=pl.BlockSpec((tm, d), lambda i: (i, 0)),
        out_shape=jax.ShapeDtypeStruct((t, d), F32),
        compiler_params=pltpu.CompilerParams(
            dimension_semantics=("arbitrary",), vmem_limit_bytes=VMEM_LIMIT),
        name="sgu_outproj",
    )(a2, w, gains, x2)


def kernel(x, norm_pre, norm_post, gla_w_in, gla_w_gate2, gla_b_gate, gla_o_gain, gla_w_out,
           sgu_w_in, sgu_ln_gain, sgu_ln_bias, sgu_w_spatial, sgu_b_spatial, sgu_w_out):
    b, s, d = x.shape
    t = b * s
    x2 = x.reshape(t, d)

    w_in_t = gla_w_in.reshape(d, -1).T
    proj, dec, gla_w_out_b = _gla_inproj(x2, norm_pre, w_in_t, gla_w_gate2.reshape(GLA_RANK, GLA_DK),
                                         gla_b_gate, gla_o_gain, gla_w_out.reshape(GLA_DV, d))
    x3, sgu_w_in_b = _gla_scan_outproj(
        proj.reshape(b, s, GLA_MAIN), dec.reshape(b, s // CHUNK, GLA_DK),
        gla_w_out_b, norm_post, x, sgu_w_in.reshape(d, 3 * SGU_WIDTH))
    x2 = x3.reshape(t, d)

    z, sgu_w_out_b = _sgu_inproj(
        x2, norm_pre, sgu_w_in_b, sgu_w_out.reshape(SGU_WIDTH, d), sgu_ln_gain, sgu_ln_bias,
        sgu_w_spatial.reshape(SGU_GROUPS, SGU_BLOCK, SGU_BLOCK),
        sgu_b_spatial.reshape(SGU_GROUPS, SGU_BLOCK))
    x2 = _sgu_outproj(z, sgu_w_out_b, norm_post, x2)
    return x2.reshape(b, s, d)
```

```python
import jax
import jax.numpy as jnp
from jax import lax
from jax.experimental import pallas as pl
from jax.experimental.pallas import tpu as pltpu

F32 = jnp.float32
BF16 = jnp.bfloat16

D_MODEL = 2048
EPS = 1e-6
CHUNK = 64

GLA_LAYER = 0
SGU_LAYER = 1

GLA_HEADS = 4
GLA_DK = D_MODEL // 2
GLA_DV = D_MODEL
GLA_DKH = GLA_DK // GLA_HEADS
GLA_DVH = GLA_DV // GLA_HEADS
GLA_RANK = 16
GLA_INV_TAU = 1.0 / 16.0
GLA_MAIN = 2 * GLA_DK + 2 * GLA_DV

SGU_WIDTH = D_MODEL
SGU_BLOCK = 128
SGU_GROUPS = 8
SGU_GD = SGU_WIDTH // SGU_GROUPS

LANES = 128
VMEM_LIMIT = 56 * 1024 * 1024
VMEM_LIMIT_LARGE = 60 * 1024 * 1024

INPROJ_TM = 512
INPROJ_SLAB = 512
OUT_TM = 512
LN_ROWS = 16
GLA_TILE = 512
GLA_TRI = 256
W_CHUNK = 128

GELU_C1 = (2.0 / 3.141592653589793) ** 0.5
GELU_C3 = GELU_C1 * 0.044715


def _rms(x, gain):
    return x * lax.rsqrt(jnp.mean(x * x, axis=-1, keepdims=True) + EPS) * gain


def _gelu(r):
    return (0.5 * r) * (1.0 + jnp.tanh(r * (GELU_C1 + GELU_C3 * (r * r))))


def _silu(r):
    hr = 0.5 * r
    return hr * (1.0 + jnp.tanh(hr))


def _dot_nt(a, b_t):
    return lax.dot_general(a, b_t, (((1,), (1,)), ((), ())), preferred_element_type=F32)


def _cast_specs(w_next, index_map, steps):
    rows, cols = w_next.shape
    spec = pl.BlockSpec((rows // steps, cols), index_map)
    return spec, spec, jax.ShapeDtypeStruct((rows, cols), BF16)


def _gla_inproj_kernel(x_ref, gain_ref, w_hbm, wlr_ref, w2_ref, bg_ref, og_ref, wn_ref,
                       o_ref, dec_ref, wnb_ref, w_ref, stage_ref, e_ref, sem):
    per_slab = INPROJ_SLAB // W_CHUNK
    q_slabs = GLA_DK // INPROJ_SLAB
    v_slabs = GLA_DV // INPROJ_SLAB
    q_ids = list(range(q_slabs))
    k_ids = list(range(q_slabs, 2 * q_slabs))
    v_ids = list(range(2 * q_slabs, 2 * q_slabs + v_slabs))
    g_ids = list(range(2 * q_slabs + v_slabs, 2 * q_slabs + 2 * v_slabs))
    order = q_ids + v_ids[:-1] + k_ids + v_ids[-1:] + g_ids
    n_chunks = len(order) * per_slab

    def chunk_copy(s):
        row0 = order[s // per_slab] * INPROJ_SLAB + (s % per_slab) * W_CHUNK
        slot = s % per_slab
        return pltpu.make_async_copy(w_hbm.at[pl.ds(row0, W_CHUNK), :],
                                     stage_ref.at[slot], sem.at[slot])

    def body(stream_weight):
        wnb_ref[...] = wn_ref[...].astype(BF16)
        h = _rms(x_ref[...], gain_ref[GLA_LAYER:GLA_LAYER + 1, :]).astype(BF16)
        tm = h.shape[0]

        def proj(p):
            n = order[p]
            if stream_weight:
                for s in range(p * per_slab, (p + 1) * per_slab):
                    row0 = n * INPROJ_SLAB + (s % per_slab) * W_CHUNK
                    chunk_copy(s).wait()
                    w_ref[row0:row0 + W_CHUNK, :] = stage_ref[s % per_slab].astype(BF16)
                    if s + per_slab < n_chunks:
                        chunk_copy(s + per_slab).start()
            return _dot_nt(h, w_ref[n * INPROJ_SLAB:(n + 1) * INPROJ_SLAB, :])

        def store(n, r):
            o_ref[:, n * INPROJ_SLAB:(n + 1) * INPROJ_SLAB] = r.astype(o_ref.dtype)

        def chunk_decays(r, hi, lo):
            rs = slice(r * GLA_TRI, (r + 1) * GLA_TRI)
            bcum = (jnp.dot(tri, hi[rs], preferred_element_type=F32)
                    + jnp.dot(tri, lo[rs], preferred_element_type=F32))
            for cc in range(GLA_TRI // CHUNK):
                c = r * (GLA_TRI // CHUNK) + cc
                bc = bcum[cc * CHUNK:(cc + 1) * CHUNK]
                b_end = bc[CHUNK - 1:CHUNK, :]
                e_ref[c * CHUNK:(c + 1) * CHUNK, :] = jnp.exp(b_end - bc).astype(BF16)
                dec_ref[c:c + 1, :] = jnp.exp(b_end)

        ri = lax.broadcasted_iota(jnp.int32, (GLA_TRI, GLA_TRI), 0)
        ci = lax.broadcasted_iota(jnp.int32, (GLA_TRI, GLA_TRI), 1)
        tri = jnp.where((ri >= ci) & (ri // CHUNK == ci // CHUNK), 1.0, 0.0).astype(BF16)

        p = 0
        lr = _dot_nt(h, wlr_ref[...].astype(BF16))
        store(order[p], proj(p)); p += 1
        z = jnp.dot(lr.astype(BF16), w2_ref[...].astype(BF16),
                    preferred_element_type=F32) + bg_ref[...]
        for _ in range(1, q_slabs):
            store(order[p], proj(p)); p += 1
        la = (jnp.minimum(z, 0.0) - jnp.log(1.0 + jnp.exp(-jnp.abs(z)))) * GLA_INV_TAU
        hi = la.astype(BF16)
        lo = (la - hi.astype(F32)).astype(BF16)
        for r in range(tm // GLA_TRI):
            store(order[p], proj(p)); p += 1
            chunk_decays(r, hi, lo)
        while order[p] not in k_ids:
            store(order[p], proj(p)); p += 1
        for j in range(q_slabs):
            kc = slice(j * INPROJ_SLAB, (j + 1) * INPROJ_SLAB)
            store(order[p], proj(p) * e_ref[:, kc].astype(F32)); p += 1
        while p < len(order):
            r = proj(p)
            if order[p] in g_ids:
                gc = slice((order[p] - g_ids[0]) * INPROJ_SLAB, (order[p] - g_ids[0] + 1) * INPROJ_SLAB)
                r = _silu(r) * og_ref[:, gc]
            store(order[p], r); p += 1

    @pl.when(pl.program_id(0) == 0)
    def _():
        for s in range(per_slab):
            chunk_copy(s).start()
        body(True)

    @pl.when(pl.program_id(0) > 0)
    def _():
        body(False)


def _gla_inproj(x2, gains, w_t, w2, b_gate, o_gain, w_next):
    t, d = x2.shape
    n = GLA_MAIN
    tm = INPROJ_TM
    steps = t // tm
    resident = pl.Buffered(1)
    const2 = lambda i: (0, 0)
    wn_in, wn_out, wn_shape = _cast_specs(w_next, lambda i: (i, 0), steps)
    return pl.pallas_call(
        _gla_inproj_kernel,
        grid=(steps,),
        in_specs=[
            pl.BlockSpec((tm, d), lambda i: (i, 0)),
            pl.BlockSpec(gains.shape, const2),
            pl.BlockSpec(memory_space=pl.ANY),
            pl.BlockSpec((GLA_RANK, d), lambda i: (GLA_MAIN // GLA_RANK, 0), pipeline_mode=resident),
            pl.BlockSpec((GLA_RANK, GLA_DK), const2, pipeline_mode=resident),
            pl.BlockSpec((1, GLA_DK), const2),
            pl.BlockSpec((1, GLA_DV), const2),
            wn_in,
        ],
        out_specs=[pl.BlockSpec((tm, n), lambda i: (i, 0)),
                   pl.BlockSpec((tm // CHUNK, GLA_DK), lambda i: (i, 0)), wn_out],
        out_shape=[jax.ShapeDtypeStruct((t, n), BF16),
                   jax.ShapeDtypeStruct((t // CHUNK, GLA_DK), F32), wn_shape],
        scratch_shapes=[
            pltpu.VMEM((n, d), BF16),
            pltpu.VMEM((INPROJ_SLAB // W_CHUNK, W_CHUNK, d), F32),
            pltpu.VMEM((tm, GLA_DK), BF16),
            pltpu.SemaphoreType.DMA((INPROJ_SLAB // W_CHUNK,)),
        ],
        compiler_params=pltpu.CompilerParams(
            dimension_semantics=("arbitrary",), vmem_limit_bytes=VMEM_LIMIT_LARGE),
        name="gla_inproj",
    )(x2, gains, w_t, w_t, w2, b_gate, o_gain, w_next)


def _sgu_inproj_kernel(x_ref, gain_ref, w_ref, wn_ref, lng_ref, lnb_ref, ws_ref, b_ref,
                       z_ref, wnb_ref, wsm_ref, bs_ref, va_ref, vn_ref, tg_ref):
    @pl.when(pl.program_id(0) == 0)
    def _():
        ri = lax.broadcasted_iota(jnp.int32, (SGU_BLOCK, SGU_BLOCK), 0) // CHUNK
        ci = lax.broadcasted_iota(jnp.int32, (SGU_BLOCK, SGU_BLOCK), 1) // CHUNK
        for gi in range(SGU_GROUPS):
            wsm_ref[gi] = jnp.where(ri >= ci, ws_ref[gi], 0.0).astype(BF16)
            col = jnp.transpose(jnp.broadcast_to(b_ref[gi:gi + 1, :], (SGU_BLOCK, SGU_BLOCK)))
            for t in range(SGU_GD // LANES):
                c0 = gi * SGU_GD + t * LANES
                bs_ref[:, c0:c0 + LANES] = col

    wnb_ref[...] = wn_ref[...].astype(BF16)
    h = _rms(x_ref[...], gain_ref[SGU_LAYER:SGU_LAYER + 1, :]).astype(BF16)
    tm = h.shape[0]
    slabs = SGU_WIDTH // INPROJ_SLAB

    def proj(n):
        return jnp.dot(h, w_ref[n], preferred_element_type=F32)

    for n in range(slabs):
        va_ref[:, n * INPROJ_SLAB:(n + 1) * INPROJ_SLAB] = _gelu(proj(slabs + n)).astype(BF16)
    for r in range(tm // LN_ROWS):
        rows = slice(r * LN_ROWS, (r + 1) * LN_ROWS)
        v = va_ref[rows, :].astype(F32)
        vc = v - jnp.mean(v, axis=-1, keepdims=True)
        var = jnp.mean(vc * vc, axis=-1, keepdims=True)
        vn_ref[rows, :] = (vc * lax.rsqrt(var + EPS) * lng_ref[...] + lnb_ref[...]).astype(BF16)
    for n in range(slabs):
        tg_ref[:, n * INPROJ_SLAB:(n + 1) * INPROJ_SLAB] = _silu(proj(2 * slabs + n)).astype(BF16)
    for nb in range(tm // SGU_BLOCK):
        rows = slice(nb * SGU_BLOCK, (nb + 1) * SGU_BLOCK)
        for gi in range(SGU_GROUPS):
            cols = slice(gi * SGU_GD, (gi + 1) * SGU_GD)
            vs = (jnp.dot(wsm_ref[gi], vn_ref[rows, cols], preferred_element_type=F32)
                  + bs_ref[:, cols])
            tg_ref[rows, cols] = (vs * tg_ref[rows, cols].astype(F32)).astype(BF16)
    for n in range(slabs):
        cols = slice(n * INPROJ_SLAB, (n + 1) * INPROJ_SLAB)
        z_ref[:, cols] = (_gelu(proj(n)) * tg_ref[:, cols].astype(F32)).astype(BF16)


def _sgu_inproj(x2, gains, w, w_next, ln_gain, ln_bias, w_spatial, b_spatial):
    t, d = x2.shape
    tm = INPROJ_TM
    steps = t // tm
    resident = pl.Buffered(1)
    const2 = lambda i: (0, 0)
    wn_in, wn_out, wn_shape = _cast_specs(w_next, lambda i: (i, 0), steps)
    return pl.pallas_call(
        _sgu_inproj_kernel,
        grid=(steps,),
        in_specs=[
            pl.BlockSpec((tm, d), lambda i: (i, 0)),
            pl.BlockSpec(gains.shape, const2),
            pl.BlockSpec(w.shape, lambda i: (0, 0, 0), pipeline_mode=resident),
            wn_in,
            pl.BlockSpec((1, SGU_WIDTH), const2),
            pl.BlockSpec((1, SGU_WIDTH), const2),
            pl.BlockSpec((SGU_GROUPS, SGU_BLOCK, SGU_BLOCK), lambda i: (0, 0, 0),
                         pipeline_mode=resident),
            pl.BlockSpec((SGU_GROUPS, SGU_BLOCK), const2, pipeline_mode=resident),
        ],
        out_specs=[pl.BlockSpec((tm, SGU_WIDTH), lambda i: (i, 0)), wn_out],
        out_shape=[jax.ShapeDtypeStruct((t, SGU_WIDTH), BF16), wn_shape],
        scratch_shapes=[
            pltpu.VMEM((SGU_GROUPS, SGU_BLOCK, SGU_BLOCK), BF16),
            pltpu.VMEM((SGU_BLOCK, SGU_WIDTH), F32),
            pltpu.VMEM((tm, SGU_WIDTH), BF16),
            pltpu.VMEM((tm, SGU_WIDTH), BF16),
            pltpu.VMEM((tm, SGU_WIDTH), BF16),
        ],
        compiler_params=pltpu.CompilerParams(
            dimension_semantics=("arbitrary",), vmem_limit_bytes=VMEM_LIMIT),
        name="sgu_inproj_mix",
    )(x2, gains, w, w_next, ln_gain, ln_bias, w_spatial, b_spatial)


def _gla_kernel(q_ref, k_ref, v_ref, g_ref, dec_ref, wo_ref, gain_ref, x_ref, wn_ref,
                o_ref, wnb_ref, s_ref, a_ref):
    @pl.when(pl.program_id(1) == 0)
    def _():
        s_ref[...] = jnp.zeros_like(s_ref)

    for n in range(wnb_ref.shape[0]):
        wnb_ref[n] = wn_ref[:, n * INPROJ_SLAB:(n + 1) * INPROJ_SLAB].astype(BF16)
    n_chunks = q_ref.shape[1] // CHUNK

    def chunk(c, carry):
        r0 = pl.multiple_of(c * CHUNK, CHUNK)
        rows = pl.ds(r0, CHUNK)
        dec = dec_ref[0, pl.ds(c, 1), :]
        for h in range(GLA_HEADS):
            kc = slice(h * GLA_DKH, (h + 1) * GLA_DKH)
            vc = slice(h * GLA_DVH, (h + 1) * GLA_DVH)
            upd = lax.dot_general(k_ref[0, rows, kc], v_ref[0, rows, vc], (((0,), (0,)), ((), ())),
                                  preferred_element_type=F32)
            decay_col = jnp.transpose(jnp.broadcast_to(dec[:, kc], (LANES, GLA_DKH)))
            for t in range(GLA_DVH // LANES):
                cols = slice(t * LANES, (t + 1) * LANES)
                s_new = s_ref[h, :, cols] * decay_col + upd[:, cols]
                s_ref[h, :, cols] = s_new
        for h in range(GLA_HEADS):
            kc = slice(h * GLA_DKH, (h + 1) * GLA_DKH)
            vc = slice(h * GLA_DVH, (h + 1) * GLA_DVH)
            o = jnp.dot(q_ref[0, rows, kc], s_ref[h].astype(BF16),
                        preferred_element_type=F32)
            ms = jnp.mean(o * o, axis=-1, keepdims=True) * (1.0 / GLA_DKH)
            scale = lax.rsqrt(ms + EPS) * (GLA_DKH ** -0.5)
            a_ref[rows, vc] = (o * scale * g_ref[0, rows, vc].astype(F32)).astype(a_ref.dtype)
        return carry

    lax.fori_loop(0, n_chunks, chunk, 0, unroll=True)

    y = jnp.dot(a_ref[...], wo_ref[...], preferred_element_type=F32)
    o_ref[0] = x_ref[0] + _rms(y, gain_ref[GLA_LAYER:GLA_LAYER + 1, :])


def _gla_scan_outproj(proj3, dec3, w_out, gains, x3, w_next):
    b, s, d = x3.shape
    tile = GLA_TILE
    tiles = s // tile
    const2 = lambda i, t: (0, 0)
    w_rows, w_cols = w_next.shape
    wn_rows = w_rows // (b * tiles)
    wn_in = pl.BlockSpec((wn_rows, w_cols), lambda i, t: (i * tiles + t, 0))
    wn_out = pl.BlockSpec((w_cols // INPROJ_SLAB, wn_rows, INPROJ_SLAB),
                          lambda i, t: (0, i * tiles + t, 0))
    wn_shape = jax.ShapeDtypeStruct((w_cols // INPROJ_SLAB, w_rows, INPROJ_SLAB), BF16)
    in_specs = [
        pl.BlockSpec((1, tile, GLA_DK), lambda i, t: (i, t, 0)),
        pl.BlockSpec((1, tile, GLA_DK), lambda i, t: (i, t, 1)),
        pl.BlockSpec((1, tile, GLA_DV), lambda i, t: (i, t, 1)),
        pl.BlockSpec((1, tile, GLA_DV), lambda i, t: (i, t, 2)),
        pl.BlockSpec((1, tile // CHUNK, GLA_DK), lambda i, t: (i, t, 0)),
        pl.BlockSpec((GLA_DV, d), const2, pipeline_mode=pl.Buffered(1)),
        pl.BlockSpec(gains.shape, const2),
        pl.BlockSpec((1, tile, d), lambda i, t: (i, t, 0)),
        wn_in,
    ]
    return pl.pallas_call(
        _gla_kernel,
        grid=(b, tiles),
        in_specs=in_specs,
        out_specs=[pl.BlockSpec((1, tile, d), lambda i, t: (i, t, 0)), wn_out],
        out_shape=[jax.ShapeDtypeStruct((b, s, d), F32), wn_shape],
        scratch_shapes=[
            pltpu.VMEM((GLA_HEADS, GLA_DKH, GLA_DVH), F32),
            pltpu.VMEM((tile, GLA_DV), BF16),
        ],
        compiler_params=pltpu.CompilerParams(
            dimension_semantics=("arbitrary", "arbitrary"), vmem_limit_bytes=VMEM_LIMIT_LARGE),
        name="gla_scan_outproj",
    )(proj3, proj3, proj3, proj3, dec3, w_out, gains, x3, w_next)


def _sgu_outproj_kernel(a_ref, w_ref, gain_ref, x_ref, o_ref):
    y = jnp.dot(a_ref[...], w_ref[...], preferred_element_type=F32)
    o_ref[...] = x_ref[...] + _rms(y, gain_ref[SGU_LAYER:SGU_LAYER + 1, :])


def _sgu_outproj(a2, w, gains, x2):
    t, d = x2.shape
    k = a2.shape[1]
    tm = OUT_TM
    return pl.pallas_call(
        _sgu_outproj_kernel,
        grid=(t // tm,),
        in_specs=[
            pl.BlockSpec((tm, k), lambda i: (i, 0)),
            pl.BlockSpec((k, d), lambda i: (0, 0), pipeline_mode=pl.Buffered(1)),
            pl.BlockSpec(gains.shape, lambda i: (0, 0)),
            pl.BlockSpec((tm, d), lambda i: (i, 0)),
        ],
        out_specs=pl.BlockSpec((tm, d), lambda i: (i, 0)),
        out_shape=jax.ShapeDtypeStruct((t, d), F32),
        compiler_params=pltpu.CompilerParams(
            dimension_semantics=("arbitrary",), vmem_limit_bytes=VMEM_LIMIT),
        name="sgu_outproj",
    )(a2, w, gains, x2)


def kernel(x, norm_pre, norm_post, gla_w_in, gla_w_gate2, gla_b_gate, gla_o_gain, gla_w_out,
           sgu_w_in, sgu_ln_gain, sgu_ln_bias, sgu_w_spatial, sgu_b_spatial, sgu_w_out):
    b, s, d = x.shape
    t = b * s
    x2 = x.reshape(t, d)

    w_in_t = gla_w_in.reshape(d, -1).T
    proj, dec, gla_w_out_b = _gla_inproj(x2, norm_pre, w_in_t, gla_w_gate2.reshape(GLA_RANK, GLA_DK),
                                         gla_b_gate, gla_o_gain, gla_w_out.reshape(GLA_DV, d))
    x3, sgu_w_in_b = _gla_scan_outproj(
        proj.reshape(b, s, GLA_MAIN), dec.reshape(b, s // CHUNK, GLA_DK),
        gla_w_out_b, norm_post, x, sgu_w_in.reshape(d, 3 * SGU_WIDTH))
    x2 = x3.reshape(t, d)

    z, sgu_w_out_b = _sgu_inproj(
        x2, norm_pre, sgu_w_in_b, sgu_w_out.reshape(SGU_WIDTH, d), sgu_ln_gain, sgu_ln_bias,
        sgu_w_spatial.reshape(SGU_GROUPS, SGU_BLOCK, SGU_BLOCK),
        sgu_b_spatial.reshape(SGU_GROUPS, SGU_BLOCK))
    x2 = _sgu_outproj(z, sgu_w_out_b, norm_post, x2)
    return x2.reshape(b, s, d)
```

```python
import jax
import jax.numpy as jnp
from jax import lax
from jax.experimental import pallas as pl
from jax.experimental.pallas import tpu as pltpu

F32 = jnp.float32
BF16 = jnp.bfloat16

D_MODEL = 2048
EPS = 1e-6
CHUNK = 64

GLA_LAYER = 0
SGU_LAYER = 1

GLA_HEADS = 4
GLA_DK = D_MODEL // 2
GLA_DV = D_MODEL
GLA_DKH = GLA_DK // GLA_HEADS
GLA_DVH = GLA_DV // GLA_HEADS
GLA_RANK = 16
GLA_INV_TAU = 1.0 / 16.0
GLA_MAIN = 2 * GLA_DK + 2 * GLA_DV

SGU_WIDTH = D_MODEL
SGU_BLOCK = 128
SGU_GROUPS = 8
SGU_GD = SGU_WIDTH // SGU_GROUPS

LANES = 128
VMEM_LIMIT = 56 * 1024 * 1024
VMEM_LIMIT_LARGE = 60 * 1024 * 1024

INPROJ_TM = 512
INPROJ_SLAB = 512
OUT_TM = 1024
LN_ROWS = 16
GLA_TILE = 512
GLA_TRI = 256
W_CHUNK = 128

GELU_C1 = (2.0 / 3.141592653589793) ** 0.5
GELU_C3 = GELU_C1 * 0.044715


def _rms(x, gain):
    return x * lax.rsqrt(jnp.mean(x * x, axis=-1, keepdims=True) + EPS) * gain


def _gelu(r):
    return (0.5 * r) * (1.0 + jnp.tanh(r * (GELU_C1 + GELU_C3 * (r * r))))


def _silu(r):
    hr = 0.5 * r
    return hr * (1.0 + jnp.tanh(hr))


def _dot_nt(a, b_t):
    return lax.dot_general(a, b_t, (((1,), (1,)), ((), ())), preferred_element_type=F32)


def _cast_specs(w_next, index_map, steps):
    rows, cols = w_next.shape
    spec = pl.BlockSpec((rows // steps, cols), index_map)
    return spec, spec, jax.ShapeDtypeStruct((rows, cols), BF16)


def _gla_inproj_kernel(x_ref, gain_ref, w_hbm, wlr_ref, w2_ref, bg_ref, og_ref, wn_ref,
                       o_ref, dec_ref, wnb_ref, w_ref, stage_ref, e_ref, sem):
    per_slab = INPROJ_SLAB // W_CHUNK
    q_slabs = GLA_DK // INPROJ_SLAB
    v_slabs = GLA_DV // INPROJ_SLAB
    q_ids = list(range(q_slabs))
    k_ids = list(range(q_slabs, 2 * q_slabs))
    v_ids = list(range(2 * q_slabs, 2 * q_slabs + v_slabs))
    g_ids = list(range(2 * q_slabs + v_slabs, 2 * q_slabs + 2 * v_slabs))
    order = q_ids + v_ids[:-1] + k_ids + v_ids[-1:] + g_ids
    n_chunks = len(order) * per_slab

    def chunk_copy(s):
        row0 = order[s // per_slab] * INPROJ_SLAB + (s % per_slab) * W_CHUNK
        slot = s % per_slab
        return pltpu.make_async_copy(w_hbm.at[pl.ds(row0, W_CHUNK), :],
                                     stage_ref.at[slot], sem.at[slot])

    def body(stream_weight):
        wnb_ref[...] = wn_ref[...].astype(BF16)
        h = _rms(x_ref[...], gain_ref[GLA_LAYER:GLA_LAYER + 1, :]).astype(BF16)
        tm = h.shape[0]

        def proj(p):
            n = order[p]
            if stream_weight:
                for s in range(p * per_slab, (p + 1) * per_slab):
                    row0 = n * INPROJ_SLAB + (s % per_slab) * W_CHUNK
                    chunk_copy(s).wait()
                    w_ref[row0:row0 + W_CHUNK, :] = stage_ref[s % per_slab].astype(BF16)
                    if s + per_slab < n_chunks:
                        chunk_copy(s + per_slab).start()
            return _dot_nt(h, w_ref[n * INPROJ_SLAB:(n + 1) * INPROJ_SLAB, :])

        def store(n, r):
            o_ref[:, n * INPROJ_SLAB:(n + 1) * INPROJ_SLAB] = r.astype(o_ref.dtype)

        def chunk_decays(r, hi, lo):
            rs = slice(r * GLA_TRI, (r + 1) * GLA_TRI)
            bcum = (jnp.dot(tri, hi[rs], preferred_element_type=F32)
                    + jnp.dot(tri, lo[rs], preferred_element_type=F32))
            for cc in range(GLA_TRI // CHUNK):
                c = r * (GLA_TRI // CHUNK) + cc
                bc = bcum[cc * CHUNK:(cc + 1) * CHUNK]
                b_end = bc[CHUNK - 1:CHUNK, :]
                e_ref[c * CHUNK:(c + 1) * CHUNK, :] = jnp.exp(b_end - bc).astype(BF16)
                dec_ref[c:c + 1, :] = jnp.exp(b_end)

        ri = lax.broadcasted_iota(jnp.int32, (GLA_TRI, GLA_TRI), 0)
        ci = lax.broadcasted_iota(jnp.int32, (GLA_TRI, GLA_TRI), 1)
        tri = jnp.where((ri >= ci) & (ri // CHUNK == ci // CHUNK), 1.0, 0.0).astype(BF16)

        p = 0
        lr = _dot_nt(h, wlr_ref[...].astype(BF16))
        store(order[p], proj(p)); p += 1
        z = jnp.dot(lr.astype(BF16), w2_ref[...].astype(BF16),
                    preferred_element_type=F32) + bg_ref[...]
        for _ in range(1, q_slabs):
            store(order[p], proj(p)); p += 1
        la = (jnp.minimum(z, 0.0) - jnp.log(1.0 + jnp.exp(-jnp.abs(z)))) * GLA_INV_TAU
        hi = la.astype(BF16)
        lo = (la - hi.astype(F32)).astype(BF16)
        for r in range(tm // GLA_TRI):
            store(order[p], proj(p)); p += 1
            chunk_decays(r, hi, lo)
        while order[p] not in k_ids:
            store(order[p], proj(p)); p += 1
        for j in range(q_slabs):
            kc = slice(j * INPROJ_SLAB, (j + 1) * INPROJ_SLAB)
            store(order[p], proj(p) * e_ref[:, kc].astype(F32)); p += 1
        while p < len(order):
            r = proj(p)
            if order[p] in g_ids:
                gc = slice((order[p] - g_ids[0]) * INPROJ_SLAB, (order[p] - g_ids[0] + 1) * INPROJ_SLAB)
                r = _silu(r) * og_ref[:, gc]
            store(order[p], r); p += 1

    @pl.when(pl.program_id(0) == 0)
    def _():
        for s in range(per_slab):
            chunk_copy(s).start()
        body(True)

    @pl.when(pl.program_id(0) > 0)
    def _():
        body(False)


def _gla_inproj(x2, gains, w_t, w2, b_gate, o_gain, w_next):
    t, d = x2.shape
    n = GLA_MAIN
    tm = INPROJ_TM
    steps = t // tm
    resident = pl.Buffered(1)
    const2 = lambda i: (0, 0)
    wn_in, wn_out, wn_shape = _cast_specs(w_next, lambda i: (i, 0), steps)
    return pl.pallas_call(
        _gla_inproj_kernel,
        grid=(steps,),
        in_specs=[
            pl.BlockSpec((tm, d), lambda i: (i, 0)),
            pl.BlockSpec(gains.shape, const2),
            pl.BlockSpec(memory_space=pl.ANY),
            pl.BlockSpec((GLA_RANK, d), lambda i: (GLA_MAIN // GLA_RANK, 0), pipeline_mode=resident),
            pl.BlockSpec((GLA_RANK, GLA_DK), const2, pipeline_mode=resident),
            pl.BlockSpec((1, GLA_DK), const2),
            pl.BlockSpec((1, GLA_DV), const2),
            wn_in,
        ],
        out_specs=[pl.BlockSpec((tm, n), lambda i: (i, 0)),
                   pl.BlockSpec((tm // CHUNK, GLA_DK), lambda i: (i, 0)), wn_out],
        out_shape=[jax.ShapeDtypeStruct((t, n), BF16),
                   jax.ShapeDtypeStruct((t // CHUNK, GLA_DK), F32), wn_shape],
        scratch_shapes=[
            pltpu.VMEM((n, d), BF16),
            pltpu.VMEM((INPROJ_SLAB // W_CHUNK, W_CHUNK, d), F32),
            pltpu.VMEM((tm, GLA_DK), BF16),
            pltpu.SemaphoreType.DMA((INPROJ_SLAB // W_CHUNK,)),
        ],
        compiler_params=pltpu.CompilerParams(
            dimension_semantics=("arbitrary",), vmem_limit_bytes=VMEM_LIMIT_LARGE),
        name="gla_inproj",
    )(x2, gains, w_t, w_t, w2, b_gate, o_gain, w_next)


def _sgu_inproj_kernel(x_ref, gain_ref, w_ref, wn_ref, lng_ref, lnb_ref, ws_ref, b_ref,
                       z_ref, wnb_ref, wsm_ref, bs_ref, va_ref, vn_ref, tg_ref):
    @pl.when(pl.program_id(0) == 0)
    def _():
        ri = lax.broadcasted_iota(jnp.int32, (SGU_BLOCK, SGU_BLOCK), 0) // CHUNK
        ci = lax.broadcasted_iota(jnp.int32, (SGU_BLOCK, SGU_BLOCK), 1) // CHUNK
        for gi in range(SGU_GROUPS):
            wsm_ref[gi] = jnp.where(ri >= ci, ws_ref[gi], 0.0).astype(BF16)
            col = jnp.transpose(jnp.broadcast_to(b_ref[gi:gi + 1, :], (SGU_BLOCK, SGU_BLOCK)))
            for t in range(SGU_GD // LANES):
                c0 = gi * SGU_GD + t * LANES
                bs_ref[:, c0:c0 + LANES] = col

    wnb_ref[...] = wn_ref[...].astype(BF16)
    h = _rms(x_ref[...], gain_ref[SGU_LAYER:SGU_LAYER + 1, :]).astype(BF16)
    tm = h.shape[0]
    slabs = SGU_WIDTH // INPROJ_SLAB

    def proj(n):
        return jnp.dot(h, w_ref[n], preferred_element_type=F32)

    for n in range(slabs):
        va_ref[:, n * INPROJ_SLAB:(n + 1) * INPROJ_SLAB] = _gelu(proj(slabs + n)).astype(BF16)
    for r in range(tm // LN_ROWS):
        rows = slice(r * LN_ROWS, (r + 1) * LN_ROWS)
        v = va_ref[rows, :].astype(F32)
        vc = v - jnp.mean(v, axis=-1, keepdims=True)
        var = jnp.mean(vc * vc, axis=-1, keepdims=True)
        vn_ref[rows, :] = (vc * lax.rsqrt(var + EPS) * lng_ref[...] + lnb_ref[...]).astype(BF16)
    for n in range(slabs):
        tg_ref[:, n * INPROJ_SLAB:(n + 1) * INPROJ_SLAB] = _silu(proj(2 * slabs + n)).astype(BF16)
    for nb in range(tm // SGU_BLOCK):
        rows = slice(nb * SGU_BLOCK, (nb + 1) * SGU_BLOCK)
        for gi in range(SGU_GROUPS):
            cols = slice(gi * SGU_GD, (gi + 1) * SGU_GD)
            vs = (jnp.dot(wsm_ref[gi], vn_ref[rows, cols], preferred_element_type=F32)
                  + bs_ref[:, cols])
            tg_ref[rows, cols] = (vs * tg_ref[rows, cols].astype(F32)).astype(BF16)
    for n in range(slabs):
        cols = slice(n * INPROJ_SLAB, (n + 1) * INPROJ_SLAB)
        z_ref[:, cols] = (_gelu(proj(n)) * tg_ref[:, cols].astype(F32)).astype(BF16)


def _sgu_inproj(x2, gains, w, w_next, ln_gain, ln_bias, w_spatial, b_spatial):
    t, d = x2.shape
    tm = INPROJ_TM
    steps = t // tm
    resident = pl.Buffered(1)
    const2 = lambda i: (0, 0)
    wn_in, wn_out, wn_shape = _cast_specs(w_next, lambda i: (i, 0), steps)
    return pl.pallas_call(
        _sgu_inproj_kernel,
        grid=(steps,),
        in_specs=[
            pl.BlockSpec((tm, d), lambda i: (i, 0)),
            pl.BlockSpec(gains.shape, const2),
            pl.BlockSpec(w.shape, lambda i: (0, 0, 0), pipeline_mode=resident),
            wn_in,
            pl.BlockSpec((1, SGU_WIDTH), const2),
            pl.BlockSpec((1, SGU_WIDTH), const2),
            pl.BlockSpec((SGU_GROUPS, SGU_BLOCK, SGU_BLOCK), lambda i: (0, 0, 0),
                         pipeline_mode=resident),
            pl.BlockSpec((SGU_GROUPS, SGU_BLOCK), const2, pipeline_mode=resident),
        ],
        out_specs=[pl.BlockSpec((tm, SGU_WIDTH), lambda i: (i, 0)), wn_out],
        out_shape=[jax.ShapeDtypeStruct((t, SGU_WIDTH), BF16), wn_shape],
        scratch_shapes=[
            pltpu.VMEM((SGU_GROUPS, SGU_BLOCK, SGU_BLOCK), BF16),
            pltpu.VMEM((SGU_BLOCK, SGU_WIDTH), F32),
            pltpu.VMEM((tm, SGU_WIDTH), BF16),
            pltpu.VMEM((tm, SGU_WIDTH), BF16),
            pltpu.VMEM((tm, SGU_WIDTH), BF16),
        ],
        compiler_params=pltpu.CompilerParams(
            dimension_semantics=("arbitrary",), vmem_limit_bytes=VMEM_LIMIT),
        name="sgu_inproj_mix",
    )(x2, gains, w, w_next, ln_gain, ln_bias, w_spatial, b_spatial)


def _gla_kernel(q_ref, k_ref, v_ref, g_ref, dec_ref, wo_ref, gain_ref, x_ref, wn_ref,
                o_ref, wnb_ref, s_ref, a_ref):
    @pl.when(pl.program_id(1) == 0)
    def _():
        s_ref[...] = jnp.zeros_like(s_ref)

    for n in range(wnb_ref.shape[0]):
        wnb_ref[n] = wn_ref[:, n * INPROJ_SLAB:(n + 1) * INPROJ_SLAB].astype(BF16)
    n_chunks = q_ref.shape[1] // CHUNK

    def chunk(c, carry):
        r0 = pl.multiple_of(c * CHUNK, CHUNK)
        rows = pl.ds(r0, CHUNK)
        dec = dec_ref[0, pl.ds(c, 1), :]
        for h in range(GLA_HEADS):
            kc = slice(h * GLA_DKH, (h + 1) * GLA_DKH)
            vc = slice(h * GLA_DVH, (h + 1) * GLA_DVH)
            upd = lax.dot_general(k_ref[0, rows, kc], v_ref[0, rows, vc], (((0,), (0,)), ((), ())),
                                  preferred_element_type=F32)
            decay_col = jnp.transpose(jnp.broadcast_to(dec[:, kc], (LANES, GLA_DKH)))
            for t in range(GLA_DVH // LANES):
                cols = slice(t * LANES, (t + 1) * LANES)
                s_new = s_ref[h, :, cols] * decay_col + upd[:, cols]
                s_ref[h, :, cols] = s_new
        for h in range(GLA_HEADS):
            kc = slice(h * GLA_DKH, (h + 1) * GLA_DKH)
            vc = slice(h * GLA_DVH, (h + 1) * GLA_DVH)
            o = jnp.dot(q_ref[0, rows, kc], s_ref[h].astype(BF16),
                        preferred_element_type=F32)
            ms = jnp.mean(o * o, axis=-1, keepdims=True) * (1.0 / GLA_DKH)
            scale = lax.rsqrt(ms + EPS) * (GLA_DKH ** -0.5)
            a_ref[rows, vc] = (o * scale * g_ref[0, rows, vc].astype(F32)).astype(a_ref.dtype)
        return carry

    lax.fori_loop(0, n_chunks, chunk, 0, unroll=True)

    y = jnp.dot(a_ref[...], wo_ref[...], preferred_element_type=F32)
    o_ref[0] = x_ref[0] + _rms(y, gain_ref[GLA_LAYER:GLA_LAYER + 1, :])


def _gla_scan_outproj(proj3, dec3, w_out, gains, x3, w_next):
    b, s, d = x3.shape
    tile = GLA_TILE
    tiles = s // tile
    const2 = lambda i, t: (0, 0)
    w_rows, w_cols = w_next.shape
    wn_rows = w_rows // (b * tiles)
    wn_in = pl.BlockSpec((wn_rows, w_cols), lambda i, t: (i * tiles + t, 0))
    wn_out = pl.BlockSpec((w_cols // INPROJ_SLAB, wn_rows, INPROJ_SLAB),
                          lambda i, t: (0, i * tiles + t, 0))
    wn_shape = jax.ShapeDtypeStruct((w_cols // INPROJ_SLAB, w_rows, INPROJ_SLAB), BF16)
    in_specs = [
        pl.BlockSpec((1, tile, GLA_DK), lambda i, t: (i, t, 0)),
        pl.BlockSpec((1, tile, GLA_DK), lambda i, t: (i, t, 1)),
        pl.BlockSpec((1, tile, GLA_DV), lambda i, t: (i, t, 1)),
        pl.BlockSpec((1, tile, GLA_DV), lambda i, t: (i, t, 2)),
        pl.BlockSpec((1, tile // CHUNK, GLA_DK), lambda i, t: (i, t, 0)),
        pl.BlockSpec((GLA_DV, d), const2, pipeline_mode=pl.Buffered(1)),
        pl.BlockSpec(gains.shape, const2),
        pl.BlockSpec((1, tile, d), lambda i, t: (i, t, 0)),
        wn_in,
    ]
    return pl.pallas_call(
        _gla_kernel,
        grid=(b, tiles),
        in_specs=in_specs,
        out_specs=[pl.BlockSpec((1, tile, d), lambda i, t: (i, t, 0)), wn_out],
        out_shape=[jax.ShapeDtypeStruct((b, s, d), F32), wn_shape],
        scratch_shapes=[
            pltpu.VMEM((GLA_HEADS, GLA_DKH, GLA_DVH), F32),
            pltpu.VMEM((tile, GLA_DV), BF16),
        ],
        compiler_params=pltpu.CompilerParams(
            dimension_semantics=("arbitrary", "arbitrary"), vmem_limit_bytes=VMEM_LIMIT_LARGE),
        name="gla_scan_outproj",
    )(proj3, proj3, proj3, proj3, dec3, w_out, gains, x3, w_next)


def _sgu_outproj_kernel(a_ref, w_ref, gain_ref, x_ref, o_ref):
    y = jnp.dot(a_ref[...], w_ref[...], preferred_element_type=F32)
    o_ref[...] = x_ref[...] + _rms(y, gain_ref[SGU_LAYER:SGU_LAYER + 1, :])


def _sgu_outproj(a2, w, gains, x2):
    t, d = x2.shape
    k = a2.shape[1]
    tm = OUT_TM
    return pl.pallas_call(
        _sgu_outproj_kernel,
        grid=(t // tm,),
        in_specs=[
            pl.BlockSpec((tm, k), lambda i: (i, 0)),
            pl.BlockSpec((k, d), lambda i: (0, 0), pipeline_mode=pl.Buffered(1)),
            pl.BlockSpec(gains.shape, lambda i: (0, 0)),
            pl.BlockSpec((tm, d), lambda i: (i, 0)),
        ],
        out_specs=pl.BlockSpec((tm, d), lambda i: (i, 0)),
        out_shape=jax.ShapeDtypeStruct((t, d), F32),
        compiler_params=pltpu.CompilerParams(
            dimension_semantics=("arbitrary",), vmem_limit_bytes=VMEM_LIMIT_LARGE),
        name="sgu_outproj",
    )(a2, w, gains, x2)


def kernel(x, norm_pre, norm_post, gla_w_in, gla_w_gate2, gla_b_gate, gla_o_gain, gla_w_out,
           sgu_w_in, sgu_ln_gain, sgu_ln_bias, sgu_w_spatial, sgu_b_spatial, sgu_w_out):
    b, s, d = x.shape
    t = b * s
    x2 = x.reshape(t, d)

    w_in_t = gla_w_in.reshape(d, -1).T
    proj, dec, gla_w_out_b = _gla_inproj(x2, norm_pre, w_in_t, gla_w_gate2.reshape(GLA_RANK, GLA_DK),
                                         gla_b_gate, gla_o_gain, gla_w_out.reshape(GLA_DV, d))
    x3, sgu_w_in_b = _gla_scan_outproj(
        proj.reshape(b, s, GLA_MAIN), dec.reshape(b, s // CHUNK, GLA_DK),
        gla_w_out_b, norm_post, x, sgu_w_in.reshape(d, 3 * SGU_WIDTH))
    x2 = x3.reshape(t, d)

    z, sgu_w_out_b = _sgu_inproj(
        x2, norm_pre, sgu_w_in_b, sgu_w_out.reshape(SGU_WIDTH, d), sgu_ln_gain, sgu_ln_bias,
        sgu_w_spatial.reshape(SGU_GROUPS, SGU_BLOCK, SGU_BLOCK),
        sgu_b_spatial.reshape(SGU_GROUPS, SGU_BLOCK))
    x2 = _sgu_outproj(z, sgu_w_out_b, norm_post, x2)
    return x2.reshape(b, s, d)
```

```python
import jax
import jax.numpy as jnp
from jax import lax
from jax.experimental import pallas as pl
from jax.experimental.pallas import tpu as pltpu

F32 = jnp.float32
BF16 = jnp.bfloat16

D_MODEL = 2048
EPS = 1e-6
CHUNK = 64

GLA_LAYER = 0
SGU_LAYER = 1

GLA_HEADS = 4
GLA_DK = D_MODEL // 2
GLA_DV = D_MODEL
GLA_DKH = GLA_DK // GLA_HEADS
GLA_DVH = GLA_DV // GLA_HEADS
GLA_RANK = 16
GLA_INV_TAU = 1.0 / 16.0
GLA_MAIN = 2 * GLA_DK + 2 * GLA_DV

SGU_WIDTH = D_MODEL
SGU_BLOCK = 128
SGU_GROUPS = 8
SGU_GD = SGU_WIDTH // SGU_GROUPS

LANES = 128
VMEM_LIMIT = 56 * 1024 * 1024
VMEM_LIMIT_LARGE = 60 * 1024 * 1024

INPROJ_TM = 512
INPROJ_SLAB = 512
OUT_TM = 512
LN_ROWS = 16
GLA_TILE = 512
GLA_TRI = 256
W_CHUNK = 128

GELU_C1 = (2.0 / 3.141592653589793) ** 0.5
GELU_C3 = GELU_C1 * 0.044715


def _rms(x, gain):
    return x * lax.rsqrt(jnp.mean(x * x, axis=-1, keepdims=True) + EPS) * gain


def _gelu(r):
    return (0.5 * r) * (1.0 + jnp.tanh(r * (GELU_C1 + GELU_C3 * (r * r))))


def _silu(r):
    hr = 0.5 * r
    return hr * (1.0 + jnp.tanh(hr))


def _dot_nt(a, b_t):
    return lax.dot_general(a, b_t, (((1,), (1,)), ((), ())), preferred_element_type=F32)


def _cast_specs(w_next, index_map, steps):
    rows, cols = w_next.shape
    spec = pl.BlockSpec((rows // steps, cols), index_map)
    return spec, spec, jax.ShapeDtypeStruct((rows, cols), BF16)


def _gla_inproj_kernel(x_ref, gain_ref, w_hbm, wlr_ref, w2_ref, bg_ref, og_ref, wn_ref,
                       o_ref, dec_ref, wnb_ref, w_ref, stage_ref, e_ref, sem):
    per_slab = INPROJ_SLAB // W_CHUNK
    q_slabs = GLA_DK // INPROJ_SLAB
    v_slabs = GLA_DV // INPROJ_SLAB
    q_ids = list(range(q_slabs))
    k_ids = list(range(q_slabs, 2 * q_slabs))
    v_ids = list(range(2 * q_slabs, 2 * q_slabs + v_slabs))
    g_ids = list(range(2 * q_slabs + v_slabs, 2 * q_slabs + 2 * v_slabs))
    order = q_ids + v_ids[:-1] + k_ids + v_ids[-1:] + g_ids
    n_chunks = len(order) * per_slab

    def chunk_copy(s):
        row0 = order[s // per_slab] * INPROJ_SLAB + (s % per_slab) * W_CHUNK
        slot = s % per_slab
        return pltpu.make_async_copy(w_hbm.at[pl.ds(row0, W_CHUNK), :],
                                     stage_ref.at[slot], sem.at[slot])

    def body(stream_weight):
        wnb_ref[...] = wn_ref[...].astype(BF16)
        h = _rms(x_ref[...], gain_ref[GLA_LAYER:GLA_LAYER + 1, :]).astype(BF16)
        tm = h.shape[0]

        def proj(p):
            n = order[p]
            if stream_weight:
                for s in range(p * per_slab, (p + 1) * per_slab):
                    row0 = n * INPROJ_SLAB + (s % per_slab) * W_CHUNK
                    chunk_copy(s).wait()
                    w_ref[row0:row0 + W_CHUNK, :] = stage_ref[s % per_slab].astype(BF16)
                    if s + per_slab < n_chunks:
                        chunk_copy(s + per_slab).start()
            return _dot_nt(h, w_ref[n * INPROJ_SLAB:(n + 1) * INPROJ_SLAB, :])

        def store(n, r):
            o_ref[:, n * INPROJ_SLAB:(n + 1) * INPROJ_SLAB] = r.astype(o_ref.dtype)

        def chunk_decays(r, hi, lo):
            rs = slice(r * GLA_TRI, (r + 1) * GLA_TRI)
            bcum = (jnp.dot(tri, hi[rs], preferred_element_type=F32)
                    + jnp.dot(tri, lo[rs], preferred_element_type=F32))
            for cc in range(GLA_TRI // CHUNK):
                c = r * (GLA_TRI // CHUNK) + cc
                bc = bcum[cc * CHUNK:(cc + 1) * CHUNK]
                b_end = bc[CHUNK - 1:CHUNK, :]
                e_ref[c * CHUNK:(c + 1) * CHUNK, :] = jnp.exp(b_end - bc).astype(BF16)
                dec_ref[c:c + 1, :] = jnp.exp(b_end)

        ri = lax.broadcasted_iota(jnp.int32, (GLA_TRI, GLA_TRI), 0)
        ci = lax.broadcasted_iota(jnp.int32, (GLA_TRI, GLA_TRI), 1)
        tri = jnp.where((ri >= ci) & (ri // CHUNK == ci // CHUNK), 1.0, 0.0).astype(BF16)

        p = 0
        lr = _dot_nt(h, wlr_ref[...].astype(BF16))
        store(order[p], proj(p)); p += 1
        z = jnp.dot(lr.astype(BF16), w2_ref[...].astype(BF16),
                    preferred_element_type=F32) + bg_ref[...]
        for _ in range(1, q_slabs):
            store(order[p], proj(p)); p += 1
        la = (jnp.minimum(z, 0.0) - jnp.log(1.0 + jnp.exp(-jnp.abs(z)))) * GLA_INV_TAU
        hi = la.astype(BF16)
        lo = (la - hi.astype(F32)).astype(BF16)
        for r in range(tm // GLA_TRI):
            store(order[p], proj(p)); p += 1
            chunk_decays(r, hi, lo)
        while order[p] not in k_ids:
            store(order[p], proj(p)); p += 1
        for j in range(q_slabs):
            kc = slice(j * INPROJ_SLAB, (j + 1) * INPROJ_SLAB)
            store(order[p], proj(p) * e_ref[:, kc].astype(F32)); p += 1
        while p < len(order):
            r = proj(p)
            if order[p] in g_ids:
                gc = slice((order[p] - g_ids[0]) * INPROJ_SLAB, (order[p] - g_ids[0] + 1) * INPROJ_SLAB)
                r = _silu(r) * og_ref[:, gc]
            store(order[p], r); p += 1

    @pl.when(pl.program_id(0) == 0)
    def _():
        for s in range(per_slab):
            chunk_copy(s).start()
        body(True)

    @pl.when(pl.program_id(0) > 0)
    def _():
        body(False)


def _gla_inproj(x2, gains, w_t, w2, b_gate, o_gain, w_next):
    t, d = x2.shape
    n = GLA_MAIN
    tm = INPROJ_TM
    steps = t // tm
    resident = pl.Buffered(1)
    const2 = lambda i: (0, 0)
    wn_in, wn_out, wn_shape = _cast_specs(w_next, lambda i: (i, 0), steps)
    return pl.pallas_call(
        _gla_inproj_kernel,
        grid=(steps,),
        in_specs=[
            pl.BlockSpec((tm, d), lambda i: (i, 0)),
            pl.BlockSpec(gains.shape, const2),
            pl.BlockSpec(memory_space=pl.ANY),
            pl.BlockSpec((GLA_RANK, d), lambda i: (GLA_MAIN // GLA_RANK, 0), pipeline_mode=resident),
            pl.BlockSpec((GLA_RANK, GLA_DK), const2, pipeline_mode=resident),
            pl.BlockSpec((1, GLA_DK), const2),
            pl.BlockSpec((1, GLA_DV), const2),
            wn_in,
        ],
        out_specs=[pl.BlockSpec((tm, n), lambda i: (i, 0)),
                   pl.BlockSpec((tm // CHUNK, GLA_DK), lambda i: (i, 0)), wn_out],
        out_shape=[jax.ShapeDtypeStruct((t, n), BF16),
                   jax.ShapeDtypeStruct((t // CHUNK, GLA_DK), F32), wn_shape],
        scratch_shapes=[
            pltpu.VMEM((n, d), BF16),
            pltpu.VMEM((INPROJ_SLAB // W_CHUNK, W_CHUNK, d), F32),
            pltpu.VMEM((tm, GLA_DK), BF16),
            pltpu.SemaphoreType.DMA((INPROJ_SLAB // W_CHUNK,)),
        ],
        compiler_params=pltpu.CompilerParams(
            dimension_semantics=("arbitrary",), vmem_limit_bytes=VMEM_LIMIT_LARGE),
        name="gla_inproj",
    )(x2, gains, w_t, w_t, w2, b_gate, o_gain, w_next)


def _sgu_inproj_kernel(x_ref, gain_ref, w_ref, wn_ref, lng_ref, lnb_ref, ws_ref, b_ref,
                       z_ref, wnb_ref, wsm_ref, bs_ref, va_ref, vn_ref, tg_ref):
    @pl.when(pl.program_id(0) == 0)
    def _():
        ri = lax.broadcasted_iota(jnp.int32, (SGU_BLOCK, SGU_BLOCK), 0) // CHUNK
        ci = lax.broadcasted_iota(jnp.int32, (SGU_BLOCK, SGU_BLOCK), 1) // CHUNK
        for gi in range(SGU_GROUPS):
            wsm_ref[gi] = jnp.where(ri >= ci, ws_ref[gi], 0.0).astype(BF16)
            col = jnp.transpose(jnp.broadcast_to(b_ref[gi:gi + 1, :], (SGU_BLOCK, SGU_BLOCK)))
            for t in range(SGU_GD // LANES):
                c0 = gi * SGU_GD + t * LANES
                bs_ref[:, c0:c0 + LANES] = col

    wnb_ref[...] = wn_ref[...].astype(BF16)
    h = _rms(x_ref[...], gain_ref[SGU_LAYER:SGU_LAYER + 1, :]).astype(BF16)
    tm = h.shape[0]
    slabs = SGU_WIDTH // INPROJ_SLAB

    def proj(n):
        return jnp.dot(h, w_ref[n], preferred_element_type=F32)

    for n in range(slabs):
        va_ref[:, n * INPROJ_SLAB:(n + 1) * INPROJ_SLAB] = _gelu(proj(slabs + n)).astype(BF16)
    for r in range(tm // LN_ROWS):
        rows = slice(r * LN_ROWS, (r + 1) * LN_ROWS)
        v = va_ref[rows, :].astype(F32)
        vc = v - jnp.mean(v, axis=-1, keepdims=True)
        var = jnp.mean(vc * vc, axis=-1, keepdims=True)
        vn_ref[rows, :] = (vc * lax.rsqrt(var + EPS) * lng_ref[...] + lnb_ref[...]).astype(BF16)
    for n in range(slabs):
        tg_ref[:, n * INPROJ_SLAB:(n + 1) * INPROJ_SLAB] = _gelu(proj(n)).astype(BF16)
    for n in range(slabs):
        cols = slice(n * INPROJ_SLAB, (n + 1) * INPROJ_SLAB)
        tg_ref[:, cols] = (_silu(proj(2 * slabs + n)) * tg_ref[:, cols].astype(F32)).astype(BF16)
    for nb in range(tm // SGU_BLOCK):
        rows = slice(nb * SGU_BLOCK, (nb + 1) * SGU_BLOCK)
        for gi in range(SGU_GROUPS):
            cols = slice(gi * SGU_GD, (gi + 1) * SGU_GD)
            vs = (jnp.dot(wsm_ref[gi], vn_ref[rows, cols], preferred_element_type=F32)
                  + bs_ref[:, cols])
            z_ref[rows, cols] = (vs * tg_ref[rows, cols].astype(F32)).astype(BF16)


def _sgu_inproj(x2, gains, w, w_next, ln_gain, ln_bias, w_spatial, b_spatial):
    t, d = x2.shape
    tm = INPROJ_TM
    steps = t // tm
    resident = pl.Buffered(1)
    const2 = lambda i: (0, 0)
    wn_in, wn_out, wn_shape = _cast_specs(w_next, lambda i: (i, 0), steps)
    return pl.pallas_call(
        _sgu_inproj_kernel,
        grid=(steps,),
        in_specs=[
            pl.BlockSpec((tm, d), lambda i: (i, 0)),
            pl.BlockSpec(gains.shape, const2),
            pl.BlockSpec(w.shape, lambda i: (0, 0, 0), pipeline_mode=resident),
            wn_in,
            pl.BlockSpec((1, SGU_WIDTH), const2),
            pl.BlockSpec((1, SGU_WIDTH), const2),
            pl.BlockSpec((SGU_GROUPS, SGU_BLOCK, SGU_BLOCK), lambda i: (0, 0, 0),
                         pipeline_mode=resident),
            pl.BlockSpec((SGU_GROUPS, SGU_BLOCK), const2, pipeline_mode=resident),
        ],
        out_specs=[pl.BlockSpec((tm, SGU_WIDTH), lambda i: (i, 0)), wn_out],
        out_shape=[jax.ShapeDtypeStruct((t, SGU_WIDTH), BF16), wn_shape],
        scratch_shapes=[
            pltpu.VMEM((SGU_GROUPS, SGU_BLOCK, SGU_BLOCK), BF16),
            pltpu.VMEM((SGU_BLOCK, SGU_WIDTH), F32),
            pltpu.VMEM((tm, SGU_WIDTH), BF16),
            pltpu.VMEM((tm, SGU_WIDTH), BF16),
            pltpu.VMEM((tm, SGU_WIDTH), BF16),
        ],
        compiler_params=pltpu.CompilerParams(
            dimension_semantics=("arbitrary",), vmem_limit_bytes=VMEM_LIMIT),
        name="sgu_inproj_mix",
    )(x2, gains, w, w_next, ln_gain, ln_bias, w_spatial, b_spatial)


def _gla_kernel(q_ref, k_ref, v_ref, g_ref, dec_ref, wo_ref, gain_ref, x_ref, wn_ref,
                o_ref, wnb_ref, s_ref, a_ref):
    @pl.when(pl.program_id(1) == 0)
    def _():
        s_ref[...] = jnp.zeros_like(s_ref)

    for n in range(wnb_ref.shape[0]):
        wnb_ref[n] = wn_ref[:, n * INPROJ_SLAB:(n + 1) * INPROJ_SLAB].astype(BF16)
    n_chunks = q_ref.shape[1] // CHUNK

    def chunk(c, carry):
        r0 = pl.multiple_of(c * CHUNK, CHUNK)
        rows = pl.ds(r0, CHUNK)
        dec = dec_ref[0, pl.ds(c, 1), :]
        for h in range(GLA_HEADS):
            kc = slice(h * GLA_DKH, (h + 1) * GLA_DKH)
            vc = slice(h * GLA_DVH, (h + 1) * GLA_DVH)
            upd = lax.dot_general(k_ref[0, rows, kc], v_ref[0, rows, vc], (((0,), (0,)), ((), ())),
                                  preferred_element_type=F32)
            decay_col = jnp.transpose(jnp.broadcast_to(dec[:, kc], (LANES, GLA_DKH)))
            for t in range(GLA_DVH // LANES):
                cols = slice(t * LANES, (t + 1) * LANES)
                s_new = s_ref[h, :, cols] * decay_col + upd[:, cols]
                s_ref[h, :, cols] = s_new
        for h in range(GLA_HEADS):
            kc = slice(h * GLA_DKH, (h + 1) * GLA_DKH)
            vc = slice(h * GLA_DVH, (h + 1) * GLA_DVH)
            o = jnp.dot(q_ref[0, rows, kc], s_ref[h].astype(BF16),
                        preferred_element_type=F32)
            ms = jnp.mean(o * o, axis=-1, keepdims=True) * (1.0 / GLA_DKH)
            scale = lax.rsqrt(ms + EPS) * (GLA_DKH ** -0.5)
            a_ref[rows, vc] = (o * scale * g_ref[0, rows, vc].astype(F32)).astype(a_ref.dtype)
        return carry

    lax.fori_loop(0, n_chunks, chunk, 0, unroll=True)

    y = jnp.dot(a_ref[...], wo_ref[...], preferred_element_type=F32)
    o_ref[0] = x_ref[0] + _rms(y, gain_ref[GLA_LAYER:GLA_LAYER + 1, :])


def _gla_scan_outproj(proj3, dec3, w_out, gains, x3, w_next):
    b, s, d = x3.shape
    tile = GLA_TILE
    tiles = s // tile
    const2 = lambda i, t: (0, 0)
    w_rows, w_cols = w_next.shape
    wn_rows = w_rows // (b * tiles)
    wn_in = pl.BlockSpec((wn_rows, w_cols), lambda i, t: (i * tiles + t, 0))
    wn_out = pl.BlockSpec((w_cols // INPROJ_SLAB, wn_rows, INPROJ_SLAB),
                          lambda i, t: (0, i * tiles + t, 0))
    wn_shape = jax.ShapeDtypeStruct((w_cols // INPROJ_SLAB, w_rows, INPROJ_SLAB), BF16)
    in_specs = [
        pl.BlockSpec((1, tile, GLA_DK), lambda i, t: (i, t, 0)),
        pl.BlockSpec((1, tile, GLA_DK), lambda i, t: (i, t, 1)),
        pl.BlockSpec((1, tile, GLA_DV), lambda i, t: (i, t, 1)),
        pl.BlockSpec((1, tile, GLA_DV), lambda i, t: (i, t, 2)),
        pl.BlockSpec((1, tile // CHUNK, GLA_DK), lambda i, t: (i, t, 0)),
        pl.BlockSpec((GLA_DV, d), const2, pipeline_mode=pl.Buffered(1)),
        pl.BlockSpec(gains.shape, const2),
        pl.BlockSpec((1, tile, d), lambda i, t: (i, t, 0)),
        wn_in,
    ]
    return pl.pallas_call(
        _gla_kernel,
        grid=(b, tiles),
        in_specs=in_specs,
        out_specs=[pl.BlockSpec((1, tile, d), lambda i, t: (i, t, 0)), wn_out],
        out_shape=[jax.ShapeDtypeStruct((b, s, d), F32), wn_shape],
        scratch_shapes=[
            pltpu.VMEM((GLA_HEADS, GLA_DKH, GLA_DVH), F32),
            pltpu.VMEM((tile, GLA_DV), BF16),
        ],
        compiler_params=pltpu.CompilerParams(
            dimension_semantics=("arbitrary", "arbitrary"), vmem_limit_bytes=VMEM_LIMIT_LARGE),
        name="gla_scan_outproj",
    )(proj3, proj3, proj3, proj3, dec3, w_out, gains, x3, w_next)


def _sgu_outproj_kernel(a_ref, w_ref, gain_ref, x_ref, o_ref):
    y = jnp.dot(a_ref[...], w_ref[...], preferred_element_type=F32)
    o_ref[...] = x_ref[...] + _rms(y, gain_ref[SGU_LAYER:SGU_LAYER + 1, :])


def _sgu_outproj(a2, w, gains, x2):
    t, d = x2.shape
    k = a2.shape[1]
    tm = OUT_TM
    return pl.pallas_call(
        _sgu_outproj_kernel,
        grid=(t // tm,),
        in_specs=[
            pl.BlockSpec((tm, k), lambda i: (i, 0)),
            pl.BlockSpec((k, d), lambda i: (0, 0), pipeline_mode=pl.Buffered(1)),
            pl.BlockSpec(gains.shape, lambda i: (0, 0)),
            pl.BlockSpec((tm, d), lambda i: (i, 0)),
        ],
        out_specs=pl.BlockSpec((tm, d), lambda i: (i, 0)),
        out_shape=jax.ShapeDtypeStruct((t, d), F32),
        compiler_params=pltpu.CompilerParams(
            dimension_semantics=("arbitrary",), vmem_limit_bytes=VMEM_LIMIT),
        name="sgu_outproj",
    )(a2, w, gains, x2)


def kernel(x, norm_pre, norm_post, gla_w_in, gla_w_gate2, gla_b_gate, gla_o_gain, gla_w_out,
           sgu_w_in, sgu_ln_gain, sgu_ln_bias, sgu_w_spatial, sgu_b_spatial, sgu_w_out):
    b, s, d = x.shape
    t = b * s
    x2 = x.reshape(t, d)

    w_in_t = gla_w_in.reshape(d, -1).T
    proj, dec, gla_w_out_b = _gla_inproj(x2, norm_pre, w_in_t, gla_w_gate2.reshape(GLA_RANK, GLA_DK),
                                         gla_b_gate, gla_o_gain, gla_w_out.reshape(GLA_DV, d))
    x3, sgu_w_in_b = _gla_scan_outproj(
        proj.reshape(b, s, GLA_MAIN), dec.reshape(b, s // CHUNK, GLA_DK),
        gla_w_out_b, norm_post, x, sgu_w_in.reshape(d, 3 * SGU_WIDTH))
    x2 = x3.reshape(t, d)

    z, sgu_w_out_b = _sgu_inproj(
        x2, norm_pre, sgu_w_in_b, sgu_w_out.reshape(SGU_WIDTH, d), sgu_ln_gain, sgu_ln_bias,
        sgu_w_spatial.reshape(SGU_GROUPS, SGU_BLOCK, SGU_BLOCK),
        sgu_b_spatial.reshape(SGU_GROUPS, SGU_BLOCK))
    x2 = _sgu_outproj(z, sgu_w_out_b, norm_post, x2)
    return x2.reshape(b, s, d)
```

```python
import jax
import jax.numpy as jnp
from jax import lax
from jax.experimental import pallas as pl
from jax.experimental.pallas import tpu as pltpu

F32 = jnp.float32
BF16 = jnp.bfloat16

D_MODEL = 2048
EPS = 1e-6
CHUNK = 64

GLA_LAYER = 0
SGU_LAYER = 1

GLA_HEADS = 4
GLA_DK = D_MODEL // 2
GLA_DV = D_MODEL
GLA_DKH = GLA_DK // GLA_HEADS
GLA_DVH = GLA_DV // GLA_HEADS
GLA_RANK = 16
GLA_INV_TAU = 1.0 / 16.0
GLA_MAIN = 2 * GLA_DK + 2 * GLA_DV

SGU_WIDTH = D_MODEL
SGU_BLOCK = 128
SGU_GROUPS = 8
SGU_GD = SGU_WIDTH // SGU_GROUPS

LANES = 128
VMEM_LIMIT = 56 * 1024 * 1024
VMEM_LIMIT_LARGE = 60 * 1024 * 1024

INPROJ_TM = 512
INPROJ_SLAB = 512
OUT_TM = 512
LN_ROWS = 16
LN_SPREAD = 4
GLA_TILE = 512
GLA_TRI = 256
W_CHUNK = 128

GELU_C1 = (2.0 / 3.141592653589793) ** 0.5
GELU_C3 = GELU_C1 * 0.044715


def _rms(x, gain):
    return x * lax.rsqrt(jnp.mean(x * x, axis=-1, keepdims=True) + EPS) * gain


def _gelu(r):
    return (0.5 * r) * (1.0 + jnp.tanh(r * (GELU_C1 + GELU_C3 * (r * r))))


def _silu(r):
    hr = 0.5 * r
    return hr * (1.0 + jnp.tanh(hr))


def _zero_from(anchor):
    bits = anchor.astype(jnp.int32)
    return lax.shift_right_logical(lax.shift_right_logical(bits, 16), 16).astype(BF16)


def _dot_nt(a, b_t):
    return lax.dot_general(a, b_t, (((1,), (1,)), ((), ())), preferred_element_type=F32)


def _cast_specs(w_next, index_map, steps):
    rows, cols = w_next.shape
    spec = pl.BlockSpec((rows // steps, cols), index_map)
    return spec, spec, jax.ShapeDtypeStruct((rows, cols), BF16)


def _gla_inproj_kernel(x_ref, gain_ref, w_hbm, wlr_ref, w2_ref, bg_ref, og_ref, wn_ref,
                       o_ref, dec_ref, wnb_ref, w_ref, stage_ref, e_ref, sem):
    per_slab = INPROJ_SLAB // W_CHUNK
    q_slabs = GLA_DK // INPROJ_SLAB
    v_slabs = GLA_DV // INPROJ_SLAB
    q_ids = list(range(q_slabs))
    k_ids = list(range(q_slabs, 2 * q_slabs))
    v_ids = list(range(2 * q_slabs, 2 * q_slabs + v_slabs))
    g_ids = list(range(2 * q_slabs + v_slabs, 2 * q_slabs + 2 * v_slabs))
    order = q_ids + v_ids[:-1] + k_ids + v_ids[-1:] + g_ids
    n_chunks = len(order) * per_slab

    def chunk_copy(s):
        row0 = order[s // per_slab] * INPROJ_SLAB + (s % per_slab) * W_CHUNK
        slot = s % per_slab
        return pltpu.make_async_copy(w_hbm.at[pl.ds(row0, W_CHUNK), :],
                                     stage_ref.at[slot], sem.at[slot])

    def body(stream_weight):
        wnb_ref[...] = wn_ref[...].astype(BF16)
        h = _rms(x_ref[...], gain_ref[GLA_LAYER:GLA_LAYER + 1, :]).astype(BF16)
        tm = h.shape[0]

        def proj(p):
            n = order[p]
            if stream_weight:
                for s in range(p * per_slab, (p + 1) * per_slab):
                    row0 = n * INPROJ_SLAB + (s % per_slab) * W_CHUNK
                    chunk_copy(s).wait()
                    w_ref[row0:row0 + W_CHUNK, :] = stage_ref[s % per_slab].astype(BF16)
                    if s + per_slab < n_chunks:
                        chunk_copy(s + per_slab).start()
            return _dot_nt(h, w_ref[n * INPROJ_SLAB:(n + 1) * INPROJ_SLAB, :])

        def store(n, r):
            o_ref[:, n * INPROJ_SLAB:(n + 1) * INPROJ_SLAB] = r.astype(o_ref.dtype)

        def chunk_decays(r, hi, lo):
            rs = slice(r * GLA_TRI, (r + 1) * GLA_TRI)
            bcum = (jnp.dot(tri, hi[rs], preferred_element_type=F32)
                    + jnp.dot(tri, lo[rs], preferred_element_type=F32))
            for cc in range(GLA_TRI // CHUNK):
                c = r * (GLA_TRI // CHUNK) + cc
                bc = bcum[cc * CHUNK:(cc + 1) * CHUNK]
                b_end = bc[CHUNK - 1:CHUNK, :]
                e_ref[c * CHUNK:(c + 1) * CHUNK, :] = jnp.exp(b_end - bc).astype(BF16)
                dec_ref[c:c + 1, :] = jnp.exp(b_end)

        ri = lax.broadcasted_iota(jnp.int32, (GLA_TRI, GLA_TRI), 0)
        ci = lax.broadcasted_iota(jnp.int32, (GLA_TRI, GLA_TRI), 1)
        tri = jnp.where((ri >= ci) & (ri // CHUNK == ci // CHUNK), 1.0, 0.0).astype(BF16)

        p = 0
        lr = _dot_nt(h, wlr_ref[...].astype(BF16))
        store(order[p], proj(p)); p += 1
        z = jnp.dot(lr.astype(BF16), w2_ref[...].astype(BF16),
                    preferred_element_type=F32) + bg_ref[...]
        for _ in range(1, q_slabs):
            store(order[p], proj(p)); p += 1
        la = (jnp.minimum(z, 0.0) - jnp.log(1.0 + jnp.exp(-jnp.abs(z)))) * GLA_INV_TAU
        hi = la.astype(BF16)
        lo = (la - hi.astype(F32)).astype(BF16)
        for r in range(tm // GLA_TRI):
            store(order[p], proj(p)); p += 1
            chunk_decays(r, hi, lo)
        while order[p] not in k_ids:
            store(order[p], proj(p)); p += 1
        for j in range(q_slabs):
            kc = slice(j * INPROJ_SLAB, (j + 1) * INPROJ_SLAB)
            store(order[p], proj(p) * e_ref[:, kc].astype(F32)); p += 1
        while p < len(order):
            r = proj(p)
            if order[p] in g_ids:
                gc = slice((order[p] - g_ids[0]) * INPROJ_SLAB, (order[p] - g_ids[0] + 1) * INPROJ_SLAB)
                r = _silu(r) * og_ref[:, gc]
            store(order[p], r); p += 1

    @pl.when(pl.program_id(0) == 0)
    def _():
        for s in range(per_slab):
            chunk_copy(s).start()
        body(True)

    @pl.when(pl.program_id(0) > 0)
    def _():
        body(False)


def _gla_inproj(x2, gains, w_t, w2, b_gate, o_gain, w_next):
    t, d = x2.shape
    n = GLA_MAIN
    tm = INPROJ_TM
    steps = t // tm
    resident = pl.Buffered(1)
    const2 = lambda i: (0, 0)
    wn_in, wn_out, wn_shape = _cast_specs(w_next, lambda i: (i, 0), steps)
    return pl.pallas_call(
        _gla_inproj_kernel,
        grid=(steps,),
        in_specs=[
            pl.BlockSpec((tm, d), lambda i: (i, 0)),
            pl.BlockSpec(gains.shape, const2),
            pl.BlockSpec(memory_space=pl.ANY),
            pl.BlockSpec((GLA_RANK, d), lambda i: (GLA_MAIN // GLA_RANK, 0), pipeline_mode=resident),
            pl.BlockSpec((GLA_RANK, GLA_DK), const2, pipeline_mode=resident),
            pl.BlockSpec((1, GLA_DK), const2),
            pl.BlockSpec((1, GLA_DV), const2),
            wn_in,
        ],
        out_specs=[pl.BlockSpec((tm, n), lambda i: (i, 0)),
                   pl.BlockSpec((tm // CHUNK, GLA_DK), lambda i: (i, 0)), wn_out],
        out_shape=[jax.ShapeDtypeStruct((t, n), BF16),
                   jax.ShapeDtypeStruct((t // CHUNK, GLA_DK), F32), wn_shape],
        scratch_shapes=[
            pltpu.VMEM((n, d), BF16),
            pltpu.VMEM((INPROJ_SLAB // W_CHUNK, W_CHUNK, d), F32),
            pltpu.VMEM((tm, GLA_DK), BF16),
            pltpu.SemaphoreType.DMA((INPROJ_SLAB // W_CHUNK,)),
        ],
        compiler_params=pltpu.CompilerParams(
            dimension_semantics=("arbitrary",), vmem_limit_bytes=VMEM_LIMIT_LARGE),
        name="gla_inproj",
    )(x2, gains, w_t, w_t, w2, b_gate, o_gain, w_next)


def _sgu_inproj_kernel(x_ref, gain_ref, w_ref, wn_ref, lng_ref, lnb_ref, ws_ref, b_ref,
                       z_ref, wnb_ref, wsm_ref, bs_ref, va_ref, vn_ref, tg_ref):
    @pl.when(pl.program_id(0) == 0)
    def _():
        ri = lax.broadcasted_iota(jnp.int32, (SGU_BLOCK, SGU_BLOCK), 0) // CHUNK
        ci = lax.broadcasted_iota(jnp.int32, (SGU_BLOCK, SGU_BLOCK), 1) // CHUNK
        for gi in range(SGU_GROUPS):
            wsm_ref[gi] = jnp.where(ri >= ci, ws_ref[gi], 0.0).astype(BF16)
            col = jnp.transpose(jnp.broadcast_to(b_ref[gi:gi + 1, :], (SGU_BLOCK, SGU_BLOCK)))
            for t in range(SGU_GD // LANES):
                c0 = gi * SGU_GD + t * LANES
                bs_ref[:, c0:c0 + LANES] = col

    wnb_ref[...] = wn_ref[...].astype(BF16)
    h = _rms(x_ref[...], gain_ref[SGU_LAYER:SGU_LAYER + 1, :]).astype(BF16)
    tm = h.shape[0]
    slabs = SGU_WIDTH // INPROJ_SLAB

    def proj(n):
        return jnp.dot(h, w_ref[n], preferred_element_type=F32)

    for n in range(slabs):
        va_ref[:, n * INPROJ_SLAB:(n + 1) * INPROJ_SLAB] = _gelu(proj(slabs + n)).astype(BF16)
    later = []
    for n in range(slabs):
        r = proj(n)
        later.append(r)
        tg_ref[:, n * INPROJ_SLAB:(n + 1) * INPROJ_SLAB] = _gelu(r).astype(BF16)
    for n in range(slabs):
        cols = slice(n * INPROJ_SLAB, (n + 1) * INPROJ_SLAB)
        r = proj(2 * slabs + n)
        later.append(r)
        tg_ref[:, cols] = (_silu(r) * tg_ref[:, cols].astype(F32)).astype(BF16)
    groups = tm // LN_ROWS
    for r in range(groups):
        rows = slice(r * LN_ROWS, (r + 1) * LN_ROWS)
        zero = _zero_from(later[r * LN_SPREAD // groups][rows, 0:LANES])
        zero = jnp.concatenate([zero] * (SGU_WIDTH // LANES), axis=1)
        v = (va_ref[rows, :] + zero).astype(F32)
        vc = v - jnp.mean(v, axis=-1, keepdims=True)
        var = jnp.mean(vc * vc, axis=-1, keepdims=True)
        vn_ref[rows, :] = (vc * lax.rsqrt(var + EPS) * lng_ref[...] + lnb_ref[...]).astype(BF16)
    for nb in range(tm // SGU_BLOCK):
        rows = slice(nb * SGU_BLOCK, (nb + 1) * SGU_BLOCK)
        for gi in range(SGU_GROUPS):
            cols = slice(gi * SGU_GD, (gi + 1) * SGU_GD)
            vs = (jnp.dot(wsm_ref[gi], vn_ref[rows, cols], preferred_element_type=F32)
                  + bs_ref[:, cols])
            z_ref[rows, cols] = (vs * tg_ref[rows, cols].astype(F32)).astype(BF16)


def _sgu_inproj(x2, gains, w, w_next, ln_gain, ln_bias, w_spatial, b_spatial):
    t, d = x2.shape
    tm = INPROJ_TM
    steps = t // tm
    resident = pl.Buffered(1)
    const2 = lambda i: (0, 0)
    wn_in, wn_out, wn_shape = _cast_specs(w_next, lambda i: (i, 0), steps)
    return pl.pallas_call(
        _sgu_inproj_kernel,
        grid=(steps,),
        in_specs=[
            pl.BlockSpec((tm, d), lambda i: (i, 0)),
            pl.BlockSpec(gains.shape, const2),
            pl.BlockSpec(w.shape, lambda i: (0, 0, 0), pipeline_mode=resident),
            wn_in,
            pl.BlockSpec((1, SGU_WIDTH), const2),
            pl.BlockSpec((1, SGU_WIDTH), const2),
            pl.BlockSpec((SGU_GROUPS, SGU_BLOCK, SGU_BLOCK), lambda i: (0, 0, 0),
                         pipeline_mode=resident),
            pl.BlockSpec((SGU_GROUPS, SGU_BLOCK), const2, pipeline_mode=resident),
        ],
        out_specs=[pl.BlockSpec((tm, SGU_WIDTH), lambda i: (i, 0)), wn_out],
        out_shape=[jax.ShapeDtypeStruct((t, SGU_WIDTH), BF16), wn_shape],
        scratch_shapes=[
            pltpu.VMEM((SGU_GROUPS, SGU_BLOCK, SGU_BLOCK), BF16),
            pltpu.VMEM((SGU_BLOCK, SGU_WIDTH), F32),
            pltpu.VMEM((tm, SGU_WIDTH), BF16),
            pltpu.VMEM((tm, SGU_WIDTH), BF16),
            pltpu.VMEM((tm, SGU_WIDTH), BF16),
        ],
        compiler_params=pltpu.CompilerParams(
            dimension_semantics=("arbitrary",), vmem_limit_bytes=VMEM_LIMIT),
        name="sgu_inproj_mix",
    )(x2, gains, w, w_next, ln_gain, ln_bias, w_spatial, b_spatial)


def _gla_kernel(q_ref, k_ref, v_ref, g_ref, dec_ref, wo_ref, gain_ref, x_ref, wn_ref,
                o_ref, wnb_ref, s_ref, a_ref):
    @pl.when(pl.program_id(1) == 0)
    def _():
        s_ref[...] = jnp.zeros_like(s_ref)

    for n in range(wnb_ref.shape[0]):
        wnb_ref[n] = wn_ref[:, n * INPROJ_SLAB:(n + 1) * INPROJ_SLAB].astype(BF16)
    n_chunks = q_ref.shape[1] // CHUNK

    def chunk(c, carry):
        r0 = pl.multiple_of(c * CHUNK, CHUNK)
        rows = pl.ds(r0, CHUNK)
        dec = dec_ref[0, pl.ds(c, 1), :]
        for h in range(GLA_HEADS):
            kc = slice(h * GLA_DKH, (h + 1) * GLA_DKH)
            vc = slice(h * GLA_DVH, (h + 1) * GLA_DVH)
            upd = lax.dot_general(k_ref[0, rows, kc], v_ref[0, rows, vc], (((0,), (0,)), ((), ())),
                                  preferred_element_type=F32)
            decay_col = jnp.transpose(jnp.broadcast_to(dec[:, kc], (LANES, GLA_DKH)))
            for t in range(GLA_DVH // LANES):
                cols = slice(t * LANES, (t + 1) * LANES)
                s_new = s_ref[h, :, cols] * decay_col + upd[:, cols]
                s_ref[h, :, cols] = s_new
        for h in range(GLA_HEADS):
            kc = slice(h * GLA_DKH, (h + 1) * GLA_DKH)
            vc = slice(h * GLA_DVH, (h + 1) * GLA_DVH)
            o = jnp.dot(q_ref[0, rows, kc], s_ref[h].astype(BF16),
                        preferred_element_type=F32)
            ms = jnp.mean(o * o, axis=-1, keepdims=True) * (1.0 / GLA_DKH)
            scale = lax.rsqrt(ms + EPS) * (GLA_DKH ** -0.5)
            a_ref[rows, vc] = (o * scale * g_ref[0, rows, vc].astype(F32)).astype(a_ref.dtype)
        return carry

    lax.fori_loop(0, n_chunks, chunk, 0, unroll=True)

    y = jnp.dot(a_ref[...], wo_ref[...], preferred_element_type=F32)
    o_ref[0] = x_ref[0] + _rms(y, gain_ref[GLA_LAYER:GLA_LAYER + 1, :])


def _gla_scan_outproj(proj3, dec3, w_out, gains, x3, w_next):
    b, s, d = x3.shape
    tile = GLA_TILE
    tiles = s // tile
    const2 = lambda i, t: (0, 0)
    w_rows, w_cols = w_next.shape
    wn_rows = w_rows // (b * tiles)
    wn_in = pl.BlockSpec((wn_rows, w_cols), lambda i, t: (i * tiles + t, 0))
    wn_out = pl.BlockSpec((w_cols // INPROJ_SLAB, wn_rows, INPROJ_SLAB),
                          lambda i, t: (0, i * tiles + t, 0))
    wn_shape = jax.ShapeDtypeStruct((w_cols // INPROJ_SLAB, w_rows, INPROJ_SLAB), BF16)
    in_specs = [
        pl.BlockSpec((1, tile, GLA_DK), lambda i, t: (i, t, 0)),
        pl.BlockSpec((1, tile, GLA_DK), lambda i, t: (i, t, 1)),
        pl.BlockSpec((1, tile, GLA_DV), lambda i, t: (i, t, 1)),
        pl.BlockSpec((1, tile, GLA_DV), lambda i, t: (i, t, 2)),
        pl.BlockSpec((1, tile // CHUNK, GLA_DK), lambda i, t: (i, t, 0)),
        pl.BlockSpec((GLA_DV, d), const2, pipeline_mode=pl.Buffered(1)),
        pl.BlockSpec(gains.shape, const2),
        pl.BlockSpec((1, tile, d), lambda i, t: (i, t, 0)),
        wn_in,
    ]
    return pl.pallas_call(
        _gla_kernel,
        grid=(b, tiles),
        in_specs=in_specs,
        out_specs=[pl.BlockSpec((1, tile, d), lambda i, t: (i, t, 0)), wn_out],
        out_shape=[jax.ShapeDtypeStruct((b, s, d), F32), wn_shape],
        scratch_shapes=[
            pltpu.VMEM((GLA_HEADS, GLA_DKH, GLA_DVH), F32),
            pltpu.VMEM((tile, GLA_DV), BF16),
        ],
        compiler_params=pltpu.CompilerParams(
            dimension_semantics=("arbitrary", "arbitrary"), vmem_limit_bytes=VMEM_LIMIT_LARGE),
        name="gla_scan_outproj",
    )(proj3, proj3, proj3, proj3, dec3, w_out, gains, x3, w_next)


def _sgu_outproj_kernel(a_ref, w_ref, gain_ref, x_ref, o_ref):
    y = jnp.dot(a_ref[...], w_ref[...], preferred_element_type=F32)
    o_ref[...] = x_ref[...] + _rms(y, gain_ref[SGU_LAYER:SGU_LAYER + 1, :])


def _sgu_outproj(a2, w, gains, x2):
    t, d = x2.shape
    k = a2.shape[1]
    tm = OUT_TM
    return pl.pallas_call(
        _sgu_outproj_kernel,
        grid=(t // tm,),
        in_specs=[
            pl.BlockSpec((tm, k), lambda i: (i, 0)),
            pl.BlockSpec((k, d), lambda i: (0, 0), pipeline_mode=pl.Buffered(1)),
            pl.BlockSpec(gains.shape, lambda i: (0, 0)),
            pl.BlockSpec((tm, d), lambda i: (i, 0)),
        ],
        out_specs=pl.BlockSpec((tm, d), lambda i: (i, 0)),
        out_shape=jax.ShapeDtypeStruct((t, d), F32),
        compiler_params=pltpu.CompilerParams(
            dimension_semantics=("arbitrary",), vmem_limit_bytes=VMEM_LIMIT),
        name="sgu_outproj",
    )(a2, w, gains, x2)


def kernel(x, norm_pre, norm_post, gla_w_in, gla_w_gate2, gla_b_gate, gla_o_gain, gla_w_out,
           sgu_w_in, sgu_ln_gain, sgu_ln_bias, sgu_w_spatial, sgu_b_spatial, sgu_w_out):
    b, s, d = x.shape
    t = b * s
    x2 = x.reshape(t, d)

    w_in_t = gla_w_in.reshape(d, -1).T
    proj, dec, gla_w_out_b = _gla_inproj(x2, norm_pre, w_in_t, gla_w_gate2.reshape(GLA_RANK, GLA_DK),
                                         gla_b_gate, gla_o_gain, gla_w_out.reshape(GLA_DV, d))
    x3, sgu_w_in_b = _gla_scan_outproj(
        proj.reshape(b, s, GLA_MAIN), dec.reshape(b, s // CHUNK, GLA_DK),
        gla_w_out_b, norm_post, x, sgu_w_in.reshape(d, 3 * SGU_WIDTH))
    x2 = x3.reshape(t, d)

    z, sgu_w_out_b = _sgu_inproj(
        x2, norm_pre, sgu_w_in_b, sgu_w_out.reshape(SGU_WIDTH, d), sgu_ln_gain, sgu_ln_bias,
        sgu_w_spatial.reshape(SGU_GROUPS, SGU_BLOCK, SGU_BLOCK),
        sgu_b_spatial.reshape(SGU_GROUPS, SGU_BLOCK))
    x2 = _sgu_outproj(z, sgu_w_out_b, norm_post, x2)
    return x2.reshape(b, s, d)
```

```python
import jax
import jax.numpy as jnp
from jax import lax
from jax.experimental import pallas as pl
from jax.experimental.pallas import tpu as pltpu

F32 = jnp.float32
BF16 = jnp.bfloat16

D_MODEL = 2048
EPS = 1e-6
CHUNK = 64

GLA_LAYER = 0
SGU_LAYER = 1

GLA_HEADS = 4
GLA_DK = D_MODEL // 2
GLA_DV = D_MODEL
GLA_DKH = GLA_DK // GLA_HEADS
GLA_DVH = GLA_DV // GLA_HEADS
GLA_RANK = 16
GLA_INV_TAU = 1.0 / 16.0
GLA_MAIN = 2 * GLA_DK + 2 * GLA_DV

SGU_WIDTH = D_MODEL
SGU_BLOCK = 128
SGU_GROUPS = 8
SGU_GD = SGU_WIDTH // SGU_GROUPS

LANES = 128
VMEM_LIMIT = 56 * 1024 * 1024
VMEM_LIMIT_LARGE = 60 * 1024 * 1024

INPROJ_TM = 512
INPROJ_SLAB = 512
OUT_TM = 512
LN_ROWS = 16
LN_SPREAD = 4
GLA_TILE = 512
SCAN_LAG = 2
GLA_TRI = 256
W_CHUNK = 128

GELU_C1 = (2.0 / 3.141592653589793) ** 0.5
GELU_C3 = GELU_C1 * 0.044715


def _rms(x, gain):
    return x * lax.rsqrt(jnp.mean(x * x, axis=-1, keepdims=True) + EPS) * gain


def _gelu(r):
    return (0.5 * r) * (1.0 + jnp.tanh(r * (GELU_C1 + GELU_C3 * (r * r))))


def _silu(r):
    hr = 0.5 * r
    return hr * (1.0 + jnp.tanh(hr))


def _zero_from(anchor):
    bits = anchor.astype(jnp.int32)
    return lax.shift_right_logical(lax.shift_right_logical(bits, 16), 16).astype(BF16)


def _dot_nt(a, b_t):
    return lax.dot_general(a, b_t, (((1,), (1,)), ((), ())), preferred_element_type=F32)


def _cast_specs(w_next, index_map, steps):
    rows, cols = w_next.shape
    spec = pl.BlockSpec((rows // steps, cols), index_map)
    return spec, spec, jax.ShapeDtypeStruct((rows, cols), BF16)


def _gla_inproj_kernel(x_ref, gain_ref, w_hbm, wlr_ref, w2_ref, bg_ref, og_ref, wn_ref,
                       o_ref, dec_ref, wnb_ref, w_ref, stage_ref, e_ref, sem):
    per_slab = INPROJ_SLAB // W_CHUNK
    q_slabs = GLA_DK // INPROJ_SLAB
    v_slabs = GLA_DV // INPROJ_SLAB
    q_ids = list(range(q_slabs))
    k_ids = list(range(q_slabs, 2 * q_slabs))
    v_ids = list(range(2 * q_slabs, 2 * q_slabs + v_slabs))
    g_ids = list(range(2 * q_slabs + v_slabs, 2 * q_slabs + 2 * v_slabs))
    order = q_ids + v_ids[:-1] + k_ids + v_ids[-1:] + g_ids
    n_chunks = len(order) * per_slab

    def chunk_copy(s):
        row0 = order[s // per_slab] * INPROJ_SLAB + (s % per_slab) * W_CHUNK
        slot = s % per_slab
        return pltpu.make_async_copy(w_hbm.at[pl.ds(row0, W_CHUNK), :],
                                     stage_ref.at[slot], sem.at[slot])

    def body(stream_weight):
        wnb_ref[...] = wn_ref[...].astype(BF16)
        h = _rms(x_ref[...], gain_ref[GLA_LAYER:GLA_LAYER + 1, :]).astype(BF16)
        tm = h.shape[0]

        def proj(p):
            n = order[p]
            if stream_weight:
                for s in range(p * per_slab, (p + 1) * per_slab):
                    row0 = n * INPROJ_SLAB + (s % per_slab) * W_CHUNK
                    chunk_copy(s).wait()
                    w_ref[row0:row0 + W_CHUNK, :] = stage_ref[s % per_slab].astype(BF16)
                    if s + per_slab < n_chunks:
                        chunk_copy(s + per_slab).start()
            return _dot_nt(h, w_ref[n * INPROJ_SLAB:(n + 1) * INPROJ_SLAB, :])

        def store(n, r):
            o_ref[:, n * INPROJ_SLAB:(n + 1) * INPROJ_SLAB] = r.astype(o_ref.dtype)

        def chunk_decays(r, hi, lo):
            rs = slice(r * GLA_TRI, (r + 1) * GLA_TRI)
            bcum = (jnp.dot(tri, hi[rs], preferred_element_type=F32)
                    + jnp.dot(tri, lo[rs], preferred_element_type=F32))
            for cc in range(GLA_TRI // CHUNK):
                c = r * (GLA_TRI // CHUNK) + cc
                bc = bcum[cc * CHUNK:(cc + 1) * CHUNK]
                b_end = bc[CHUNK - 1:CHUNK, :]
                e_ref[c * CHUNK:(c + 1) * CHUNK, :] = jnp.exp(b_end - bc).astype(BF16)
                dec_ref[c:c + 1, :] = jnp.exp(b_end)

        ri = lax.broadcasted_iota(jnp.int32, (GLA_TRI, GLA_TRI), 0)
        ci = lax.broadcasted_iota(jnp.int32, (GLA_TRI, GLA_TRI), 1)
        tri = jnp.where((ri >= ci) & (ri // CHUNK == ci // CHUNK), 1.0, 0.0).astype(BF16)

        p = 0
        lr = _dot_nt(h, wlr_ref[...].astype(BF16))
        store(order[p], proj(p)); p += 1
        z = jnp.dot(lr.astype(BF16), w2_ref[...].astype(BF16),
                    preferred_element_type=F32) + bg_ref[...]
        for _ in range(1, q_slabs):
            store(order[p], proj(p)); p += 1
        la = (jnp.minimum(z, 0.0) - jnp.log(1.0 + jnp.exp(-jnp.abs(z)))) * GLA_INV_TAU
        hi = la.astype(BF16)
        lo = (la - hi.astype(F32)).astype(BF16)
        for r in range(tm // GLA_TRI):
            store(order[p], proj(p)); p += 1
            chunk_decays(r, hi, lo)
        while order[p] not in k_ids:
            store(order[p], proj(p)); p += 1
        for j in range(q_slabs):
            kc = slice(j * INPROJ_SLAB, (j + 1) * INPROJ_SLAB)
            store(order[p], proj(p) * e_ref[:, kc].astype(F32)); p += 1
        while p < len(order):
            r = proj(p)
            if order[p] in g_ids:
                gc = slice((order[p] - g_ids[0]) * INPROJ_SLAB, (order[p] - g_ids[0] + 1) * INPROJ_SLAB)
                r = _silu(r) * og_ref[:, gc]
            store(order[p], r); p += 1

    @pl.when(pl.program_id(0) == 0)
    def _():
        for s in range(per_slab):
            chunk_copy(s).start()
        body(True)

    @pl.when(pl.program_id(0) > 0)
    def _():
        body(False)


def _gla_inproj(x2, gains, w_t, w2, b_gate, o_gain, w_next):
    t, d = x2.shape
    n = GLA_MAIN
    tm = INPROJ_TM
    steps = t // tm
    resident = pl.Buffered(1)
    const2 = lambda i: (0, 0)
    wn_in, wn_out, wn_shape = _cast_specs(w_next, lambda i: (i, 0), steps)
    return pl.pallas_call(
        _gla_inproj_kernel,
        grid=(steps,),
        in_specs=[
            pl.BlockSpec((tm, d), lambda i: (i, 0)),
            pl.BlockSpec(gains.shape, const2),
            pl.BlockSpec(memory_space=pl.ANY),
            pl.BlockSpec((GLA_RANK, d), lambda i: (GLA_MAIN // GLA_RANK, 0), pipeline_mode=resident),
            pl.BlockSpec((GLA_RANK, GLA_DK), const2, pipeline_mode=resident),
            pl.BlockSpec((1, GLA_DK), const2),
            pl.BlockSpec((1, GLA_DV), const2),
            wn_in,
        ],
        out_specs=[pl.BlockSpec((tm, n), lambda i: (i, 0)),
                   pl.BlockSpec((tm // CHUNK, GLA_DK), lambda i: (i, 0)), wn_out],
        out_shape=[jax.ShapeDtypeStruct((t, n), BF16),
                   jax.ShapeDtypeStruct((t // CHUNK, GLA_DK), F32), wn_shape],
        scratch_shapes=[
            pltpu.VMEM((n, d), BF16),
            pltpu.VMEM((INPROJ_SLAB // W_CHUNK, W_CHUNK, d), F32),
            pltpu.VMEM((tm, GLA_DK), BF16),
            pltpu.SemaphoreType.DMA((INPROJ_SLAB // W_CHUNK,)),
        ],
        compiler_params=pltpu.CompilerParams(
            dimension_semantics=("arbitrary",), vmem_limit_bytes=VMEM_LIMIT_LARGE),
        name="gla_inproj",
    )(x2, gains, w_t, w_t, w2, b_gate, o_gain, w_next)


def _sgu_inproj_kernel(x_ref, gain_ref, w_ref, wn_ref, lng_ref, lnb_ref, ws_ref, b_ref,
                       z_ref, wnb_ref, wsm_ref, bs_ref, va_ref, vn_ref, tg_ref):
    @pl.when(pl.program_id(0) == 0)
    def _():
        ri = lax.broadcasted_iota(jnp.int32, (SGU_BLOCK, SGU_BLOCK), 0) // CHUNK
        ci = lax.broadcasted_iota(jnp.int32, (SGU_BLOCK, SGU_BLOCK), 1) // CHUNK
        for gi in range(SGU_GROUPS):
            wsm_ref[gi] = jnp.where(ri >= ci, ws_ref[gi], 0.0).astype(BF16)
            col = jnp.transpose(jnp.broadcast_to(b_ref[gi:gi + 1, :], (SGU_BLOCK, SGU_BLOCK)))
            for t in range(SGU_GD // LANES):
                c0 = gi * SGU_GD + t * LANES
                bs_ref[:, c0:c0 + LANES] = col

    wnb_ref[...] = wn_ref[...].astype(BF16)
    h = _rms(x_ref[...], gain_ref[SGU_LAYER:SGU_LAYER + 1, :]).astype(BF16)
    tm = h.shape[0]
    slabs = SGU_WIDTH // INPROJ_SLAB

    def proj(n):
        return jnp.dot(h, w_ref[n], preferred_element_type=F32)

    for n in range(slabs):
        va_ref[:, n * INPROJ_SLAB:(n + 1) * INPROJ_SLAB] = _gelu(proj(slabs + n)).astype(BF16)
    later = []
    for n in range(slabs):
        r = proj(n)
        later.append(r)
        tg_ref[:, n * INPROJ_SLAB:(n + 1) * INPROJ_SLAB] = _gelu(r).astype(BF16)
    for n in range(slabs):
        cols = slice(n * INPROJ_SLAB, (n + 1) * INPROJ_SLAB)
        r = proj(2 * slabs + n)
        later.append(r)
        tg_ref[:, cols] = (_silu(r) * tg_ref[:, cols].astype(F32)).astype(BF16)
    groups = tm // LN_ROWS
    for r in range(groups):
        rows = slice(r * LN_ROWS, (r + 1) * LN_ROWS)
        zero = _zero_from(later[r * LN_SPREAD // groups][rows, 0:LANES])
        zero = jnp.concatenate([zero] * (SGU_WIDTH // LANES), axis=1)
        v = (va_ref[rows, :] + zero).astype(F32)
        vc = v - jnp.mean(v, axis=-1, keepdims=True)
        var = jnp.mean(vc * vc, axis=-1, keepdims=True)
        vn_ref[rows, :] = (vc * lax.rsqrt(var + EPS) * lng_ref[...] + lnb_ref[...]).astype(BF16)
    for nb in range(tm // SGU_BLOCK):
        rows = slice(nb * SGU_BLOCK, (nb + 1) * SGU_BLOCK)
        for gi in range(SGU_GROUPS):
            cols = slice(gi * SGU_GD, (gi + 1) * SGU_GD)
            vs = (jnp.dot(wsm_ref[gi], vn_ref[rows, cols], preferred_element_type=F32)
                  + bs_ref[:, cols])
            z_ref[rows, cols] = (vs * tg_ref[rows, cols].astype(F32)).astype(BF16)


def _sgu_inproj(x2, gains, w, w_next, ln_gain, ln_bias, w_spatial, b_spatial):
    t, d = x2.shape
    tm = INPROJ_TM
    steps = t // tm
    resident = pl.Buffered(1)
    const2 = lambda i: (0, 0)
    wn_in, wn_out, wn_shape = _cast_specs(w_next, lambda i: (i, 0), steps)
    return pl.pallas_call(
        _sgu_inproj_kernel,
        grid=(steps,),
        in_specs=[
            pl.BlockSpec((tm, d), lambda i: (i, 0)),
            pl.BlockSpec(gains.shape, const2),
            pl.BlockSpec(w.shape, lambda i: (0, 0, 0), pipeline_mode=resident),
            wn_in,
            pl.BlockSpec((1, SGU_WIDTH), const2),
            pl.BlockSpec((1, SGU_WIDTH), const2),
            pl.BlockSpec((SGU_GROUPS, SGU_BLOCK, SGU_BLOCK), lambda i: (0, 0, 0),
                         pipeline_mode=resident),
            pl.BlockSpec((SGU_GROUPS, SGU_BLOCK), const2, pipeline_mode=resident),
        ],
        out_specs=[pl.BlockSpec((tm, SGU_WIDTH), lambda i: (i, 0)), wn_out],
        out_shape=[jax.ShapeDtypeStruct((t, SGU_WIDTH), BF16), wn_shape],
        scratch_shapes=[
            pltpu.VMEM((SGU_GROUPS, SGU_BLOCK, SGU_BLOCK), BF16),
            pltpu.VMEM((SGU_BLOCK, SGU_WIDTH), F32),
            pltpu.VMEM((tm, SGU_WIDTH), BF16),
            pltpu.VMEM((tm, SGU_WIDTH), BF16),
            pltpu.VMEM((tm, SGU_WIDTH), BF16),
        ],
        compiler_params=pltpu.CompilerParams(
            dimension_semantics=("arbitrary",), vmem_limit_bytes=VMEM_LIMIT),
        name="sgu_inproj_mix",
    )(x2, gains, w, w_next, ln_gain, ln_bias, w_spatial, b_spatial)


def _gla_kernel(q_ref, k_ref, v_ref, g_ref, dec_ref, wo_ref, gain_ref, x_ref, wn_ref,
                o_ref, wnb_ref, s_ref, a_ref):
    @pl.when(pl.program_id(1) == 0)
    def _():
        s_ref[...] = jnp.zeros_like(s_ref)

    for n in range(wnb_ref.shape[0]):
        wnb_ref[n] = wn_ref[:, n * INPROJ_SLAB:(n + 1) * INPROJ_SLAB].astype(BF16)
    n_chunks = q_ref.shape[1] // CHUNK

    def absorb(c, h):
        rows = slice(c * CHUNK, (c + 1) * CHUNK)
        kc = slice(h * GLA_DKH, (h + 1) * GLA_DKH)
        vc = slice(h * GLA_DVH, (h + 1) * GLA_DVH)
        dec = dec_ref[0, c:c + 1, kc]
        upd = lax.dot_general(k_ref[0, rows, kc], v_ref[0, rows, vc], (((0,), (0,)), ((), ())),
                              preferred_element_type=F32)
        decay_col = jnp.transpose(jnp.broadcast_to(dec, (LANES, GLA_DKH)))
        for t in range(GLA_DVH // LANES):
            cols = slice(t * LANES, (t + 1) * LANES)
            s_new = s_ref[h, :, cols] * decay_col + upd[:, cols]
            s_ref[h, :, cols] = s_new

    def read(c, h):
        rows = slice(c * CHUNK, (c + 1) * CHUNK)
        kc = slice(h * GLA_DKH, (h + 1) * GLA_DKH)
        vc = slice(h * GLA_DVH, (h + 1) * GLA_DVH)
        o = jnp.dot(q_ref[0, rows, kc], s_ref[h].astype(BF16),
                    preferred_element_type=F32)
        ms = jnp.mean(o * o, axis=-1, keepdims=True) * (1.0 / GLA_DKH)
        scale = lax.rsqrt(ms + EPS) * (GLA_DKH ** -0.5)
        a_ref[rows, vc] = (o * scale * g_ref[0, rows, vc].astype(F32)).astype(a_ref.dtype)

    seq = [(c, h) for c in range(n_chunks) for h in range(GLA_HEADS)]
    for i, (c, h) in enumerate(seq):
        absorb(c, h)
        if i >= SCAN_LAG:
            read(*seq[i - SCAN_LAG])
    for ch in seq[len(seq) - SCAN_LAG:]:
        read(*ch)

    y = jnp.dot(a_ref[...], wo_ref[...], preferred_element_type=F32)
    o_ref[0] = x_ref[0] + _rms(y, gain_ref[GLA_LAYER:GLA_LAYER + 1, :])


def _gla_scan_outproj(proj3, dec3, w_out, gains, x3, w_next):
    b, s, d = x3.shape
    tile = GLA_TILE
    tiles = s // tile
    const2 = lambda i, t: (0, 0)
    w_rows, w_cols = w_next.shape
    wn_rows = w_rows // (b * tiles)
    wn_in = pl.BlockSpec((wn_rows, w_cols), lambda i, t: (i * tiles + t, 0))
    wn_out = pl.BlockSpec((w_cols // INPROJ_SLAB, wn_rows, INPROJ_SLAB),
                          lambda i, t: (0, i * tiles + t, 0))
    wn_shape = jax.ShapeDtypeStruct((w_cols // INPROJ_SLAB, w_rows, INPROJ_SLAB), BF16)
    in_specs = [
        pl.BlockSpec((1, tile, GLA_DK), lambda i, t: (i, t, 0)),
        pl.BlockSpec((1, tile, GLA_DK), lambda i, t: (i, t, 1)),
        pl.BlockSpec((1, tile, GLA_DV), lambda i, t: (i, t, 1)),
        pl.BlockSpec((1, tile, GLA_DV), lambda i, t: (i, t, 2)),
        pl.BlockSpec((1, tile // CHUNK, GLA_DK), lambda i, t: (i, t, 0)),
        pl.BlockSpec((GLA_DV, d), const2, pipeline_mode=pl.Buffered(1)),
        pl.BlockSpec(gains.shape, const2),
        pl.BlockSpec((1, tile, d), lambda i, t: (i, t, 0)),
        wn_in,
    ]
    return pl.pallas_call(
        _gla_kernel,
        grid=(b, tiles),
        in_specs=in_specs,
        out_specs=[pl.BlockSpec((1, tile, d), lambda i, t: (i, t, 0)), wn_out],
        out_shape=[jax.ShapeDtypeStruct((b, s, d), F32), wn_shape],
        scratch_shapes=[
            pltpu.VMEM((GLA_HEADS, GLA_DKH, GLA_DVH), F32),
            pltpu.VMEM((tile, GLA_DV), BF16),
        ],
        compiler_params=pltpu.CompilerParams(
            dimension_semantics=("arbitrary", "arbitrary"), vmem_limit_bytes=VMEM_LIMIT_LARGE),
        name="gla_scan_outproj",
    )(proj3, proj3, proj3, proj3, dec3, w_out, gains, x3, w_next)


def _sgu_outproj_kernel(a_ref, w_ref, gain_ref, x_ref, o_ref):
    y = jnp.dot(a_ref[...], w_ref[...], preferred_element_type=F32)
    o_ref[...] = x_ref[...] + _rms(y, gain_ref[SGU_LAYER:SGU_LAYER + 1, :])


def _sgu_outproj(a2, w, gains, x2):
    t, d = x2.shape
    k = a2.shape[1]
    tm = OUT_TM
    return pl.pallas_call(
        _sgu_outproj_kernel,
        grid=(t // tm,),
        in_specs=[
            pl.BlockSpec((tm, k), lambda i: (i, 0)),
            pl.BlockSpec((k, d), lambda i: (0, 0), pipeline_mode=pl.Buffered(1)),
            pl.BlockSpec(gains.shape, lambda i: (0, 0)),
            pl.BlockSpec((tm, d), lambda i: (i, 0)),
        ],
        out_specs=pl.BlockSpec((tm, d), lambda i: (i, 0)),
        out_shape=jax.ShapeDtypeStruct((t, d), F32),
        compiler_params=pltpu.CompilerParams(
            dimension_semantics=("arbitrary",), vmem_limit_bytes=VMEM_LIMIT),
        name="sgu_outproj",
    )(a2, w, gains, x2)


def kernel(x, norm_pre, norm_post, gla_w_in, gla_w_gate2, gla_b_gate, gla_o_gain, gla_w_out,
           sgu_w_in, sgu_ln_gain, sgu_ln_bias, sgu_w_spatial, sgu_b_spatial, sgu_w_out):
    b, s, d = x.shape
    t = b * s
    x2 = x.reshape(t, d)

    w_in_t = gla_w_in.reshape(d, -1).T
    proj, dec, gla_w_out_b = _gla_inproj(x2, norm_pre, w_in_t, gla_w_gate2.reshape(GLA_RANK, GLA_DK),
                                         gla_b_gate, gla_o_gain, gla_w_out.reshape(GLA_DV, d))
    x3, sgu_w_in_b = _gla_scan_outproj(
        proj.reshape(b, s, GLA_MAIN), dec.reshape(b, s // CHUNK, GLA_DK),
        gla_w_out_b, norm_post, x, sgu_w_in.reshape(d, 3 * SGU_WIDTH))
    x2 = x3.reshape(t, d)

    z, sgu_w_out_b = _sgu_inproj(
        x2, norm_pre, sgu_w_in_b, sgu_w_out.reshape(SGU_WIDTH, d), sgu_ln_gain, sgu_ln_bias,
        sgu_w_spatial.reshape(SGU_GROUPS, SGU_BLOCK, SGU_BLOCK),
        sgu_b_spatial.reshape(SGU_GROUPS, SGU_BLOCK))
    x2 = _sgu_outproj(z, sgu_w_out_b, norm_post, x2)
    return x2.reshape(b, s, d)
```
